```python
import jax, jax.numpy as jnp
from jax import lax
import numpy as np

D_MODEL = 1024
BATCH = 8
SEQ = 2048
DEPTH = 2

MLA_HEADS = 8
QK_NOPE_DIM = 64
QK_ROPE_DIM = 32
QK_HEAD_DIM = QK_NOPE_DIM + QK_ROPE_DIM
V_HEAD_DIM = 64
Q_LORA_RANK = 384
KV_LORA_RANK = 256
ROPE_THETA = 10000.0
Q_BLOCK = 128
RWKV_HEAD_DIM = 64
RWKV_HEADS = 4
RWKV_WIDTH = RWKV_HEADS * RWKV_HEAD_DIM
DECAY_LORA = 64
AAA_LORA = 64
GATE_LORA = 128
MV_LORA = 32
GN_EPS = 64e-5
CONV_WIDTH = 256
CONV_K = 3
D_FF = 4 * D_MODEL
N_BRANCH = 3
NORM_EPS = 1e-6

GATE_COLS = N_BRANCH * D_MODEL
MLA_COLS = Q_LORA_RANK + KV_LORA_RANK + QK_ROPE_DIM
RWKV_COLS = 3 * RWKV_WIDTH + DECAY_LORA + AAA_LORA + GATE_LORA
CONV_COLS = 3 * CONV_WIDTH
IN_COLS = GATE_COLS + MLA_COLS + RWKV_COLS + CONV_COLS

kernel_name = 'hybrid_mla_rwkv7_shortconv_block'


def _split(x, sizes):
    idx = np.cumsum(sizes)[:-1].tolist()
    return jnp.split(x, idx, axis=-1)


def rms_norm(x, g, eps=NORM_EPS):
    xf = x.astype(jnp.float32)
    y = xf * lax.rsqrt(jnp.mean(xf * xf, axis=-1, keepdims=True) + eps)
    return (y * g.astype(jnp.float32)).astype(x.dtype)


def token_shift(x):
    return jnp.pad(x, ((0, 0), (1, 0), (0, 0)))[:, :-1]


def rope_tables(positions):
    freqs = ROPE_THETA ** (-(jnp.arange(QK_ROPE_DIM // 2, dtype=jnp.float32) * 2.0 / QK_ROPE_DIM))
    ang = positions.astype(jnp.float32)[..., None] * freqs
    return jnp.cos(ang)[:, :, None, :], jnp.sin(ang)[:, :, None, :]


def apply_rope(x, cos, sin):
    x1, x2 = x[..., :QK_ROPE_DIM // 2], x[..., QK_ROPE_DIM // 2:]
    cos, sin = cos.astype(x.dtype), sin.astype(x.dtype)
    return jnp.concatenate([x1 * cos - x2 * sin, x1 * sin + x2 * cos], axis=-1)


def causal_block_attention(q, k, v):
    scale = QK_HEAD_DIM ** -0.5
    qh, kh, vh = (jnp.swapaxes(t, 1, 2) for t in (q, k, v))
    seq = qh.shape[2]
    outs = []
    for start in range(0, seq, Q_BLOCK):
        stop = start + Q_BLOCK
        s = jnp.einsum('bhqd,bhkd->bhqk', qh[:, :, start:stop], kh[:, :, :stop]).astype(jnp.float32) * scale
        causal = (start + jnp.arange(Q_BLOCK))[:, None] >= jnp.arange(stop)[None, :]
        p = jax.nn.softmax(jnp.where(causal, s, -jnp.inf), axis=-1)
        outs.append(jnp.einsum('bhqk,bhkd->bhqd', p.astype(vh.dtype), vh[:, :, :stop]))
    return jnp.swapaxes(jnp.concatenate(outs, axis=2), 1, 2)


def mla_branch(cols, positions, q_a_norm, wq_b, kv_a_norm, wkv_b, q_norm, k_norm):
    bsz, seq, _ = cols.shape
    c_q, c_kv, k_pe = _split(cols, [Q_LORA_RANK, KV_LORA_RANK, QK_ROPE_DIM])
    q = (rms_norm(c_q, q_a_norm) @ wq_b).reshape(bsz, seq, MLA_HEADS, QK_HEAD_DIM)
    kv = (rms_norm(c_kv, kv_a_norm) @ wkv_b).reshape(bsz, seq, MLA_HEADS, QK_NOPE_DIM + V_HEAD_DIM)
    k_nope, v = kv[..., :QK_NOPE_DIM], kv[..., QK_NOPE_DIM:]
    k_pe = jnp.broadcast_to(k_pe[:, :, None, :], (bsz, seq, MLA_HEADS, QK_ROPE_DIM))
    k = jnp.concatenate([k_nope, k_pe], axis=-1)
    q = rms_norm(q, q_norm)
    k = rms_norm(k, k_norm)
    cos, sin = rope_tables(positions)
    q = jnp.concatenate([q[..., :QK_NOPE_DIM], apply_rope(q[..., QK_NOPE_DIM:], cos, sin)], axis=-1)
    k = jnp.concatenate([k[..., :QK_NOPE_DIM], apply_rope(k[..., QK_NOPE_DIM:], cos, sin)], axis=-1)
    o = causal_block_attention(q, k, v)
    return o.reshape(bsz, seq, MLA_HEADS * V_HEAD_DIM)


def wkv7_scan(r, w, k, v, a, b):
    def step(state, inp):
        r_t, w_t, k_t, v_t, a_t, b_t = inp
        sa = jnp.einsum('bhvk,bhk->bhv', state, a_t)
        state = state * w_t[:, :, None, :] + sa[..., None] * b_t[:, :, None, :] + v_t[..., None] * k_t[:, :, None, :]
        return state, jnp.einsum('bhvk,bhk->bhv', state, r_t)
    bsz, _, heads, n = r.shape
    xs = tuple(jnp.moveaxis(t, 1, 0) for t in (r, w, k, v, a, b))
    s0 = jnp.zeros((bsz, heads, n, n), jnp.float32)
    _, ys = lax.scan(step, s0, xs)
    return jnp.moveaxis(ys, 0, 1)


def rwkv7_branch(cols, h, v_first, mu, w0, w2, a0, a2, g2, k_k, k_a, r_k, ln_w, ln_b, vres):
    bsz, seq, _ = cols.shape
    f32 = jnp.float32
    cols = cols + (token_shift(cols) - cols) * mu
    r, k, v, xw, xa, xg = _split(cols, [RWKV_WIDTH, RWKV_WIDTH, RWKV_WIDTH, DECAY_LORA, AAA_LORA, GATE_LORA])
    log_w = -jax.nn.softplus(-(w0 + jnp.tanh(xw) @ w2).astype(f32)) - 0.5
    decay = jnp.exp(-jnp.exp(log_w))
    a = jax.nn.sigmoid(a0 + xa @ a2)
    g = jax.nn.sigmoid(xg) @ g2
    if vres is None:
        v_first = v
    else:
        v1, v_mu, v0, v2 = vres
        xv = h @ v1
        xv = xv + (token_shift(xv) - xv) * v_mu
        v = v + (v_first - v) * jax.nn.sigmoid(v0 + xv @ v2)
    heads = lambda t: t.reshape(bsz, seq, RWKV_HEADS, RWKV_HEAD_DIM).astype(f32)
    kk = heads(k * k_k)
    kk = kk / jnp.maximum(jnp.linalg.norm(kk, axis=-1, keepdims=True), 1e-12)
    k = k * (1 + (a - 1) * k_a)
    rh, kh, vh, ah = heads(r), heads(k), heads(v), heads(a)
    y = wkv7_scan(rh, heads(decay), kh, vh, -kk, kk * ah)
    mean = jnp.mean(y, axis=-1, keepdims=True)
    var = jnp.mean(jnp.square(y - mean), axis=-1, keepdims=True)
    y = ((y - mean) * lax.rsqrt(var + GN_EPS)).reshape(bsz, seq, RWKV_WIDTH)
    y = y * ln_w.astype(f32) + ln_b.astype(f32)
    bonus = jnp.sum(rh * kh * r_k.astype(f32), axis=-1, keepdims=True) * vh
    y = (y + bonus.reshape(bsz, seq, RWKV_WIDTH)).astype(cols.dtype)
    return y * g, v_first


def short_conv_branch(cols, conv_w):
    b_gate, c_gate, xc = _split(cols, [CONV_WIDTH, CONV_WIDTH, CONV_WIDTH])
    u = c_gate * xc
    y = lax.conv_general_dilated(
        u, conv_w.astype(u.dtype)[:, None, :], window_strides=(1,), padding=[(CONV_K - 1, 0)],
        dimension_numbers=('NWC', 'WIO', 'NWC'), feature_group_count=CONV_WIDTH)
    return b_gate * y


def setup_inputs(seed: int = 0) -> dict:
    key = jax.random.key(seed)
    ks = iter(jax.random.split(key, 64))
    nrm = lambda shape, scale: jax.random.normal(next(ks), shape, jnp.float32) * scale
    gain = lambda shape: 1.0 + nrm(shape, 0.05)
    unif = lambda shape, lo, hi: jax.random.uniform(next(ks), shape, jnp.float32, lo, hi)
    L, Lv = DEPTH, DEPTH - 1
    x = nrm((BATCH, SEQ, D_MODEL), 1.0)
    offsets = jax.random.randint(next(ks), (BATCH, 1), 0, 4096, dtype=jnp.int32)
    positions = offsets + jnp.arange(SEQ, dtype=jnp.int32)[None, :]
    return {
        'x': x,
        'positions': positions,
        'attn_norm': gain((L, D_MODEL)),
        'w_in': nrm((L, D_MODEL, IN_COLS), D_MODEL ** -0.5),
        'mla_q_a_norm': gain((L, Q_LORA_RANK)),
        'mla_wq_b': nrm((L, Q_LORA_RANK, MLA_HEADS * QK_HEAD_DIM), Q_LORA_RANK ** -0.5),
        'mla_kv_a_norm': gain((L, KV_LORA_RANK)),
        'mla_wkv_b': nrm((L, KV_LORA_RANK, MLA_HEADS * (QK_NOPE_DIM + V_HEAD_DIM)), KV_LORA_RANK ** -0.5),
        'mla_q_norm': gain((L, QK_HEAD_DIM)),
        'mla_k_norm': gain((L, QK_HEAD_DIM)),
        'mla_w_o': nrm((L, MLA_HEADS * V_HEAD_DIM, D_MODEL), (MLA_HEADS * V_HEAD_DIM) ** -0.5),
        'rwkv_mu': unif((L, RWKV_COLS), 0.0, 1.0),
        'rwkv_w0': unif((L, RWKV_WIDTH), -6.0, -1.0),
        'rwkv_w2': nrm((L, DECAY_LORA, RWKV_WIDTH), 0.5 * DECAY_LORA ** -0.5),
        'rwkv_a0': nrm((L, RWKV_WIDTH), 0.1),
        'rwkv_a2': nrm((L, AAA_LORA, RWKV_WIDTH), 0.5 * AAA_LORA ** -0.5),
        'rwkv_g2': nrm((L, GATE_LORA, RWKV_WIDTH), GATE_LORA ** -0.5),
        'rwkv_k_k': 0.85 + nrm((L, RWKV_WIDTH), 0.05),
        'rwkv_k_a': gain((L, RWKV_WIDTH)),
        'rwkv_r_k': nrm((L, RWKV_HEADS, RWKV_HEAD_DIM), 0.1),
        'rwkv_ln_w': gain((L, RWKV_WIDTH)),
        'rwkv_ln_b': nrm((L, RWKV_WIDTH), 0.02),
        'rwkv_w_o': nrm((L, RWKV_WIDTH, D_MODEL), RWKV_WIDTH ** -0.5),
        'rwkv_v1': nrm((Lv, D_MODEL, MV_LORA), D_MODEL ** -0.5),
        'rwkv_v_mu': unif((Lv, MV_LORA), 0.0, 1.0),
        'rwkv_v0': nrm((Lv, RWKV_WIDTH), 0.1),
        'rwkv_v2': nrm((Lv, MV_LORA, RWKV_WIDTH), 0.5 * MV_LORA ** -0.5),
        'conv_w': nrm((L, CONV_K, CONV_WIDTH), CONV_K ** -0.5),
        'conv_w_o': nrm((L, CONV_WIDTH, D_MODEL), CONV_WIDTH ** -0.5),
        'w_out': nrm((L, D_MODEL, D_MODEL), D_MODEL ** -0.5),
        'mlp_norm': gain((L, D_MODEL)),
        'w_up': nrm((L, D_MODEL, D_FF), D_MODEL ** -0.5),
        'w_down': nrm((L, D_FF, D_MODEL), D_FF ** -0.5),
    }


def reference(x, positions, attn_norm, w_in, mla_q_a_norm, mla_wq_b, mla_kv_a_norm, mla_wkv_b,
              mla_q_norm, mla_k_norm, mla_w_o, rwkv_mu, rwkv_w0, rwkv_w2, rwkv_a0, rwkv_a2, rwkv_g2,
              rwkv_k_k, rwkv_k_a, rwkv_r_k, rwkv_ln_w, rwkv_ln_b, rwkv_w_o, rwkv_v1, rwkv_v_mu,
              rwkv_v0, rwkv_v2, conv_w, conv_w_o, w_out, mlp_norm, w_up, w_down):
    bsz, seq, _ = x.shape
    v_first = None
    for l in range(DEPTH):
        h = rms_norm(x, attn_norm[l])
        proj = h @ w_in[l]
        gate_cols, mla_cols, rwkv_cols, conv_cols = _split(proj, [GATE_COLS, MLA_COLS, RWKV_COLS, CONV_COLS])
        o_a = mla_branch(mla_cols, positions, mla_q_a_norm[l], mla_wq_b[l], mla_kv_a_norm[l],
                         mla_wkv_b[l], mla_q_norm[l], mla_k_norm[l]) @ mla_w_o[l]
        vres = None if l == 0 else (rwkv_v1[l - 1], rwkv_v_mu[l - 1], rwkv_v0[l - 1], rwkv_v2[l - 1])
        o_b, v_first = rwkv7_branch(rwkv_cols, h, v_first, rwkv_mu[l], rwkv_w0[l], rwkv_w2[l], rwkv_a0[l],
                                    rwkv_a2[l], rwkv_g2[l], rwkv_k_k[l], rwkv_k_a[l], rwkv_r_k[l],
                                    rwkv_ln_w[l], rwkv_ln_b[l], vres)
        o_b = o_b @ rwkv_w_o[l]
        o_c = short_conv_branch(conv_cols, conv_w[l]) @ conv_w_o[l]
        g = jax.nn.sigmoid(gate_cols).reshape(bsz, seq, N_BRANCH, D_MODEL)
        merged = g[:, :, 0] * o_a + g[:, :, 1] * o_b + g[:, :, 2] * o_c
        x = x + merged @ w_out[l]
        h2 = rms_norm(x, mlp_norm[l])
        x = x + jnp.square(jax.nn.relu(h2 @ w_up[l])) @ w_down[l]
    return x
```

```python
import functools

import jax
import jax.numpy as jnp
import numpy as np
from jax import lax
from jax.experimental import pallas as pl
from jax.experimental.pallas import tpu as pltpu

D_MODEL = 1024
DEPTH = 2
MLA_HEADS = 8
QK_NOPE_DIM = 64
QK_ROPE_DIM = 32
QK_HEAD_DIM = QK_NOPE_DIM + QK_ROPE_DIM
V_HEAD_DIM = 64
Q_LORA_RANK = 384
KV_LORA_RANK = 256
ROPE_THETA = 10000.0
RWKV_HEAD_DIM = 64
RWKV_HEADS = 4
RWKV_WIDTH = RWKV_HEADS * RWKV_HEAD_DIM
DECAY_LORA = 64
AAA_LORA = 64
GATE_LORA = 128
MV_LORA = 32
GN_EPS = 64e-5
CONV_WIDTH = 256
CONV_K = 3
D_FF = 4 * D_MODEL
N_BRANCH = 3
NORM_EPS = 1e-6
GATE_COLS = N_BRANCH * D_MODEL
MLA_COLS = Q_LORA_RANK + KV_LORA_RANK + QK_ROPE_DIM
RWKV_COLS = 3 * RWKV_WIDTH + DECAY_LORA + AAA_LORA + GATE_LORA

LANES = 128
HEAD_PAD = LANES
MLA_OUT_COLS = Q_LORA_RANK + KV_LORA_RANK + 2 * LANES
VMEM_LIMIT = 56 * 1024 * 1024

F32 = jnp.float32
BF16 = jnp.bfloat16


def _cparams(sem):
    return pltpu.CompilerParams(dimension_semantics=sem, vmem_limit_bytes=VMEM_LIMIT)


def _const_spec(shape):
    nd = len(shape)
    return pl.BlockSpec(shape, lambda *_: (0,) * nd, pipeline_mode=pl.Buffered(1))


def _bdot(a, b):
    return jnp.dot(a.astype(BF16), b.astype(BF16), preferred_element_type=F32)


def _seg_sum(x, e):
    return jnp.dot(x, e, preferred_element_type=F32, precision=lax.Precision.HIGHEST)


def _in_proj_body(x_ref, g_ref, wg_ref, wm_ref, wr_ref, wc_ref, gate_ref, mla_ref, rwkv_ref, conv_ref):
    x = x_ref[...]
    ms = jnp.mean(x * x, axis=-1, keepdims=True)
    h = (x * lax.rsqrt(ms + NORM_EPS) * g_ref[...]).astype(BF16)
    gate_ref[...] = jax.nn.sigmoid(jnp.dot(h, wg_ref[...], preferred_element_type=F32)).astype(gate_ref.dtype)
    mla_ref[...] = jnp.dot(h, wm_ref[...], preferred_element_type=F32)
    rwkv_ref[...] = jnp.dot(h, wr_ref[...], preferred_element_type=F32)
    conv_ref[...] = jnp.dot(h, wc_ref[...], preferred_element_type=F32)


def _in_proj(x2d, gain, wg, wm, wr, wc, tm=256):
    t, d = x2d.shape
    row = lambda n: pl.BlockSpec((tm, n), lambda i: (i, 0))
    widths = (wg.shape[1], wm.shape[1], wr.shape[1], wc.shape[1])
    return pl.pallas_call(
        _in_proj_body,
        grid=(t // tm,),
        in_specs=[row(d), _const_spec((1, d))] + [_const_spec(w.shape) for w in (wg, wm, wr, wc)],
        out_specs=[row(n) for n in widths],
        out_shape=[jax.ShapeDtypeStruct((t, widths[0]), BF16)]
        + [jax.ShapeDtypeStruct((t, n), F32) for n in widths[1:]],
        compiler_params=_cparams(("parallel",)),
        name="in_proj",
    )(x2d, gain, wg, wm, wr, wc)


def _mla_prep_body(c_ref, cos_ref, sin_ref, qan_ref, kvan_ref, wq_ref, wqr_ref, wk_ref, wv_ref,
                   gq_ref, gqr_ref, gk_ref, gkr_ref, q_ref, k_ref, v_ref):
    c = c_ref[0]
    cq = c[:, :Q_LORA_RANK]
    ckv = c[:, Q_LORA_RANK:Q_LORA_RANK + KV_LORA_RANK]
    kpe = c[:, Q_LORA_RANK + KV_LORA_RANK:Q_LORA_RANK + KV_LORA_RANK + LANES]
    kper = c[:, Q_LORA_RANK + KV_LORA_RANK + LANES:]

    def rms(z, g):
        return (z * lax.rsqrt(jnp.mean(z * z, axis=-1, keepdims=True) + NORM_EPS) * g).astype(BF16)

    cqn = rms(cq, qan_ref[...])
    ckvn = rms(ckv, kvan_ref[...])
    q = jnp.dot(cqn, wq_ref[...], preferred_element_type=F32)
    qr = jnp.dot(cqn, wqr_ref[...], preferred_element_type=F32)
    kn = jnp.dot(ckvn, wk_ref[...], preferred_element_type=F32)
    v = jnp.dot(ckvn, wv_ref[...], preferred_element_type=F32)
    cosf = cos_ref[0]
    sinf = sin_ref[0]
    gqc = gq_ref[...] * cosf
    gqs = gqr_ref[...] * sinf
    gkc = gk_ref[...] * cosf
    kper_s = kper * gkr_ref[...] * sinf
    inv_dim = 1.0 / QK_HEAD_DIM
    for h in range(MLA_HEADS):
        sl = slice(h * HEAD_PAD, (h + 1) * HEAD_PAD)
        qh = q[:, sl]
        rq = lax.rsqrt(jnp.sum(qh * qh, axis=-1, keepdims=True) * inv_dim + NORM_EPS)
        q_ref[0, h] = (rq * (qh * gqc + qr[:, sl] * gqs)).astype(q_ref.dtype)
        kh = kn[:, sl] + kpe
        rk = lax.rsqrt(jnp.sum(kh * kh, axis=-1, keepdims=True) * inv_dim + NORM_EPS)
        k_ref[0, h] = (rk * (kh * gkc + kper_s)).astype(k_ref.dtype)
        v_ref[0, h] = v[:, h * V_HEAD_DIM:(h + 1) * V_HEAD_DIM].astype(v_ref.dtype)


def _mla_prep(mla_cols, cosf, sinf, p, tm=256):
    b, s, n = mla_cols.shape
    tok = lambda w: pl.BlockSpec((1, tm, w), lambda bi, i: (bi, i, 0))
    head = lambda w: pl.BlockSpec((1, MLA_HEADS, tm, w), lambda bi, i: (bi, 0, i, 0))
    consts = (p["qan"], p["kvan"], p["wq"], p["wqr"], p["wk"], p["wv"], p["gq"], p["gqr"], p["gk"], p["gkr"])
    return pl.pallas_call(
        _mla_prep_body,
        grid=(b, s // tm),
        in_specs=[tok(n), tok(LANES), tok(LANES)] + [_const_spec(a.shape) for a in consts],
        out_specs=[head(HEAD_PAD), head(HEAD_PAD), head(V_HEAD_DIM)],
        out_shape=[jax.ShapeDtypeStruct((b, MLA_HEADS, s, HEAD_PAD), BF16),
                   jax.ShapeDtypeStruct((b, MLA_HEADS, s, HEAD_PAD), BF16),
                   jax.ShapeDtypeStruct((b, MLA_HEADS, s, V_HEAD_DIM), BF16)],
        compiler_params=_cparams(("parallel", "parallel")),
        name="mla_prep",
    )(mla_cols, cosf, sinf, *consts)


MASK_VALUE = -1e30


def _attn_body(q_ref, k_ref, v_ref, o_ref, *, tq):
    i = pl.program_id(1)
    row = lax.broadcasted_iota(jnp.int32, (tq, tq), 0)
    col = lax.broadcasted_iota(jnp.int32, (tq, tq), 1)
    causal = row >= col
    for h in range(MLA_HEADS):
        q = q_ref[0, h]

        def block(j, carry, masked):
            m, l, acc = carry
            start = pl.multiple_of(j * tq, tq)
            kj = k_ref[0, h, pl.ds(start, tq), :]
            vj = v_ref[0, h, pl.ds(start, tq), :]
            s = lax.dot_general(q, kj, (((1,), (1,)), ((), ())), preferred_element_type=F32)
            if masked:
                s = jnp.where(causal, s, MASK_VALUE)
            m_new = jnp.maximum(m, jnp.max(s, axis=-1, keepdims=True))
            alpha = jnp.exp(m - m_new)
            pr = jnp.exp(s - m_new)
            l = alpha * l + jnp.sum(pr, axis=-1, keepdims=True)
            acc = alpha * acc + jnp.dot(pr.astype(vj.dtype), vj, preferred_element_type=F32)
            return m_new, l, acc

        init = (jnp.full((tq, 1), MASK_VALUE, F32), jnp.zeros((tq, 1), F32), jnp.zeros((tq, V_HEAD_DIM), F32))
        carry = lax.fori_loop(0, i, lambda j, cr: block(j, cr, False), init)
        _, l, acc = block(i, carry, True)
        o_ref[0, :, h * V_HEAD_DIM:(h + 1) * V_HEAD_DIM] = (acc / l).astype(o_ref.dtype)


def _attention(q, k, v, tq=256):
    b, nh, s, dp = q.shape
    return pl.pallas_call(
        functools.partial(_attn_body, tq=tq),
        grid=(b, s // tq),
        in_specs=[pl.BlockSpec((1, nh, tq, dp), lambda bi, i: (bi, 0, i, 0)),
                  pl.BlockSpec((1, nh, s, dp), lambda bi, i: (bi, 0, 0, 0)),
                  pl.BlockSpec((1, nh, s, V_HEAD_DIM), lambda bi, i: (bi, 0, 0, 0))],
        out_specs=pl.BlockSpec((1, tq, nh * V_HEAD_DIM), lambda bi, i: (bi, i, 0)),
        out_shape=jax.ShapeDtypeStruct((b, s, nh * V_HEAD_DIM), BF16),
        compiler_params=_cparams(("parallel", "arbitrary")),
        name="mla_attention",
    )(q, k, v)


def _rwkv_prep_body(*refs, has_vres, ts):
    if has_vres:
        (x_ref, xp_ref, vf_ref, mu_ref, w0_ref, w2_ref, a0_ref, a2_ref, g2_ref, kk_ref, ka_ref, rk_ref, e_ref,
         v0_ref, v2_ref, r_o, w_o, k_o, v_o, a_o, b_o, g_o, bonus_o) = refs
    else:
        (x_ref, xp_ref, mu_ref, w0_ref, w2_ref, a0_ref, a2_ref, g2_ref, kk_ref, ka_ref, rk_ref, e_ref,
         r_o, w_o, k_o, v_o, a_o, b_o, g_o, bonus_o) = refs
    i = pl.program_id(1)
    x = x_ref[0]
    prev = jnp.where(i > 0, xp_ref[0][7:8, :], 0.0)
    row = lax.broadcasted_iota(jnp.int32, (ts, 1), 0)
    shifted = jnp.where(row == 0, prev, pltpu.roll(x, 1, axis=0))
    xs = x + (shifted - x) * mu_ref[...]
    wd = RWKV_WIDTH
    r = xs[:, 0:wd]
    k = xs[:, wd:2 * wd]
    v = xs[:, 2 * wd:3 * wd]
    lora_in = xs[:, 3 * wd:3 * wd + LANES]
    xg = xs[:, 3 * wd + LANES:3 * wd + 2 * LANES]
    e = e_ref[...]
    zw = w0_ref[...] + _bdot(jnp.tanh(lora_in), w2_ref[...])
    nz = -zw
    softplus = jnp.maximum(nz, 0.0) + jnp.log(1.0 + jnp.exp(-jnp.abs(nz)))
    decay = jnp.exp(-jnp.exp(-softplus - 0.5))
    a_lr = jax.nn.sigmoid(a0_ref[...] + _bdot(lora_in, a2_ref[...]))
    g = _bdot(jax.nn.sigmoid(xg), g2_ref[...])
    if has_vres:
        xvs = xs[:, RWKV_COLS:RWKV_COLS + LANES]
        v = v + (vf_ref[0] - v) * jax.nn.sigmoid(v0_ref[...] + _bdot(xvs, v2_ref[...]))
    kk = k * kk_ref[...]
    norm = jnp.sqrt(_seg_sum(kk * kk, e))
    kk = kk / jnp.maximum(norm, 1e-12)
    k = k * (1.0 + (a_lr - 1.0) * ka_ref[...])
    r_o[0] = r
    w_o[0] = decay
    k_o[0] = k
    v_o[0] = v
    a_o[0] = -kk
    b_o[0] = kk * a_lr
    g_o[0] = g
    bonus_o[0] = _seg_sum(r * k * rk_ref[...], e) * v


def _rwkv_prep(rwkv_cols, v_first, p, ts=256):
    b, s, n = rwkv_cols.shape
    has_vres = v_first is not None
    tok = lambda w: pl.BlockSpec((1, ts, w), lambda bi, i: (bi, i, 0))
    halo = pl.BlockSpec((1, 8, n), lambda bi, i: (bi, jnp.maximum(i * (ts // 8) - 1, 0), 0))
    consts = [p["mu"], p["w0"], p["w2"], p["a0"], p["a2"], p["g2"], p["k_k"], p["k_a"], p["r_k"], p["seg"]]
    args = [rwkv_cols, rwkv_cols]
    in_specs = [tok(n), halo]
    if has_vres:
        args.append(v_first)
        in_specs.append(tok(RWKV_WIDTH))
        consts += [p["v0"], p["v2"]]
    in_specs += [_const_spec(a.shape) for a in consts]
    return pl.pallas_call(
        functools.partial(_rwkv_prep_body, has_vres=has_vres, ts=ts),
        grid=(b, s // ts),
        in_specs=in_specs,
        out_specs=[tok(RWKV_WIDTH)] * 8,
        out_shape=[jax.ShapeDtypeStruct((b, s, RWKV_WIDTH), F32)] * 8,
        compiler_params=_cparams(("parallel", "parallel")),
        name="rwkv_prep",
    )(*args, *consts)


SCAN_VR = 16
SCAN_VQ = RWKV_HEAD_DIM // SCAN_VR
N_ACC = 4


def _scan_body(a_ref, w_ref, b_ref, k_ref, r_ref, v_ref, y_ref, s_ref, *, tc):
    @pl.when(pl.program_id(0) == 0)
    def _():
        s_ref[...] = jnp.zeros_like(s_ref)

    def step(t, carry):
        acc = [None] * N_ACC
        for kk in range(RWKV_HEAD_DIM):
            term = s_ref[kk] * a_ref[t, pl.ds(kk, 1), :]
            acc[kk % N_ACC] = term if acc[kk % N_ACC] is None else acc[kk % N_ACC] + term
        u = (acc[0] + acc[1]) + (acc[2] + acc[3])
        v = v_ref[t]
        yacc = [None] * N_ACC
        for kk in range(RWKV_HEAD_DIM):
            s_new = (s_ref[kk] * w_ref[t, pl.ds(kk, 1), :] + u * b_ref[t, pl.ds(kk, 1), :]
                     + v * k_ref[t, pl.ds(kk, 1), :])
            s_ref[kk] = s_new
            term = s_new * r_ref[t, pl.ds(kk, 1), :]
            yacc[kk % N_ACC] = term if yacc[kk % N_ACC] is None else yacc[kk % N_ACC] + term
        y_ref[t] = (yacc[0] + yacc[1]) + (yacc[2] + yacc[3])
        return carry

    lax.fori_loop(0, tc, step, 0)


def _wkv_scan(a4, w4, b4, k4, r4, v4, tc=32):
    s = a4.shape[0]
    key_spec = pl.BlockSpec((tc, RWKV_HEAD_DIM, LANES), lambda i: (i, 0, 0))
    val_spec = pl.BlockSpec((tc, SCAN_VR, LANES), lambda i: (i, 0, 0))
    return pl.pallas_call(
        functools.partial(_scan_body, tc=tc),
        grid=(s // tc,),
        in_specs=[key_spec] * 5 + [val_spec],
        out_specs=val_spec,
        out_shape=jax.ShapeDtypeStruct((s, SCAN_VR, LANES), F32),
        scratch_shapes=[pltpu.VMEM((RWKV_HEAD_DIM, SCAN_VR, LANES), F32)],
        compiler_params=_cparams(("arbitrary",)),
        name="wkv_scan",
    )(a4, w4, b4, k4, r4, v4)


def _key_layout(x):
    b, s, _ = x.shape
    y = x.reshape(b, s, RWKV_HEADS, RWKV_HEAD_DIM).transpose(1, 3, 0, 2).reshape(s, RWKV_HEAD_DIM, b * RWKV_HEADS)
    return jnp.tile(y, (1, 1, LANES // (b * RWKV_HEADS)))


def _value_layout(x):
    b, s, _ = x.shape
    y = x.reshape(b, s, RWKV_HEADS, SCAN_VQ, SCAN_VR).transpose(1, 4, 3, 0, 2)
    return y.reshape(s, SCAN_VR, SCAN_VQ * b * RWKV_HEADS)


def _value_unlayout(y, b):
    s = y.shape[0]
    x = y.reshape(s, SCAN_VR, SCAN_VQ, b, RWKV_HEADS).transpose(3, 0, 4, 2, 1)
    return x.reshape(b, s, RWKV_WIDTH)


def _merge_body(x_ref, gate_ref, att_ref, y_ref, g_ref, bonus_ref, conv_ref, convp_ref,
                lnw_ref, lnb_ref, e_ref, cw_ref, wa_ref, wb_ref, wc_ref, wo_ref, out_ref, *, ts):
    i = pl.program_id(1)
    e = e_ref[...]
    y = y_ref[0]
    inv_n = 1.0 / RWKV_HEAD_DIM
    mean = _seg_sum(y, e) * inv_n
    d = y - mean
    var = _seg_sum(d * d, e) * inv_n
    yn = d * lax.rsqrt(var + GN_EPS) * lnw_ref[...] + lnb_ref[...] + bonus_ref[0]
    ob = _bdot(yn * g_ref[0], wb_ref[...])

    cw = CONV_WIDTH
    c = conv_ref[0]
    u = c[:, cw:2 * cw] * c[:, 2 * cw:3 * cw]
    cp = convp_ref[0]
    up = jnp.where(i > 0, cp[:, cw:2 * cw] * cp[:, 2 * cw:3 * cw], 0.0)
    p6 = up[6:7, :]
    p7 = up[7:8, :]
    row = lax.broadcasted_iota(jnp.int32, (ts, 1), 0)
    u1 = jnp.where(row == 0, p7, pltpu.roll(u, 1, axis=0))
    u2 = jnp.where(row == 0, p6, jnp.where(row == 1, p7, pltpu.roll(u, 2, axis=0)))
    taps = cw_ref[...]
    yc = taps[0:1, :] * u2 + taps[1:2, :] * u1 + taps[2:3, :] * u
    oc = _bdot(c[:, 0:cw] * yc, wc_ref[...])

    oa = jnp.dot(att_ref[0], wa_ref[...], preferred_element_type=F32)
    gates = gate_ref[0].astype(F32)
    dm = D_MODEL
    merged = gates[:, 0:dm] * oa + gates[:, dm:2 * dm] * ob + gates[:, 2 * dm:3 * dm] * oc
    out_ref[0] = x_ref[0] + _bdot(merged, wo_ref[...])


def _merge(x, gates, att, y, g, bonus, conv_cols, p, ts=256):
    b, s, d = x.shape
    tok = lambda w: pl.BlockSpec((1, ts, w), lambda bi, i: (bi, i, 0))
    nconv = conv_cols.shape[-1]
    halo = pl.BlockSpec((1, 8, nconv), lambda bi, i: (bi, jnp.maximum(i * (ts // 8) - 1, 0), 0))
    consts = (p["ln_w"], p["ln_b"], p["seg"], p["conv_w"], p["mla_w_o"], p["rwkv_w_o"], p["conv_w_o"], p["w_out"])
    return pl.pallas_call(
        functools.partial(_merge_body, ts=ts),
        grid=(b, s // ts),
        in_specs=[tok(d), tok(GATE_COLS), tok(att.shape[-1]), tok(RWKV_WIDTH), tok(RWKV_WIDTH), tok(RWKV_WIDTH),
                  tok(nconv), halo] + [_const_spec(a.shape) for a in consts],
        out_specs=tok(d),
        out_shape=jax.ShapeDtypeStruct((b, s, d), F32),
        compiler_params=_cparams(("parallel", "parallel")),
        name="branch_merge",
    )(x, gates, att, y, g, bonus, conv_cols, conv_cols, *consts)


def _mlp_body(x_ref, g_ref, wu_ref, wd_ref, o_ref):
    x = x_ref[...]
    ms = jnp.mean(x * x, axis=-1, keepdims=True)
    h = (x * lax.rsqrt(ms + NORM_EPS) * g_ref[...]).astype(BF16)
    up = jnp.dot(h, wu_ref[...], preferred_element_type=F32)
    act = jnp.square(jnp.maximum(up, 0.0)).astype(BF16)
    o_ref[...] = x + jnp.dot(act, wd_ref[...], preferred_element_type=F32)


def _mlp(x2d, gain, w_up, w_down, tm=256):
    t, d = x2d.shape
    row = pl.BlockSpec((tm, d), lambda i: (i, 0))
    return pl.pallas_call(
        _mlp_body,
        grid=(t // tm,),
        in_specs=[row, _const_spec((1, d)), _const_spec(w_up.shape), _const_spec(w_down.shape)],
        out_specs=row,
        out_shape=jax.ShapeDtypeStruct((t, d), F32),
        compiler_params=_cparams(("parallel",)),
        name="mlp",
    )(x2d, gain, w_up, w_down)


def _rope_partner_cols(w):
    half = QK_ROPE_DIM // 2
    return jnp.concatenate([-w[..., half:], w[..., :half]], axis=-1)


def _pad_lanes(w, lo, total=HEAD_PAD):
    n = w.shape[-1]
    pad = [(0, 0)] * (w.ndim - 1) + [(lo, total - lo - n)]
    return jnp.pad(w, pad)


def _layer_params(l, attn_norm, w_in, mla_q_a_norm, mla_wq_b, mla_kv_a_norm, mla_wkv_b, mla_q_norm, mla_k_norm,
                  mla_w_o, rwkv_mu, rwkv_w0, rwkv_w2, rwkv_a0, rwkv_a2, rwkv_g2, rwkv_k_k, rwkv_k_a, rwkv_r_k,
                  rwkv_ln_w, rwkv_ln_b, rwkv_w_o, rwkv_v1, rwkv_v_mu, rwkv_v0, rwkv_v2, conv_w, conv_w_o, w_out,
                  mlp_norm, w_up, w_down):
    p = {}
    row = lambda a: a.reshape(1, -1).astype(F32)
    w = w_in[l]
    o_mla = GATE_COLS
    o_rwkv = o_mla + MLA_COLS
    o_conv = o_rwkv + RWKV_COLS
    p["attn_norm"] = row(attn_norm[l])
    p["w_gate"] = w[:, :GATE_COLS].astype(BF16)
    w_kpe = w[:, o_mla + Q_LORA_RANK + KV_LORA_RANK:o_rwkv]
    p["w_mla"] = jnp.concatenate(
        [w[:, o_mla:o_mla + Q_LORA_RANK + KV_LORA_RANK], _pad_lanes(w_kpe, QK_NOPE_DIM),
         _pad_lanes(_rope_partner_cols(w_kpe), QK_NOPE_DIM)], axis=1).astype(BF16)
    w_rwkv = w[:, o_rwkv:o_conv]
    mu = rwkv_mu[l]
    if l > 0:
        w_rwkv = jnp.concatenate([w_rwkv, _pad_lanes(rwkv_v1[l - 1], 0)], axis=1)
        mu = jnp.concatenate([mu, _pad_lanes(rwkv_v_mu[l - 1], 0)])
        p["v0"] = row(rwkv_v0[l - 1])
        p["v2"] = jnp.pad(rwkv_v2[l - 1], ((0, LANES - MV_LORA), (0, 0)))
    p["w_rwkv"] = w_rwkv.astype(BF16)
    p["mu"] = row(mu)
    p["w_conv"] = w[:, o_conv:].astype(BF16)

    scale = QK_HEAD_DIM ** -0.5
    wq = mla_wq_b[l].reshape(Q_LORA_RANK, MLA_HEADS, QK_HEAD_DIM)
    p["wq"] = _pad_lanes(wq, 0).reshape(Q_LORA_RANK, -1).astype(BF16)
    p["wqr"] = _pad_lanes(_rope_partner_cols(wq[..., QK_NOPE_DIM:]), QK_NOPE_DIM).reshape(Q_LORA_RANK, -1).astype(BF16)
    wkv = mla_wkv_b[l].reshape(KV_LORA_RANK, MLA_HEADS, QK_NOPE_DIM + V_HEAD_DIM)
    p["wk"] = _pad_lanes(wkv[..., :QK_NOPE_DIM], 0).reshape(KV_LORA_RANK, -1).astype(BF16)
    p["wv"] = wkv[..., QK_NOPE_DIM:].reshape(KV_LORA_RANK, -1).astype(BF16)
    p["qan"] = row(mla_q_a_norm[l])
    p["kvan"] = row(mla_kv_a_norm[l])
    swap = lambda g: jnp.concatenate([g[QK_ROPE_DIM // 2:], g[:QK_ROPE_DIM // 2]])
    gq, gk = mla_q_norm[l] * scale, mla_k_norm[l]
    p["gq"] = row(_pad_lanes(gq, 0))
    p["gqr"] = row(_pad_lanes(swap(gq[QK_NOPE_DIM:]), QK_NOPE_DIM))
    p["gk"] = row(_pad_lanes(gk, 0))
    p["gkr"] = row(_pad_lanes(swap(gk[QK_NOPE_DIM:]), QK_NOPE_DIM))
    p["mla_w_o"] = mla_w_o[l].astype(BF16)

    p["w0"] = row(rwkv_w0[l])
    p["w2"] = jnp.pad(rwkv_w2[l], ((0, AAA_LORA), (0, 0)))
    p["a0"] = row(rwkv_a0[l])
    p["a2"] = jnp.pad(rwkv_a2[l], ((DECAY_LORA, 0), (0, 0)))
    p["g2"] = rwkv_g2[l]
    p["k_k"] = row(rwkv_k_k[l])
    p["k_a"] = row(rwkv_k_a[l])
    p["r_k"] = row(rwkv_r_k[l])
    p["ln_w"] = row(rwkv_ln_w[l])
    p["ln_b"] = row(rwkv_ln_b[l])
    head_of = np.arange(RWKV_WIDTH) // RWKV_HEAD_DIM
    p["seg"] = jnp.asarray(head_of[:, None] == head_of[None, :], F32)
    p["rwkv_w_o"] = rwkv_w_o[l].astype(BF16)
    p["conv_w"] = conv_w[l].astype(F32)
    p["conv_w_o"] = conv_w_o[l].astype(BF16)
    p["w_out"] = w_out[l].astype(BF16)
    p["mlp_norm"] = row(mlp_norm[l])
    p["w_up"] = w_up[l].astype(BF16)
    p["w_down"] = w_down[l].astype(BF16)
    return p


def _rope_tables(positions):
    half = QK_ROPE_DIM // 2
    freqs = ROPE_THETA ** (-(jnp.arange(half, dtype=F32) * 2.0 / QK_ROPE_DIM))
    ang = positions.astype(F32)[..., None] * freqs
    cos, sin = jnp.cos(ang), jnp.sin(ang)
    ones = jnp.ones(positions.shape + (QK_NOPE_DIM,), F32)
    tail = jnp.ones(positions.shape + (HEAD_PAD - QK_HEAD_DIM,), F32)
    cosf = jnp.concatenate([ones, cos, cos, tail], axis=-1)
    sinf = jnp.concatenate([0 * ones, sin, sin, 0 * tail], axis=-1)
    return cosf, sinf


def kernel(x, positions, attn_norm, w_in, mla_q_a_norm, mla_wq_b, mla_kv_a_norm, mla_wkv_b, mla_q_norm, mla_k_norm, mla_w_o, rwkv_mu, rwkv_w0, rwkv_w2, rwkv_a0, rwkv_a2, rwkv_g2, rwkv_k_k, rwkv_k_a, rwkv_r_k, rwkv_ln_w, rwkv_ln_b, rwkv_w_o, rwkv_v1, rwkv_v_mu, rwkv_v0, rwkv_v2, conv_w, conv_w_o, w_out, mlp_norm, w_up, w_down):
    weights = (attn_norm, w_in, mla_q_a_norm, mla_wq_b, mla_kv_a_norm, mla_wkv_b, mla_q_norm, mla_k_norm, mla_w_o,
               rwkv_mu, rwkv_w0, rwkv_w2, rwkv_a0, rwkv_a2, rwkv_g2, rwkv_k_k, rwkv_k_a, rwkv_r_k, rwkv_ln_w,
               rwkv_ln_b, rwkv_w_o, rwkv_v1, rwkv_v_mu, rwkv_v0, rwkv_v2, conv_w, conv_w_o, w_out, mlp_norm,
               w_up, w_down)
    b, s, d = x.shape
    cosf, sinf = _rope_tables(positions)
    v_first = None
    for l in range(DEPTH):
        p = _layer_params(l, *weights)
        gates, mla_cols, rwkv_cols, conv_cols = _in_proj(
            x.reshape(b * s, d), p["attn_norm"], p["w_gate"], p["w_mla"], p["w_rwkv"], p["w_conv"])
        q, k, v = _mla_prep(mla_cols.reshape(b, s, -1), cosf, sinf, p)
        att = _attention(q, k, v)
        r_, w_, k_, v_, a_, b_, g_, bonus = _rwkv_prep(rwkv_cols.reshape(b, s, -1), v_first, p)
        if l == 0:
            v_first = v_
        y = _wkv_scan(_key_layout(a_), _key_layout(w_), _key_layout(b_), _key_layout(k_), _key_layout(r_),
                      _value_layout(v_))
        y = _value_unlayout(y, b)
        x = _merge(x, gates.reshape(b, s, -1), att, y, g_, bonus, conv_cols.reshape(b, s, -1), p)
        x = _mlp(x.reshape(b * s, d), p["mlp_norm"], p["w_up"], p["w_down"]).reshape(b, s, d)
    return x
```

```python
import functools

import jax
import jax.numpy as jnp
import numpy as np
from jax import lax
from jax.experimental import pallas as pl
from jax.experimental.pallas import tpu as pltpu

D_MODEL = 1024
DEPTH = 2
MLA_HEADS = 8
QK_NOPE_DIM = 64
QK_ROPE_DIM = 32
QK_HEAD_DIM = QK_NOPE_DIM + QK_ROPE_DIM
V_HEAD_DIM = 64
Q_LORA_RANK = 384
KV_LORA_RANK = 256
ROPE_THETA = 10000.0
RWKV_HEAD_DIM = 64
RWKV_HEADS = 4
RWKV_WIDTH = RWKV_HEADS * RWKV_HEAD_DIM
DECAY_LORA = 64
AAA_LORA = 64
GATE_LORA = 128
MV_LORA = 32
GN_EPS = 64e-5
CONV_WIDTH = 256
CONV_K = 3
D_FF = 4 * D_MODEL
N_BRANCH = 3
NORM_EPS = 1e-6
GATE_COLS = N_BRANCH * D_MODEL
MLA_COLS = Q_LORA_RANK + KV_LORA_RANK + QK_ROPE_DIM
RWKV_COLS = 3 * RWKV_WIDTH + DECAY_LORA + AAA_LORA + GATE_LORA

LANES = 128
HEAD_PAD = LANES
MLA_OUT_COLS = Q_LORA_RANK + KV_LORA_RANK + 2 * LANES
VMEM_LIMIT = 56 * 1024 * 1024

F32 = jnp.float32
BF16 = jnp.bfloat16


def _cparams(sem):
    return pltpu.CompilerParams(dimension_semantics=sem, vmem_limit_bytes=VMEM_LIMIT)


def _const_spec(shape):
    nd = len(shape)
    return pl.BlockSpec(shape, lambda *_: (0,) * nd, pipeline_mode=pl.Buffered(1))


def _bdot(a, b):
    return jnp.dot(a.astype(BF16), b.astype(BF16), preferred_element_type=F32)


def _seg_sum(x, e):
    return jnp.dot(x, e, preferred_element_type=F32, precision=lax.Precision.HIGHEST)


def _in_proj_body(x_ref, g_ref, wg_ref, wm_ref, wr_ref, wc_ref, gate_ref, mla_ref, rwkv_ref, conv_ref):
    x = x_ref[...]
    ms = jnp.mean(x * x, axis=-1, keepdims=True)
    h = (x * lax.rsqrt(ms + NORM_EPS) * g_ref[...]).astype(BF16)
    gate_ref[...] = jax.nn.sigmoid(jnp.dot(h, wg_ref[...], preferred_element_type=F32)).astype(gate_ref.dtype)
    mla_ref[...] = jnp.dot(h, wm_ref[...], preferred_element_type=F32)
    rwkv_ref[...] = jnp.dot(h, wr_ref[...], preferred_element_type=F32)
    conv_ref[...] = jnp.dot(h, wc_ref[...], preferred_element_type=F32)


def _in_proj(x2d, gain, wg, wm, wr, wc, tm=256):
    t, d = x2d.shape
    row = lambda n: pl.BlockSpec((tm, n), lambda i: (i, 0))
    widths = (wg.shape[1], wm.shape[1], wr.shape[1], wc.shape[1])
    return pl.pallas_call(
        _in_proj_body,
        grid=(t // tm,),
        in_specs=[row(d), _const_spec((1, d))] + [_const_spec(w.shape) for w in (wg, wm, wr, wc)],
        out_specs=[row(n) for n in widths],
        out_shape=[jax.ShapeDtypeStruct((t, widths[0]), BF16)]
        + [jax.ShapeDtypeStruct((t, n), F32) for n in widths[1:]],
        compiler_params=_cparams(("parallel",)),
        name="in_proj",
    )(x2d, gain, wg, wm, wr, wc)


def _mla_prep_body(c_ref, cos_ref, sin_ref, qan_ref, kvan_ref, wq_ref, wqr_ref, wk_ref, wv_ref,
                   gq_ref, gqr_ref, gk_ref, gkr_ref, q_ref, k_ref, v_ref):
    c = c_ref[0]
    cq = c[:, :Q_LORA_RANK]
    ckv = c[:, Q_LORA_RANK:Q_LORA_RANK + KV_LORA_RANK]
    kpe = c[:, Q_LORA_RANK + KV_LORA_RANK:Q_LORA_RANK + KV_LORA_RANK + LANES]
    kper = c[:, Q_LORA_RANK + KV_LORA_RANK + LANES:]

    def rms(z, g):
        return (z * lax.rsqrt(jnp.mean(z * z, axis=-1, keepdims=True) + NORM_EPS) * g).astype(BF16)

    cqn = rms(cq, qan_ref[...])
    ckvn = rms(ckv, kvan_ref[...])
    q = jnp.dot(cqn, wq_ref[...], preferred_element_type=F32)
    qr = jnp.dot(cqn, wqr_ref[...], preferred_element_type=F32)
    kn = jnp.dot(ckvn, wk_ref[...], preferred_element_type=F32)
    v = jnp.dot(ckvn, wv_ref[...], preferred_element_type=F32)
    cosf = cos_ref[0]
    sinf = sin_ref[0]
    gqc = gq_ref[...] * cosf
    gqs = gqr_ref[...] * sinf
    gkc = gk_ref[...] * cosf
    kper_s = kper * gkr_ref[...] * sinf
    inv_dim = 1.0 / QK_HEAD_DIM
    vt = v.T
    for h in range(MLA_HEADS):
        sl = slice(h * HEAD_PAD, (h + 1) * HEAD_PAD)
        qh = q[:, sl]
        rq = lax.rsqrt(jnp.sum(qh * qh, axis=-1, keepdims=True) * inv_dim + NORM_EPS)
        q_ref[0, h] = (rq * (qh * gqc + qr[:, sl] * gqs)).astype(q_ref.dtype)
        kh = kn[:, sl] + kpe
        rk = lax.rsqrt(jnp.sum(kh * kh, axis=-1, keepdims=True) * inv_dim + NORM_EPS)
        k_ref[0, h] = (rk * (kh * gkc + kper_s)).astype(k_ref.dtype)
        v_ref[0, h, 0] = vt[h * V_HEAD_DIM:(h + 1) * V_HEAD_DIM, :].astype(v_ref.dtype)


def _mla_prep(mla_cols, cosf, sinf, p):
    b, s, n = mla_cols.shape
    tm = ATTN_BLOCK
    tok = lambda w: pl.BlockSpec((1, tm, w), lambda bi, i: (bi, i, 0))
    head = lambda w: pl.BlockSpec((1, MLA_HEADS, tm, w), lambda bi, i: (bi, 0, i, 0))
    consts = (p["qan"], p["kvan"], p["wq"], p["wqr"], p["wk"], p["wv"], p["gq"], p["gqr"], p["gk"], p["gkr"])
    return pl.pallas_call(
        _mla_prep_body,
        grid=(b, s // tm),
        in_specs=[tok(n), tok(LANES), tok(LANES)] + [_const_spec(a.shape) for a in consts],
        out_specs=[head(HEAD_PAD), head(HEAD_PAD),
                   pl.BlockSpec((1, MLA_HEADS, 1, V_HEAD_DIM, tm), lambda bi, i: (bi, 0, i, 0, 0))],
        out_shape=[jax.ShapeDtypeStruct((b, MLA_HEADS, s, HEAD_PAD), BF16),
                   jax.ShapeDtypeStruct((b, MLA_HEADS, s, HEAD_PAD), BF16),
                   jax.ShapeDtypeStruct((b, MLA_HEADS, s // tm, V_HEAD_DIM, tm), BF16)],
        compiler_params=_cparams(("parallel", "parallel")),
        name="mla_prep",
    )(mla_cols, cosf, sinf, *consts)


MASK_VALUE = -1e30


ATTN_BLOCK = 256
ATTN_LOOKAHEAD = 4


def _attn_body(q_ref, k_ref, vt_ref, o_ref, m_ref, l_ref, acc_ref, *, tq):
    i = pl.program_id(1)
    m_ref[...] = jnp.full(m_ref.shape, MASK_VALUE, F32)
    l_ref[...] = jnp.zeros(l_ref.shape, F32)
    acc_ref[...] = jnp.zeros(acc_ref.shape, F32)
    key_idx = lax.broadcasted_iota(jnp.int32, (tq, tq), 0)
    qry_idx = lax.broadcasted_iota(jnp.int32, (tq, tq), 1)
    causal = key_idx <= qry_idx

    def block(j, masked):
        start = pl.multiple_of(j * tq, tq)

        def scores_t(h):
            kj = k_ref[0, h, pl.ds(start, tq), :]
            return lax.dot_general(kj, q_ref[0, h], (((1,), (1,)), ((), ())), preferred_element_type=F32)

        pending = [scores_t(h) for h in range(ATTN_LOOKAHEAD)]
        for h in range(MLA_HEADS):
            st = pending.pop(0)
            if h + ATTN_LOOKAHEAD < MLA_HEADS:
                pending.append(scores_t(h + ATTN_LOOKAHEAD))
            if masked:
                st = jnp.where(causal, st, MASK_VALUE)
            m_prev = m_ref[h]
            m_new = jnp.maximum(m_prev, jnp.max(st, axis=0, keepdims=True))
            alpha = jnp.exp(m_prev - m_new)
            pt = jnp.exp(st - m_new)
            l_ref[h] = alpha * l_ref[h] + jnp.sum(pt, axis=0, keepdims=True)
            acc_ref[h] = alpha * acc_ref[h] + jnp.dot(vt_ref[0, h, j], pt.astype(BF16), preferred_element_type=F32)
            m_ref[h] = m_new

    def loop_body(j, carry):
        block(j, False)
        return carry

    lax.fori_loop(0, i, loop_body, 0)
    block(i, True)
    for h in range(MLA_HEADS):
        out_t = acc_ref[h] / l_ref[h]
        o_ref[0, :, h * V_HEAD_DIM:(h + 1) * V_HEAD_DIM] = out_t.T.astype(o_ref.dtype)


def _attention(q, k, vt):
    b, nh, s, dp = q.shape
    tq = ATTN_BLOCK
    return pl.pallas_call(
        functools.partial(_attn_body, tq=tq),
        grid=(b, s // tq),
        in_specs=[pl.BlockSpec((1, nh, tq, dp), lambda bi, i: (bi, 0, i, 0)),
                  pl.BlockSpec((1, nh, s, dp), lambda bi, i: (bi, 0, 0, 0)),
                  pl.BlockSpec((1, nh, s // tq, V_HEAD_DIM, tq), lambda bi, i: (bi, 0, 0, 0, 0))],
        out_specs=pl.BlockSpec((1, tq, nh * V_HEAD_DIM), lambda bi, i: (bi, i, 0)),
        out_shape=jax.ShapeDtypeStruct((b, s, nh * V_HEAD_DIM), BF16),
        scratch_shapes=[pltpu.VMEM((nh, 1, tq), F32), pltpu.VMEM((nh, 1, tq), F32),
                        pltpu.VMEM((nh, V_HEAD_DIM, tq), F32)],
        compiler_params=_cparams(("parallel", "arbitrary")),
        name="mla_attention",
    )(q, k, vt)


def _rwkv_prep_body(*refs, has_vres, ts):
    if has_vres:
        (x_ref, xp_ref, vf_ref, mu_ref, w0_ref, w2_ref, a0_ref, a2_ref, g2_ref, kk_ref, ka_ref, rk_ref, e_ref,
         v0_ref, v2_ref, r_o, w_o, k_o, v_o, a_o, b_o, g_o, bonus_o) = refs
    else:
        (x_ref, xp_ref, mu_ref, w0_ref, w2_ref, a0_ref, a2_ref, g2_ref, kk_ref, ka_ref, rk_ref, e_ref,
         r_o, w_o, k_o, v_o, a_o, b_o, g_o, bonus_o) = refs
    i = pl.program_id(1)
    x = x_ref[0]
    prev = jnp.where(i > 0, xp_ref[0][7:8, :], 0.0)
    row = lax.broadcasted_iota(jnp.int32, (ts, 1), 0)
    shifted = jnp.where(row == 0, prev, pltpu.roll(x, 1, axis=0))
    xs = x + (shifted - x) * mu_ref[...]
    wd = RWKV_WIDTH
    r = xs[:, 0:wd]
    k = xs[:, wd:2 * wd]
    v = xs[:, 2 * wd:3 * wd]
    lora_in = xs[:, 3 * wd:3 * wd + LANES]
    xg = xs[:, 3 * wd + LANES:3 * wd + 2 * LANES]
    e = e_ref[...]
    zw = w0_ref[...] + _bdot(jnp.tanh(lora_in), w2_ref[...])
    nz = -zw
    softplus = jnp.maximum(nz, 0.0) + jnp.log(1.0 + jnp.exp(-jnp.abs(nz)))
    decay = jnp.exp(-jnp.exp(-softplus - 0.5))
    a_lr = jax.nn.sigmoid(a0_ref[...] + _bdot(lora_in, a2_ref[...]))
    g = _bdot(jax.nn.sigmoid(xg), g2_ref[...])
    if has_vres:
        xvs = xs[:, RWKV_COLS:RWKV_COLS + LANES]
        v = v + (vf_ref[0] - v) * jax.nn.sigmoid(v0_ref[...] + _bdot(xvs, v2_ref[...]))
    kk = k * kk_ref[...]
    norm = jnp.sqrt(_seg_sum(kk * kk, e))
    kk = kk / jnp.maximum(norm, 1e-12)
    k = k * (1.0 + (a_lr - 1.0) * ka_ref[...])
    r_o[0] = r
    w_o[0] = decay
    k_o[0] = k
    v_o[0] = v
    a_o[0] = -kk
    b_o[0] = kk * a_lr
    g_o[0] = g
    bonus_o[0] = _seg_sum(r * k * rk_ref[...], e) * v


def _rwkv_prep(rwkv_cols, v_first, p, ts=256):
    b, s, n = rwkv_cols.shape
    has_vres = v_first is not None
    tok = lambda w: pl.BlockSpec((1, ts, w), lambda bi, i: (bi, i, 0))
    halo = pl.BlockSpec((1, 8, n), lambda bi, i: (bi, jnp.maximum(i * (ts // 8) - 1, 0), 0))
    consts = [p["mu"], p["w0"], p["w2"], p["a0"], p["a2"], p["g2"], p["k_k"], p["k_a"], p["r_k"], p["seg"]]
    args = [rwkv_cols, rwkv_cols]
    in_specs = [tok(n), halo]
    if has_vres:
        args.append(v_first)
        in_specs.append(tok(RWKV_WIDTH))
        consts += [p["v0"], p["v2"]]
    in_specs += [_const_spec(a.shape) for a in consts]
    return pl.pallas_call(
        functools.partial(_rwkv_prep_body, has_vres=has_vres, ts=ts),
        grid=(b, s // ts),
        in_specs=in_specs,
        out_specs=[tok(RWKV_WIDTH)] * 8,
        out_shape=[jax.ShapeDtypeStruct((b, s, RWKV_WIDTH), F32)] * 8,
        compiler_params=_cparams(("parallel", "parallel")),
        name="rwkv_prep",
    )(*args, *consts)


SCAN_VR = 16
SCAN_VQ = RWKV_HEAD_DIM // SCAN_VR
N_ACC = 4


def _scan_body(a_ref, w_ref, b_ref, k_ref, r_ref, v_ref, y_ref, s_ref, *, tc):
    @pl.when(pl.program_id(0) == 0)
    def _():
        s_ref[...] = jnp.zeros_like(s_ref)

    def step(t, carry):
        acc = [None] * N_ACC
        for kk in range(RWKV_HEAD_DIM):
            term = s_ref[kk] * a_ref[t, pl.ds(kk, 1), :]
            acc[kk % N_ACC] = term if acc[kk % N_ACC] is None else acc[kk % N_ACC] + term
        u = (acc[0] + acc[1]) + (acc[2] + acc[3])
        v = v_ref[t]
        yacc = [None] * N_ACC
        for kk in range(RWKV_HEAD_DIM):
            s_new = (s_ref[kk] * w_ref[t, pl.ds(kk, 1), :] + u * b_ref[t, pl.ds(kk, 1), :]
                     + v * k_ref[t, pl.ds(kk, 1), :])
            s_ref[kk] = s_new
            term = s_new * r_ref[t, pl.ds(kk, 1), :]
            yacc[kk % N_ACC] = term if yacc[kk % N_ACC] is None else yacc[kk % N_ACC] + term
        y_ref[t] = (yacc[0] + yacc[1]) + (yacc[2] + yacc[3])
        return carry

    lax.fori_loop(0, tc, step, 0)


def _wkv_scan(a4, w4, b4, k4, r4, v4, tc=32):
    s = a4.shape[0]
    key_spec = pl.BlockSpec((tc, RWKV_HEAD_DIM, LANES), lambda i: (i, 0, 0))
    val_spec = pl.BlockSpec((tc, SCAN_VR, LANES), lambda i: (i, 0, 0))
    return pl.pallas_call(
        functools.partial(_scan_body, tc=tc),
        grid=(s // tc,),
        in_specs=[key_spec] * 5 + [val_spec],
        out_specs=val_spec,
        out_shape=jax.ShapeDtypeStruct((s, SCAN_VR, LANES), F32),
        scratch_shapes=[pltpu.VMEM((RWKV_HEAD_DIM, SCAN_VR, LANES), F32)],
        compiler_params=_cparams(("arbitrary",)),
        name="wkv_scan",
    )(a4, w4, b4, k4, r4, v4)


def _key_layout(x):
    b, s, _ = x.shape
    y = x.reshape(b, s, RWKV_HEADS, RWKV_HEAD_DIM).transpose(1, 3, 0, 2).reshape(s, RWKV_HEAD_DIM, b * RWKV_HEADS)
    return jnp.tile(y, (1, 1, LANES // (b * RWKV_HEADS)))


def _value_layout(x):
    b, s, _ = x.shape
    y = x.reshape(b, s, RWKV_HEADS, SCAN_VQ, SCAN_VR).transpose(1, 4, 3, 0, 2)
    return y.reshape(s, SCAN_VR, SCAN_VQ * b * RWKV_HEADS)


def _value_unlayout(y, b):
    s = y.shape[0]
    x = y.reshape(s, SCAN_VR, SCAN_VQ, b, RWKV_HEADS).transpose(3, 0, 4, 2, 1)
    return x.reshape(b, s, RWKV_WIDTH)


def _merge_body(x_ref, gate_ref, att_ref, y_ref, g_ref, bonus_ref, conv_ref, convp_ref,
                lnw_ref, lnb_ref, e_ref, cw_ref, wa_ref, wb_ref, wc_ref, wo_ref, out_ref, *, ts):
    i = pl.program_id(1)
    e = e_ref[...]
    y = y_ref[0]
    inv_n = 1.0 / RWKV_HEAD_DIM
    mean = _seg_sum(y, e) * inv_n
    d = y - mean
    var = _seg_sum(d * d, e) * inv_n
    yn = d * lax.rsqrt(var + GN_EPS) * lnw_ref[...] + lnb_ref[...] + bonus_ref[0]
    ob = _bdot(yn * g_ref[0], wb_ref[...])

    cw = CONV_WIDTH
    c = conv_ref[0]
    u = c[:, cw:2 * cw] * c[:, 2 * cw:3 * cw]
    cp = convp_ref[0]
    up = jnp.where(i > 0, cp[:, cw:2 * cw] * cp[:, 2 * cw:3 * cw], 0.0)
    p6 = up[6:7, :]
    p7 = up[7:8, :]
    row = lax.broadcasted_iota(jnp.int32, (ts, 1), 0)
    u1 = jnp.where(row == 0, p7, pltpu.roll(u, 1, axis=0))
    u2 = jnp.where(row == 0, p6, jnp.where(row == 1, p7, pltpu.roll(u, 2, axis=0)))
    taps = cw_ref[...]
    yc = taps[0:1, :] * u2 + taps[1:2, :] * u1 + taps[2:3, :] * u
    oc = _bdot(c[:, 0:cw] * yc, wc_ref[...])

    oa = jnp.dot(att_ref[0], wa_ref[...], preferred_element_type=F32)
    gates = gate_ref[0].astype(F32)
    dm = D_MODEL
    merged = gates[:, 0:dm] * oa + gates[:, dm:2 * dm] * ob + gates[:, 2 * dm:3 * dm] * oc
    out_ref[0] = x_ref[0] + _bdot(merged, wo_ref[...])


def _merge(x, gates, att, y, g, bonus, conv_cols, p, ts=256):
    b, s, d = x.shape
    tok = lambda w: pl.BlockSpec((1, ts, w), lambda bi, i: (bi, i, 0))
    nconv = conv_cols.shape[-1]
    halo = pl.BlockSpec((1, 8, nconv), lambda bi, i: (bi, jnp.maximum(i * (ts // 8) - 1, 0), 0))
    consts = (p["ln_w"], p["ln_b"], p["seg"], p["conv_w"], p["mla_w_o"], p["rwkv_w_o"], p["conv_w_o"], p["w_out"])
    return pl.pallas_call(
        functools.partial(_merge_body, ts=ts),
        grid=(b, s // ts),
        in_specs=[tok(d), tok(GATE_COLS), tok(att.shape[-1]), tok(RWKV_WIDTH), tok(RWKV_WIDTH), tok(RWKV_WIDTH),
                  tok(nconv), halo] + [_const_spec(a.shape) for a in consts],
        out_specs=tok(d),
        out_shape=jax.ShapeDtypeStruct((b, s, d), F32),
        compiler_params=_cparams(("parallel", "parallel")),
        name="branch_merge",
    )(x, gates, att, y, g, bonus, conv_cols, conv_cols, *consts)


def _mlp_body(x_ref, g_ref, wu_ref, wd_ref, o_ref):
    x = x_ref[...]
    ms = jnp.mean(x * x, axis=-1, keepdims=True)
    h = (x * lax.rsqrt(ms + NORM_EPS) * g_ref[...]).astype(BF16)
    up = jnp.dot(h, wu_ref[...], preferred_element_type=F32)
    act = jnp.square(jnp.maximum(up, 0.0)).astype(BF16)
    o_ref[...] = x + jnp.dot(act, wd_ref[...], preferred_element_type=F32)


def _mlp(x2d, gain, w_up, w_down, tm=256):
    t, d = x2d.shape
    row = pl.BlockSpec((tm, d), lambda i: (i, 0))
    return pl.pallas_call(
        _mlp_body,
        grid=(t // tm,),
        in_specs=[row, _const_spec((1, d)), _const_spec(w_up.shape), _const_spec(w_down.shape)],
        out_specs=row,
        out_shape=jax.ShapeDtypeStruct((t, d), F32),
        compiler_params=_cparams(("parallel",)),
        name="mlp",
    )(x2d, gain, w_up, w_down)


def _rope_partner_cols(w):
    half = QK_ROPE_DIM // 2
    return jnp.concatenate([-w[..., half:], w[..., :half]], axis=-1)


def _pad_lanes(w, lo, total=HEAD_PAD):
    n = w.shape[-1]
    pad = [(0, 0)] * (w.ndim - 1) + [(lo, total - lo - n)]
    return jnp.pad(w, pad)


def _layer_params(l, attn_norm, w_in, mla_q_a_norm, mla_wq_b, mla_kv_a_norm, mla_wkv_b, mla_q_norm, mla_k_norm,
                  mla_w_o, rwkv_mu, rwkv_w0, rwkv_w2, rwkv_a0, rwkv_a2, rwkv_g2, rwkv_k_k, rwkv_k_a, rwkv_r_k,
                  rwkv_ln_w, rwkv_ln_b, rwkv_w_o, rwkv_v1, rwkv_v_mu, rwkv_v0, rwkv_v2, conv_w, conv_w_o, w_out,
                  mlp_norm, w_up, w_down):
    p = {}
    row = lambda a: a.reshape(1, -1).astype(F32)
    w = w_in[l]
    o_mla = GATE_COLS
    o_rwkv = o_mla + MLA_COLS
    o_conv = o_rwkv + RWKV_COLS
    p["attn_norm"] = row(attn_norm[l])
    p["w_gate"] = w[:, :GATE_COLS].astype(BF16)
    w_kpe = w[:, o_mla + Q_LORA_RANK + KV_LORA_RANK:o_rwkv]
    p["w_mla"] = jnp.concatenate(
        [w[:, o_mla:o_mla + Q_LORA_RANK + KV_LORA_RANK], _pad_lanes(w_kpe, QK_NOPE_DIM),
         _pad_lanes(_rope_partner_cols(w_kpe), QK_NOPE_DIM)], axis=1).astype(BF16)
    w_rwkv = w[:, o_rwkv:o_conv]
    mu = rwkv_mu[l]
    if l > 0:
        w_rwkv = jnp.concatenate([w_rwkv, _pad_lanes(rwkv_v1[l - 1], 0)], axis=1)
        mu = jnp.concatenate([mu, _pad_lanes(rwkv_v_mu[l - 1], 0)])
        p["v0"] = row(rwkv_v0[l - 1])
        p["v2"] = jnp.pad(rwkv_v2[l - 1], ((0, LANES - MV_LORA), (0, 0)))
    p["w_rwkv"] = w_rwkv.astype(BF16)
    p["mu"] = row(mu)
    p["w_conv"] = w[:, o_conv:].astype(BF16)

    scale = QK_HEAD_DIM ** -0.5
    wq = mla_wq_b[l].reshape(Q_LORA_RANK, MLA_HEADS, QK_HEAD_DIM)
    p["wq"] = _pad_lanes(wq, 0).reshape(Q_LORA_RANK, -1).astype(BF16)
    p["wqr"] = _pad_lanes(_rope_partner_cols(wq[..., QK_NOPE_DIM:]), QK_NOPE_DIM).reshape(Q_LORA_RANK, -1).astype(BF16)
    wkv = mla_wkv_b[l].reshape(KV_LORA_RANK, MLA_HEADS, QK_NOPE_DIM + V_HEAD_DIM)
    p["wk"] = _pad_lanes(wkv[..., :QK_NOPE_DIM], 0).reshape(KV_LORA_RANK, -1).astype(BF16)
    p["wv"] = wkv[..., QK_NOPE_DIM:].reshape(KV_LORA_RANK, -1).astype(BF16)
    p["qan"] = row(mla_q_a_norm[l])
    p["kvan"] = row(mla_kv_a_norm[l])
    swap = lambda g: jnp.concatenate([g[QK_ROPE_DIM // 2:], g[:QK_ROPE_DIM // 2]])
    gq, gk = mla_q_norm[l] * scale, mla_k_norm[l]
    p["gq"] = row(_pad_lanes(gq, 0))
    p["gqr"] = row(_pad_lanes(swap(gq[QK_NOPE_DIM:]), QK_NOPE_DIM))
    p["gk"] = row(_pad_lanes(gk, 0))
    p["gkr"] = row(_pad_lanes(swap(gk[QK_NOPE_DIM:]), QK_NOPE_DIM))
    p["mla_w_o"] = mla_w_o[l].astype(BF16)

    p["w0"] = row(rwkv_w0[l])
    p["w2"] = jnp.pad(rwkv_w2[l], ((0, AAA_LORA), (0, 0)))
    p["a0"] = row(rwkv_a0[l])
    p["a2"] = jnp.pad(rwkv_a2[l], ((DECAY_LORA, 0), (0, 0)))
    p["g2"] = rwkv_g2[l]
    p["k_k"] = row(rwkv_k_k[l])
    p["k_a"] = row(rwkv_k_a[l])
    p["r_k"] = row(rwkv_r_k[l])
    p["ln_w"] = row(rwkv_ln_w[l])
    p["ln_b"] = row(rwkv_ln_b[l])
    head_of = np.arange(RWKV_WIDTH) // RWKV_HEAD_DIM
    p["seg"] = jnp.asarray(head_of[:, None] == head_of[None, :], F32)
    p["rwkv_w_o"] = rwkv_w_o[l].astype(BF16)
    p["conv_w"] = conv_w[l].astype(F32)
    p["conv_w_o"] = conv_w_o[l].astype(BF16)
    p["w_out"] = w_out[l].astype(BF16)
    p["mlp_norm"] = row(mlp_norm[l])
    p["w_up"] = w_up[l].astype(BF16)
    p["w_down"] = w_down[l].astype(BF16)
    return p


def _rope_tables(positions):
    half = QK_ROPE_DIM // 2
    freqs = ROPE_THETA ** (-(jnp.arange(half, dtype=F32) * 2.0 / QK_ROPE_DIM))
    ang = positions.astype(F32)[..., None] * freqs
    cos, sin = jnp.cos(ang), jnp.sin(ang)
    ones = jnp.ones(positions.shape + (QK_NOPE_DIM,), F32)
    tail = jnp.ones(positions.shape + (HEAD_PAD - QK_HEAD_DIM,), F32)
    cosf = jnp.concatenate([ones, cos, cos, tail], axis=-1)
    sinf = jnp.concatenate([0 * ones, sin, sin, 0 * tail], axis=-1)
    return cosf, sinf


def kernel(x, positions, attn_norm, w_in, mla_q_a_norm, mla_wq_b, mla_kv_a_norm, mla_wkv_b, mla_q_norm, mla_k_norm, mla_w_o, rwkv_mu, rwkv_w0, rwkv_w2, rwkv_a0, rwkv_a2, rwkv_g2, rwkv_k_k, rwkv_k_a, rwkv_r_k, rwkv_ln_w, rwkv_ln_b, rwkv_w_o, rwkv_v1, rwkv_v_mu, rwkv_v0, rwkv_v2, conv_w, conv_w_o, w_out, mlp_norm, w_up, w_down):
    weights = (attn_norm, w_in, mla_q_a_norm, mla_wq_b, mla_kv_a_norm, mla_wkv_b, mla_q_norm, mla_k_norm, mla_w_o,
               rwkv_mu, rwkv_w0, rwkv_w2, rwkv_a0, rwkv_a2, rwkv_g2, rwkv_k_k, rwkv_k_a, rwkv_r_k, rwkv_ln_w,
               rwkv_ln_b, rwkv_w_o, rwkv_v1, rwkv_v_mu, rwkv_v0, rwkv_v2, conv_w, conv_w_o, w_out, mlp_norm,
               w_up, w_down)
    b, s, d = x.shape
    cosf, sinf = _rope_tables(positions)
    v_first = None
    for l in range(DEPTH):
        p = _layer_params(l, *weights)
        gates, mla_cols, rwkv_cols, conv_cols = _in_proj(
            x.reshape(b * s, d), p["attn_norm"], p["w_gate"], p["w_mla"], p["w_rwkv"], p["w_conv"])
        q, k, v = _mla_prep(mla_cols.reshape(b, s, -1), cosf, sinf, p)
        att = _attention(q, k, v)
        r_, w_, k_, v_, a_, b_, g_, bonus = _rwkv_prep(rwkv_cols.reshape(b, s, -1), v_first, p)
        if l == 0:
            v_first = v_
        y = _wkv_scan(_key_layout(a_), _key_layout(w_), _key_layout(b_), _key_layout(k_), _key_layout(r_),
                      _value_layout(v_))
        y = _value_unlayout(y, b)
        x = _merge(x, gates.reshape(b, s, -1), att, y, g_, bonus, conv_cols.reshape(b, s, -1), p)
        x = _mlp(x.reshape(b * s, d), p["mlp_norm"], p["w_up"], p["w_down"]).reshape(b, s, d)
    return x
```

```python
import functools

import jax
import jax.numpy as jnp
import numpy as np
from jax import lax
from jax.experimental import pallas as pl
from jax.experimental.pallas import tpu as pltpu

D_MODEL = 1024
DEPTH = 2
MLA_HEADS = 8
QK_NOPE_DIM = 64
QK_ROPE_DIM = 32
QK_HEAD_DIM = QK_NOPE_DIM + QK_ROPE_DIM
V_HEAD_DIM = 64
Q_LORA_RANK = 384
KV_LORA_RANK = 256
ROPE_THETA = 10000.0
RWKV_HEAD_DIM = 64
RWKV_HEADS = 4
RWKV_WIDTH = RWKV_HEADS * RWKV_HEAD_DIM
DECAY_LORA = 64
AAA_LORA = 64
GATE_LORA = 128
MV_LORA = 32
GN_EPS = 64e-5
CONV_WIDTH = 256
CONV_K = 3
D_FF = 4 * D_MODEL
N_BRANCH = 3
NORM_EPS = 1e-6
GATE_COLS = N_BRANCH * D_MODEL
MLA_COLS = Q_LORA_RANK + KV_LORA_RANK + QK_ROPE_DIM
RWKV_COLS = 3 * RWKV_WIDTH + DECAY_LORA + AAA_LORA + GATE_LORA

LANES = 128
HEAD_PAD = LANES
MLA_OUT_COLS = Q_LORA_RANK + KV_LORA_RANK + 2 * LANES
VMEM_LIMIT = 56 * 1024 * 1024

F32 = jnp.float32
BF16 = jnp.bfloat16


def _cparams(sem):
    return pltpu.CompilerParams(dimension_semantics=sem, vmem_limit_bytes=VMEM_LIMIT)


def _const_spec(shape):
    nd = len(shape)
    return pl.BlockSpec(shape, lambda *_: (0,) * nd, pipeline_mode=pl.Buffered(1))


def _bdot(a, b):
    return jnp.dot(a.astype(BF16), b.astype(BF16), preferred_element_type=F32)


def _seg_sum(x, e):
    return jnp.dot(x, e, preferred_element_type=F32, precision=lax.Precision.HIGHEST)


def _in_proj_body(x_ref, g_ref, wg_ref, wm_ref, wr_ref, wc_ref, gate_ref, mla_ref, rwkv_ref, conv_ref):
    x = x_ref[...]
    ms = jnp.mean(x * x, axis=-1, keepdims=True)
    h = (x * lax.rsqrt(ms + NORM_EPS) * g_ref[...]).astype(BF16)
    gate_ref[...] = jax.nn.sigmoid(jnp.dot(h, wg_ref[...], preferred_element_type=F32)).astype(gate_ref.dtype)
    mla_ref[...] = jnp.dot(h, wm_ref[...], preferred_element_type=F32)
    rwkv_ref[...] = jnp.dot(h, wr_ref[...], preferred_element_type=F32)
    conv_ref[...] = jnp.dot(h, wc_ref[...], preferred_element_type=F32)


def _in_proj(x2d, gain, wg, wm, wr, wc, tm=256):
    t, d = x2d.shape
    row = lambda n: pl.BlockSpec((tm, n), lambda i: (i, 0))
    widths = (wg.shape[1], wm.shape[1], wr.shape[1], wc.shape[1])
    return pl.pallas_call(
        _in_proj_body,
        grid=(t // tm,),
        in_specs=[row(d), _const_spec((1, d))] + [_const_spec(w.shape) for w in (wg, wm, wr, wc)],
        out_specs=[row(n) for n in widths],
        out_shape=[jax.ShapeDtypeStruct((t, widths[0]), BF16)]
        + [jax.ShapeDtypeStruct((t, n), F32) for n in widths[1:]],
        compiler_params=_cparams(("parallel",)),
        name="in_proj",
    )(x2d, gain, wg, wm, wr, wc)


def _mla_prep_body(c_ref, cos_ref, sin_ref, qan_ref, kvan_ref, wq_ref, wqr_ref, wk_ref, wv_ref,
                   gq_ref, gqr_ref, gk_ref, gkr_ref, q_ref, k_ref, v_ref):
    c = c_ref[0]
    cq = c[:, :Q_LORA_RANK]
    ckv = c[:, Q_LORA_RANK:Q_LORA_RANK + KV_LORA_RANK]
    kpe = c[:, Q_LORA_RANK + KV_LORA_RANK:Q_LORA_RANK + KV_LORA_RANK + LANES]
    kper = c[:, Q_LORA_RANK + KV_LORA_RANK + LANES:]

    def rms(z, g):
        return (z * lax.rsqrt(jnp.mean(z * z, axis=-1, keepdims=True) + NORM_EPS) * g).astype(BF16)

    cqn = rms(cq, qan_ref[...])
    ckvn = rms(ckv, kvan_ref[...])
    q = jnp.dot(cqn, wq_ref[...], preferred_element_type=F32)
    qr = jnp.dot(cqn, wqr_ref[...], preferred_element_type=F32)
    kn = jnp.dot(ckvn, wk_ref[...], preferred_element_type=F32)
    v = jnp.dot(ckvn, wv_ref[...], preferred_element_type=F32)
    cosf = cos_ref[0]
    sinf = sin_ref[0]
    gqc = gq_ref[...] * cosf
    gqs = gqr_ref[...] * sinf
    gkc = gk_ref[...] * cosf
    kper_s = kper * gkr_ref[...] * sinf
    inv_dim = 1.0 / QK_HEAD_DIM
    vt = v.T
    for h in range(MLA_HEADS):
        sl = slice(h * HEAD_PAD, (h + 1) * HEAD_PAD)
        qh = q[:, sl]
        rq = lax.rsqrt(jnp.sum(qh * qh, axis=-1, keepdims=True) * inv_dim + NORM_EPS)
        q_ref[0, h] = (rq * (qh * gqc + qr[:, sl] * gqs)).astype(q_ref.dtype)
        kh = kn[:, sl] + kpe
        rk = lax.rsqrt(jnp.sum(kh * kh, axis=-1, keepdims=True) * inv_dim + NORM_EPS)
        k_ref[0, h] = (rk * (kh * gkc + kper_s)).astype(k_ref.dtype)
        v_ref[0, h, 0] = vt[h * V_HEAD_DIM:(h + 1) * V_HEAD_DIM, :].astype(v_ref.dtype)


def _mla_prep(mla_cols, cosf, sinf, p):
    b, s, n = mla_cols.shape
    tm = ATTN_BLOCK
    tok = lambda w: pl.BlockSpec((1, tm, w), lambda bi, i: (bi, i, 0))
    head = lambda w: pl.BlockSpec((1, MLA_HEADS, tm, w), lambda bi, i: (bi, 0, i, 0))
    consts = (p["qan"], p["kvan"], p["wq"], p["wqr"], p["wk"], p["wv"], p["gq"], p["gqr"], p["gk"], p["gkr"])
    return pl.pallas_call(
        _mla_prep_body,
        grid=(b, s // tm),
        in_specs=[tok(n), tok(LANES), tok(LANES)] + [_const_spec(a.shape) for a in consts],
        out_specs=[head(HEAD_PAD), head(HEAD_PAD),
                   pl.BlockSpec((1, MLA_HEADS, 1, V_HEAD_DIM, tm), lambda bi, i: (bi, 0, i, 0, 0))],
        out_shape=[jax.ShapeDtypeStruct((b, MLA_HEADS, s, HEAD_PAD), BF16),
                   jax.ShapeDtypeStruct((b, MLA_HEADS, s, HEAD_PAD), BF16),
                   jax.ShapeDtypeStruct((b, MLA_HEADS, s // tm, V_HEAD_DIM, tm), BF16)],
        compiler_params=_cparams(("parallel", "parallel")),
        name="mla_prep",
    )(mla_cols, cosf, sinf, *consts)


MASK_VALUE = -1e30


ATTN_BLOCK = 256
ATTN_LOOKAHEAD = 4


def _attn_body(q_ref, k_ref, vt_ref, o_ref, m_ref, l_ref, acc_ref, *, tq):
    i = pl.program_id(1)
    m_ref[...] = jnp.full(m_ref.shape, MASK_VALUE, F32)
    l_ref[...] = jnp.zeros(l_ref.shape, F32)
    acc_ref[...] = jnp.zeros(acc_ref.shape, F32)
    key_idx = lax.broadcasted_iota(jnp.int32, (tq, tq), 0)
    qry_idx = lax.broadcasted_iota(jnp.int32, (tq, tq), 1)
    causal = key_idx <= qry_idx

    def block(j, masked):
        start = pl.multiple_of(j * tq, tq)

        def scores_t(h):
            kj = k_ref[0, h, pl.ds(start, tq), :]
            return lax.dot_general(kj, q_ref[0, h], (((1,), (1,)), ((), ())), preferred_element_type=F32)

        pending = [scores_t(h) for h in range(ATTN_LOOKAHEAD)]
        for h in range(MLA_HEADS):
            st = pending.pop(0)
            if h + ATTN_LOOKAHEAD < MLA_HEADS:
                pending.append(scores_t(h + ATTN_LOOKAHEAD))
            if masked:
                st = jnp.where(causal, st, MASK_VALUE)
            m_prev = m_ref[h]
            m_new = jnp.maximum(m_prev, jnp.max(st, axis=0, keepdims=True))
            alpha = jnp.exp(m_prev - m_new)
            pt = jnp.exp(st - m_new)
            l_ref[h] = alpha * l_ref[h] + jnp.sum(pt, axis=0, keepdims=True)
            acc_ref[h] = alpha * acc_ref[h] + jnp.dot(vt_ref[0, h, j], pt.astype(BF16), preferred_element_type=F32)
            m_ref[h] = m_new

    def loop_body(j, carry):
        block(j, False)
        return carry

    lax.fori_loop(0, i, loop_body, 0)
    block(i, True)
    for h in range(MLA_HEADS):
        out_t = acc_ref[h] / l_ref[h]
        o_ref[0, :, h * V_HEAD_DIM:(h + 1) * V_HEAD_DIM] = out_t.T.astype(o_ref.dtype)


def _attention(q, k, vt):
    b, nh, s, dp = q.shape
    tq = ATTN_BLOCK
    return pl.pallas_call(
        functools.partial(_attn_body, tq=tq),
        grid=(b, s // tq),
        in_specs=[pl.BlockSpec((1, nh, tq, dp), lambda bi, i: (bi, 0, i, 0)),
                  pl.BlockSpec((1, nh, s, dp), lambda bi, i: (bi, 0, 0, 0)),
                  pl.BlockSpec((1, nh, s // tq, V_HEAD_DIM, tq), lambda bi, i: (bi, 0, 0, 0, 0))],
        out_specs=pl.BlockSpec((1, tq, nh * V_HEAD_DIM), lambda bi, i: (bi, i, 0)),
        out_shape=jax.ShapeDtypeStruct((b, s, nh * V_HEAD_DIM), BF16),
        scratch_shapes=[pltpu.VMEM((nh, 1, tq), F32), pltpu.VMEM((nh, 1, tq), F32),
                        pltpu.VMEM((nh, V_HEAD_DIM, tq), F32)],
        compiler_params=_cparams(("parallel", "arbitrary")),
        name="mla_attention",
    )(q, k, vt)


SCAN_VR = 16
SCAN_VQ = RWKV_HEAD_DIM // SCAN_VR
SCAN_TENSORS = 5
PREP_TS = LANES


def _rwkv_prep_body(*refs, has_vres, ts, nb):
    if has_vres:
        (x_ref, xp_ref, vf_ref, mu_ref, w0_ref, w2_ref, a0_ref, a2_ref, g2_ref, kk_ref, ka_ref, rk_ref, e_ref,
         v0_ref, v2_ref, ops_o, val_o, g_o, bonus_o, xt_ref, vt_ref) = refs
    else:
        (x_ref, xp_ref, mu_ref, w0_ref, w2_ref, a0_ref, a2_ref, g2_ref, kk_ref, ka_ref, rk_ref, e_ref,
         ops_o, val_o, g_o, bonus_o, vfirst_o, xt_ref, vt_ref) = refs
    i = pl.program_id(0)
    tensor = pl.program_id(1)
    bh = nb * RWKV_HEADS

    @pl.when(tensor == 0)
    def _():
        def per_batch(b, carry):
            x = x_ref[b]
            prev = jnp.where(i > 0, xp_ref[b][7:8, :], 0.0)
            row = lax.broadcasted_iota(jnp.int32, (ts, 1), 0)
            shifted = jnp.where(row == 0, prev, pltpu.roll(x, 1, axis=0))
            xs = x + (shifted - x) * mu_ref[...]
            wd = RWKV_WIDTH
            r = xs[:, 0:wd]
            k = xs[:, wd:2 * wd]
            v = xs[:, 2 * wd:3 * wd]
            lora_in = xs[:, 3 * wd:3 * wd + LANES]
            xg = xs[:, 3 * wd + LANES:3 * wd + 2 * LANES]
            e = e_ref[...]
            zw = w0_ref[...] + _bdot(jnp.tanh(lora_in), w2_ref[...])
            nz = -zw
            softplus = jnp.maximum(nz, 0.0) + jnp.log(1.0 + jnp.exp(-jnp.abs(nz)))
            decay = jnp.exp(-jnp.exp(-softplus - 0.5))
            a_lr = jax.nn.sigmoid(a0_ref[...] + _bdot(lora_in, a2_ref[...]))
            g_o[b] = _bdot(jax.nn.sigmoid(xg), g2_ref[...])
            if has_vres:
                xvs = xs[:, RWKV_COLS:RWKV_COLS + LANES]
                v = v + (vf_ref[b] - v) * jax.nn.sigmoid(v0_ref[...] + _bdot(xvs, v2_ref[...]))
            else:
                vfirst_o[b] = v
            kk = k * kk_ref[...]
            norm = jnp.sqrt(_seg_sum(kk * kk, e))
            kk = kk / jnp.maximum(norm, 1e-12)
            k = k * (1.0 + (a_lr - 1.0) * ka_ref[...])
            bonus_o[b] = _seg_sum(r * k * rk_ref[...], e) * v
            rows = pl.ds(pl.multiple_of(b * wd, wd), wd)
            for idx, val in enumerate((-kk, decay, kk * a_lr, k, r)):
                xt_ref[idx, rows, :] = val.T
            vt_ref[rows, :] = v.T
            return carry

        lax.fori_loop(0, nb, per_batch, 0)
        for vr in range(SCAN_VR):
            slab = jnp.concatenate(
                [vt_ref[pl.ds(vq * SCAN_VR + vr, bh, stride=RWKV_HEAD_DIM), :] for vq in range(SCAN_VQ)], axis=0)
            val_o[pl.ds(vr, ts, stride=SCAN_VR), :] = slab.T

    def emit(key, carry):
        rows = xt_ref[tensor, pl.ds(key, bh, stride=RWKV_HEAD_DIM), :]
        ops_o[0, key] = jnp.concatenate([rows] * SCAN_VQ, axis=0).T
        return carry

    lax.fori_loop(0, RWKV_HEAD_DIM, emit, 0, unroll=4)


def _rwkv_prep(rwkv_cols, v_first, p):
    b, s, n = rwkv_cols.shape
    ts = PREP_TS
    has_vres = v_first is not None
    tok = lambda w: pl.BlockSpec((b, ts, w), lambda i, t: (0, i, 0))
    halo = pl.BlockSpec((b, 8, n), lambda i, t: (0, jnp.maximum(i * (ts // 8) - 1, 0), 0))
    consts = [p["mu"], p["w0"], p["w2"], p["a0"], p["a2"], p["g2"], p["k_k"], p["k_a"], p["r_k"], p["seg"]]
    args = [rwkv_cols, rwkv_cols]
    in_specs = [tok(n), halo]
    if has_vres:
        args.append(v_first)
        in_specs.append(tok(RWKV_WIDTH))
        consts += [p["v0"], p["v2"]]
    in_specs += [_const_spec(a.shape) for a in consts]
    n_tok_out = 2 if has_vres else 3
    return pl.pallas_call(
        functools.partial(_rwkv_prep_body, has_vres=has_vres, ts=ts, nb=b),
        grid=(s // ts, SCAN_TENSORS),
        in_specs=in_specs,
        out_specs=[pl.BlockSpec((1, RWKV_HEAD_DIM, ts, LANES), lambda i, t: (t, 0, i, 0)),
                   pl.BlockSpec((ts * SCAN_VR, LANES), lambda i, t: (i, 0))] + [tok(RWKV_WIDTH)] * n_tok_out,
        out_shape=[jax.ShapeDtypeStruct((SCAN_TENSORS, RWKV_HEAD_DIM, s, LANES), F32),
                   jax.ShapeDtypeStruct((s * SCAN_VR, LANES), F32)]
        + [jax.ShapeDtypeStruct((b, s, RWKV_WIDTH), F32)] * n_tok_out,
        scratch_shapes=[pltpu.VMEM((SCAN_TENSORS, b * RWKV_WIDTH, ts), F32), pltpu.VMEM((b * RWKV_WIDTH, ts), F32)],
        compiler_params=_cparams(("arbitrary", "arbitrary")),
        name="rwkv_prep",
    )(*args, *consts)


N_ACC = 4
SCAN_TC = 32
SCAN_SUB = PREP_TS // SCAN_TC


def _scan_body(ops_ref, v_ref, y_ref, s_ref, ych_ref, yt_ref, *, tc, nb):
    c = pl.program_id(1)

    @pl.when((pl.program_id(0) == 0) & (c == 0))
    def _():
        s_ref[...] = jnp.zeros_like(s_ref)

    def tree(acc):
        return (acc[0] + acc[1]) + (acc[2] + acc[3])

    def accumulate(acc, kk, term):
        acc[kk % N_ACC] = term if acc[kk % N_ACC] is None else acc[kk % N_ACC] + term

    acc = [None] * N_ACC
    for kk in range(RWKV_HEAD_DIM):
        accumulate(acc, kk, s_ref[kk] * ops_ref[0, kk, pl.ds(0, 1), :])

    def step(t, u):
        op = lambda idx, kk: ops_ref[idx, kk, pl.ds(t, 1), :]
        t_next = jnp.minimum(t + 1, tc - 1)
        v = v_ref[pl.ds(pl.multiple_of(t * SCAN_VR, SCAN_VR), SCAN_VR), :]
        yacc = [None] * N_ACC
        uacc = [None] * N_ACC
        for kk in range(RWKV_HEAD_DIM):
            s_new = s_ref[kk] * op(1, kk) + u * op(2, kk) + v * op(3, kk)
            s_ref[kk] = s_new
            accumulate(yacc, kk, s_new * op(4, kk))
            accumulate(uacc, kk, s_new * ops_ref[0, kk, pl.ds(t_next, 1), :])
        row0 = pl.multiple_of((c * tc + t) * SCAN_VR, SCAN_VR)
        ych_ref[pl.ds(row0, SCAN_VR), :] = tree(yacc)
        return tree(uacc)

    lax.fori_loop(0, tc, step, tree(acc), unroll=4)

    @pl.when(c == SCAN_SUB - 1)
    def _():
        bh = nb * RWKV_HEADS
        for vr in range(SCAN_VR):
            lanes_by_t = ych_ref[pl.ds(vr, PREP_TS, stride=SCAN_VR), :].T
            for vq in range(SCAN_VQ):
                yt_ref[pl.ds(vq * SCAN_VR + vr, bh, stride=RWKV_HEAD_DIM), :] = lanes_by_t[vq * bh:(vq + 1) * bh, :]
        for b in range(nb):
            y_ref[b] = yt_ref[b * RWKV_WIDTH:(b + 1) * RWKV_WIDTH, :].T


def _wkv_scan(ops, val, nb):
    s = ops.shape[2]
    tc = SCAN_TC
    return pl.pallas_call(
        functools.partial(_scan_body, tc=tc, nb=nb),
        grid=(s // PREP_TS, SCAN_SUB),
        in_specs=[pl.BlockSpec((SCAN_TENSORS, RWKV_HEAD_DIM, tc, LANES), lambda i, c: (0, 0, i * SCAN_SUB + c, 0)),
                  pl.BlockSpec((tc * SCAN_VR, LANES), lambda i, c: (i * SCAN_SUB + c, 0))],
        out_specs=pl.BlockSpec((nb, PREP_TS, RWKV_WIDTH), lambda i, c: (0, i, 0)),
        out_shape=jax.ShapeDtypeStruct((nb, s, RWKV_WIDTH), F32),
        scratch_shapes=[pltpu.VMEM((RWKV_HEAD_DIM, SCAN_VR, LANES), F32),
                        pltpu.VMEM((PREP_TS * SCAN_VR, LANES), F32),
                        pltpu.VMEM((nb * RWKV_WIDTH, PREP_TS), F32)],
        compiler_params=_cparams(("arbitrary", "arbitrary")),
        name="wkv_scan",
    )(ops, val)


def _merge_body(x_ref, gate_ref, att_ref, y_ref, g_ref, bonus_ref, conv_ref, convp_ref,
                lnw_ref, lnb_ref, e_ref, cw_ref, wa_ref, wb_ref, wc_ref, wo_ref, out_ref, *, ts):
    i = pl.program_id(1)
    e = e_ref[...]
    y = y_ref[0]
    inv_n = 1.0 / RWKV_HEAD_DIM
    mean = _seg_sum(y, e) * inv_n
    d = y - mean
    var = _seg_sum(d * d, e) * inv_n
    yn = d * lax.rsqrt(var + GN_EPS) * lnw_ref[...] + lnb_ref[...] + bonus_ref[0]
    ob = _bdot(yn * g_ref[0], wb_ref[...])

    cw = CONV_WIDTH
    c = conv_ref[0]
    u = c[:, cw:2 * cw] * c[:, 2 * cw:3 * cw]
    cp = convp_ref[0]
    up = jnp.where(i > 0, cp[:, cw:2 * cw] * cp[:, 2 * cw:3 * cw], 0.0)
    p6 = up[6:7, :]
    p7 = up[7:8, :]
    row = lax.broadcasted_iota(jnp.int32, (ts, 1), 0)
    u1 = jnp.where(row == 0, p7, pltpu.roll(u, 1, axis=0))
    u2 = jnp.where(row == 0, p6, jnp.where(row == 1, p7, pltpu.roll(u, 2, axis=0)))
    taps = cw_ref[...]
    yc = taps[0:1, :] * u2 + taps[1:2, :] * u1 + taps[2:3, :] * u
    oc = _bdot(c[:, 0:cw] * yc, wc_ref[...])

    oa = jnp.dot(att_ref[0], wa_ref[...], preferred_element_type=F32)
    gates = gate_ref[0].astype(F32)
    dm = D_MODEL
    merged = gates[:, 0:dm] * oa + gates[:, dm:2 * dm] * ob + gates[:, 2 * dm:3 * dm] * oc
    out_ref[0] = x_ref[0] + _bdot(merged, wo_ref[...])


def _merge(x, gates, att, y, g, bonus, conv_cols, p, ts=256):
    b, s, d = x.shape
    tok = lambda w: pl.BlockSpec((1, ts, w), lambda bi, i: (bi, i, 0))
    nconv = conv_cols.shape[-1]
    halo = pl.BlockSpec((1, 8, nconv), lambda bi, i: (bi, jnp.maximum(i * (ts // 8) - 1, 0), 0))
    consts = (p["ln_w"], p["ln_b"], p["seg"], p["conv_w"], p["mla_w_o"], p["rwkv_w_o"], p["conv_w_o"], p["w_out"])
    return pl.pallas_call(
        functools.partial(_merge_body, ts=ts),
        grid=(b, s // ts),
        in_specs=[tok(d), tok(GATE_COLS), tok(att.shape[-1]), tok(RWKV_WIDTH), tok(RWKV_WIDTH), tok(RWKV_WIDTH),
                  tok(nconv), halo] + [_const_spec(a.shape) for a in consts],
        out_specs=tok(d),
        out_shape=jax.ShapeDtypeStruct((b, s, d), F32),
        compiler_params=_cparams(("parallel", "parallel")),
        name="branch_merge",
    )(x, gates, att, y, g, bonus, conv_cols, conv_cols, *consts)


def _mlp_body(x_ref, g_ref, wu_ref, wd_ref, o_ref):
    x = x_ref[...]
    ms = jnp.mean(x * x, axis=-1, keepdims=True)
    h = (x * lax.rsqrt(ms + NORM_EPS) * g_ref[...]).astype(BF16)
    up = jnp.dot(h, wu_ref[...], preferred_element_type=F32)
    act = jnp.square(jnp.maximum(up, 0.0)).astype(BF16)
    o_ref[...] = x + jnp.dot(act, wd_ref[...], preferred_element_type=F32)


def _mlp(x2d, gain, w_up, w_down, tm=256):
    t, d = x2d.shape
    row = pl.BlockSpec((tm, d), lambda i: (i, 0))
    return pl.pallas_call(
        _mlp_body,
        grid=(t // tm,),
        in_specs=[row, _const_spec((1, d)), _const_spec(w_up.shape), _const_spec(w_down.shape)],
        out_specs=row,
        out_shape=jax.ShapeDtypeStruct((t, d), F32),
        compiler_params=_cparams(("parallel",)),
        name="mlp",
    )(x2d, gain, w_up, w_down)


def _rope_partner_cols(w):
    half = QK_ROPE_DIM // 2
    return jnp.concatenate([-w[..., half:], w[..., :half]], axis=-1)


def _pad_lanes(w, lo, total=HEAD_PAD):
    n = w.shape[-1]
    pad = [(0, 0)] * (w.ndim - 1) + [(lo, total - lo - n)]
    return jnp.pad(w, pad)


def _layer_params(l, attn_norm, w_in, mla_q_a_norm, mla_wq_b, mla_kv_a_norm, mla_wkv_b, mla_q_norm, mla_k_norm,
                  mla_w_o, rwkv_mu, rwkv_w0, rwkv_w2, rwkv_a0, rwkv_a2, rwkv_g2, rwkv_k_k, rwkv_k_a, rwkv_r_k,
                  rwkv_ln_w, rwkv_ln_b, rwkv_w_o, rwkv_v1, rwkv_v_mu, rwkv_v0, rwkv_v2, conv_w, conv_w_o, w_out,
                  mlp_norm, w_up, w_down):
    p = {}
    row = lambda a: a.reshape(1, -1).astype(F32)
    w = w_in[l]
    o_mla = GATE_COLS
    o_rwkv = o_mla + MLA_COLS
    o_conv = o_rwkv + RWKV_COLS
    p["attn_norm"] = row(attn_norm[l])
    p["w_gate"] = w[:, :GATE_COLS].astype(BF16)
    w_kpe = w[:, o_mla + Q_LORA_RANK + KV_LORA_RANK:o_rwkv]
    p["w_mla"] = jnp.concatenate(
        [w[:, o_mla:o_mla + Q_LORA_RANK + KV_LORA_RANK], _pad_lanes(w_kpe, QK_NOPE_DIM),
         _pad_lanes(_rope_partner_cols(w_kpe), QK_NOPE_DIM)], axis=1).astype(BF16)
    w_rwkv = w[:, o_rwkv:o_conv]
    mu = rwkv_mu[l]
    if l > 0:
        w_rwkv = jnp.concatenate([w_rwkv, _pad_lanes(rwkv_v1[l - 1], 0)], axis=1)
        mu = jnp.concatenate([mu, _pad_lanes(rwkv_v_mu[l - 1], 0)])
        p["v0"] = row(rwkv_v0[l - 1])
        p["v2"] = jnp.pad(rwkv_v2[l - 1], ((0, LANES - MV_LORA), (0, 0)))
    p["w_rwkv"] = w_rwkv.astype(BF16)
    p["mu"] = row(mu)
    p["w_conv"] = w[:, o_conv:].astype(BF16)

    scale = QK_HEAD_DIM ** -0.5
    wq = mla_wq_b[l].reshape(Q_LORA_RANK, MLA_HEADS, QK_HEAD_DIM)
    p["wq"] = _pad_lanes(wq, 0).reshape(Q_LORA_RANK, -1).astype(BF16)
    p["wqr"] = _pad_lanes(_rope_partner_cols(wq[..., QK_NOPE_DIM:]), QK_NOPE_DIM).reshape(Q_LORA_RANK, -1).astype(BF16)
    wkv = mla_wkv_b[l].reshape(KV_LORA_RANK, MLA_HEADS, QK_NOPE_DIM + V_HEAD_DIM)
    p["wk"] = _pad_lanes(wkv[..., :QK_NOPE_DIM], 0).reshape(KV_LORA_RANK, -1).astype(BF16)
    p["wv"] = wkv[..., QK_NOPE_DIM:].reshape(KV_LORA_RANK, -1).astype(BF16)
    p["qan"] = row(mla_q_a_norm[l])
    p["kvan"] = row(mla_kv_a_norm[l])
    swap = lambda g: jnp.concatenate([g[QK_ROPE_DIM // 2:], g[:QK_ROPE_DIM // 2]])
    gq, gk = mla_q_norm[l] * scale, mla_k_norm[l]
    p["gq"] = row(_pad_lanes(gq, 0))
    p["gqr"] = row(_pad_lanes(swap(gq[QK_NOPE_DIM:]), QK_NOPE_DIM))
    p["gk"] = row(_pad_lanes(gk, 0))
    p["gkr"] = row(_pad_lanes(swap(gk[QK_NOPE_DIM:]), QK_NOPE_DIM))
    p["mla_w_o"] = mla_w_o[l].astype(BF16)

    p["w0"] = row(rwkv_w0[l])
    p["w2"] = jnp.pad(rwkv_w2[l], ((0, AAA_LORA), (0, 0)))
    p["a0"] = row(rwkv_a0[l])
    p["a2"] = jnp.pad(rwkv_a2[l], ((DECAY_LORA, 0), (0, 0)))
    p["g2"] = rwkv_g2[l]
    p["k_k"] = row(rwkv_k_k[l])
    p["k_a"] = row(rwkv_k_a[l])
    p["r_k"] = row(rwkv_r_k[l])
    p["ln_w"] = row(rwkv_ln_w[l])
    p["ln_b"] = row(rwkv_ln_b[l])
    head_of = np.arange(RWKV_WIDTH) // RWKV_HEAD_DIM
    p["seg"] = jnp.asarray(head_of[:, None] == head_of[None, :], F32)
    p["rwkv_w_o"] = rwkv_w_o[l].astype(BF16)
    p["conv_w"] = conv_w[l].astype(F32)
    p["conv_w_o"] = conv_w_o[l].astype(BF16)
    p["w_out"] = w_out[l].astype(BF16)
    p["mlp_norm"] = row(mlp_norm[l])
    p["w_up"] = w_up[l].astype(BF16)
    p["w_down"] = w_down[l].astype(BF16)
    return p


def _rope_tables(positions):
    half = QK_ROPE_DIM // 2
    freqs = ROPE_THETA ** (-(jnp.arange(half, dtype=F32) * 2.0 / QK_ROPE_DIM))
    ang = positions.astype(F32)[..., None] * freqs
    cos, sin = jnp.cos(ang), jnp.sin(ang)
    ones = jnp.ones(positions.shape + (QK_NOPE_DIM,), F32)
    tail = jnp.ones(positions.shape + (HEAD_PAD - QK_HEAD_DIM,), F32)
    cosf = jnp.concatenate([ones, cos, cos, tail], axis=-1)
    sinf = jnp.concatenate([0 * ones, sin, sin, 0 * tail], axis=-1)
    return cosf, sinf


def kernel(x, positions, attn_norm, w_in, mla_q_a_norm, mla_wq_b, mla_kv_a_norm, mla_wkv_b, mla_q_norm, mla_k_norm, mla_w_o, rwkv_mu, rwkv_w0, rwkv_w2, rwkv_a0, rwkv_a2, rwkv_g2, rwkv_k_k, rwkv_k_a, rwkv_r_k, rwkv_ln_w, rwkv_ln_b, rwkv_w_o, rwkv_v1, rwkv_v_mu, rwkv_v0, rwkv_v2, conv_w, conv_w_o, w_out, mlp_norm, w_up, w_down):
    weights = (attn_norm, w_in, mla_q_a_norm, mla_wq_b, mla_kv_a_norm, mla_wkv_b, mla_q_norm, mla_k_norm, mla_w_o,
               rwkv_mu, rwkv_w0, rwkv_w2, rwkv_a0, rwkv_a2, rwkv_g2, rwkv_k_k, rwkv_k_a, rwkv_r_k, rwkv_ln_w,
               rwkv_ln_b, rwkv_w_o, rwkv_v1, rwkv_v_mu, rwkv_v0, rwkv_v2, conv_w, conv_w_o, w_out, mlp_norm,
               w_up, w_down)
    b, s, d = x.shape
    cosf, sinf = _rope_tables(positions)
    v_first = None
    for l in range(DEPTH):
        p = _layer_params(l, *weights)
        gates, mla_cols, rwkv_cols, conv_cols = _in_proj(
            x.reshape(b * s, d), p["attn_norm"], p["w_gate"], p["w_mla"], p["w_rwkv"], p["w_conv"])
        q, k, v = _mla_prep(mla_cols.reshape(b, s, -1), cosf, sinf, p)
        att = _attention(q, k, v)
        prep = _rwkv_prep(rwkv_cols.reshape(b, s, -1), v_first, p)
        scan_ops, scan_val, g_, bonus = prep[:4]
        if l == 0:
            v_first = prep[4]
        y = _wkv_scan(scan_ops, scan_val, b)
        x = _merge(x, gates.reshape(b, s, -1), att, y, g_, bonus, conv_cols.reshape(b, s, -1), p)
        x = _mlp(x.reshape(b * s, d), p["mlp_norm"], p["w_up"], p["w_down"]).reshape(b, s, d)
    return x
```

```python
import functools

import jax
import jax.numpy as jnp
import numpy as np
from jax import lax
from jax.experimental import pallas as pl
from jax.experimental.pallas import tpu as pltpu

D_MODEL = 1024
DEPTH = 2
MLA_HEADS = 8
QK_NOPE_DIM = 64
QK_ROPE_DIM = 32
QK_HEAD_DIM = QK_NOPE_DIM + QK_ROPE_DIM
V_HEAD_DIM = 64
Q_LORA_RANK = 384
KV_LORA_RANK = 256
ROPE_THETA = 10000.0
RWKV_HEAD_DIM = 64
RWKV_HEADS = 4
RWKV_WIDTH = RWKV_HEADS * RWKV_HEAD_DIM
DECAY_LORA = 64
AAA_LORA = 64
GATE_LORA = 128
MV_LORA = 32
GN_EPS = 64e-5
CONV_WIDTH = 256
CONV_K = 3
D_FF = 4 * D_MODEL
N_BRANCH = 3
NORM_EPS = 1e-6
GATE_COLS = N_BRANCH * D_MODEL
MLA_COLS = Q_LORA_RANK + KV_LORA_RANK + QK_ROPE_DIM
RWKV_COLS = 3 * RWKV_WIDTH + DECAY_LORA + AAA_LORA + GATE_LORA

LANES = 128
HEAD_PAD = LANES
MLA_OUT_COLS = Q_LORA_RANK + KV_LORA_RANK + 2 * LANES
VMEM_LIMIT = 56 * 1024 * 1024

F32 = jnp.float32
BF16 = jnp.bfloat16


def _cparams(sem):
    return pltpu.CompilerParams(dimension_semantics=sem, vmem_limit_bytes=VMEM_LIMIT)


def _const_spec(shape):
    nd = len(shape)
    return pl.BlockSpec(shape, lambda *_: (0,) * nd, pipeline_mode=pl.Buffered(1))


def _bdot(a, b):
    return jnp.dot(a.astype(BF16), b.astype(BF16), preferred_element_type=F32)


def _seg_sum(x, e):
    return jnp.dot(x, e, preferred_element_type=F32, precision=lax.Precision.HIGHEST)


def _in_proj_body(x_ref, g_ref, wg_ref, wm_ref, wr_ref, wc_ref, gate_ref, mla_ref, rwkv_ref, conv_ref):
    x = x_ref[...]
    ms = jnp.mean(x * x, axis=-1, keepdims=True)
    h = (x * lax.rsqrt(ms + NORM_EPS) * g_ref[...]).astype(BF16)
    gate_ref[...] = jax.nn.sigmoid(jnp.dot(h, wg_ref[...], preferred_element_type=F32)).astype(gate_ref.dtype)
    mla_ref[...] = jnp.dot(h, wm_ref[...], preferred_element_type=F32)
    rwkv_ref[...] = jnp.dot(h, wr_ref[...], preferred_element_type=F32)
    conv_ref[...] = jnp.dot(h, wc_ref[...], preferred_element_type=F32)


def _in_proj(x2d, gain, wg, wm, wr, wc, tm=256):
    t, d = x2d.shape
    row = lambda n: pl.BlockSpec((tm, n), lambda i: (i, 0))
    widths = (wg.shape[1], wm.shape[1], wr.shape[1], wc.shape[1])
    return pl.pallas_call(
        _in_proj_body,
        grid=(t // tm,),
        in_specs=[row(d), _const_spec((1, d))] + [_const_spec(w.shape) for w in (wg, wm, wr, wc)],
        out_specs=[row(n) for n in widths],
        out_shape=[jax.ShapeDtypeStruct((t, widths[0]), BF16)]
        + [jax.ShapeDtypeStruct((t, n), F32) for n in widths[1:]],
        compiler_params=_cparams(("parallel",)),
        name="in_proj",
    )(x2d, gain, wg, wm, wr, wc)


def _mla_prep_body(c_ref, cos_ref, sin_ref, qan_ref, kvan_ref, wq_ref, wqr_ref, wk_ref, wv_ref,
                   gq_ref, gqr_ref, gk_ref, gkr_ref, q_ref, k_ref, v_ref):
    c = c_ref[0]
    cq = c[:, :Q_LORA_RANK]
    ckv = c[:, Q_LORA_RANK:Q_LORA_RANK + KV_LORA_RANK]
    kpe = c[:, Q_LORA_RANK + KV_LORA_RANK:Q_LORA_RANK + KV_LORA_RANK + LANES]
    kper = c[:, Q_LORA_RANK + KV_LORA_RANK + LANES:]

    def rms(z, g):
        return (z * lax.rsqrt(jnp.mean(z * z, axis=-1, keepdims=True) + NORM_EPS) * g).astype(BF16)

    cqn = rms(cq, qan_ref[...])
    ckvn = rms(ckv, kvan_ref[...])
    q = jnp.dot(cqn, wq_ref[...], preferred_element_type=F32)
    qr = jnp.dot(cqn, wqr_ref[...], preferred_element_type=F32)
    kn = jnp.dot(ckvn, wk_ref[...], preferred_element_type=F32)
    v = jnp.dot(ckvn, wv_ref[...], preferred_element_type=F32)
    cosf = cos_ref[0]
    sinf = sin_ref[0]
    gqc = gq_ref[...] * cosf
    gqs = gqr_ref[...] * sinf
    gkc = gk_ref[...] * cosf
    kper_s = kper * gkr_ref[...] * sinf
    inv_dim = 1.0 / QK_HEAD_DIM
    vt = v.T
    for h in range(MLA_HEADS):
        sl = slice(h * HEAD_PAD, (h + 1) * HEAD_PAD)
        qh = q[:, sl]
        rq = lax.rsqrt(jnp.sum(qh * qh, axis=-1, keepdims=True) * inv_dim + NORM_EPS)
        q_ref[0, h] = (rq * (qh * gqc + qr[:, sl] * gqs)).astype(q_ref.dtype)
        kh = kn[:, sl] + kpe
        rk = lax.rsqrt(jnp.sum(kh * kh, axis=-1, keepdims=True) * inv_dim + NORM_EPS)
        k_ref[0, h] = (rk * (kh * gkc + kper_s)).astype(k_ref.dtype)
        v_ref[0, h, 0] = vt[h * V_HEAD_DIM:(h + 1) * V_HEAD_DIM, :].astype(v_ref.dtype)


def _mla_prep(mla_cols, cosf, sinf, p):
    b, s, n = mla_cols.shape
    tm = ATTN_BLOCK
    tok = lambda w: pl.BlockSpec((1, tm, w), lambda bi, i: (bi, i, 0))
    head = lambda w: pl.BlockSpec((1, MLA_HEADS, tm, w), lambda bi, i: (bi, 0, i, 0))
    consts = (p["qan"], p["kvan"], p["wq"], p["wqr"], p["wk"], p["wv"], p["gq"], p["gqr"], p["gk"], p["gkr"])
    return pl.pallas_call(
        _mla_prep_body,
        grid=(b, s // tm),
        in_specs=[tok(n), tok(LANES), tok(LANES)] + [_const_spec(a.shape) for a in consts],
        out_specs=[head(HEAD_PAD), head(HEAD_PAD),
                   pl.BlockSpec((1, MLA_HEADS, 1, V_HEAD_DIM, tm), lambda bi, i: (bi, 0, i, 0, 0))],
        out_shape=[jax.ShapeDtypeStruct((b, MLA_HEADS, s, HEAD_PAD), BF16),
                   jax.ShapeDtypeStruct((b, MLA_HEADS, s, HEAD_PAD), BF16),
                   jax.ShapeDtypeStruct((b, MLA_HEADS, s // tm, V_HEAD_DIM, tm), BF16)],
        compiler_params=_cparams(("parallel", "parallel")),
        name="mla_prep",
    )(mla_cols, cosf, sinf, *consts)


MASK_VALUE = -1e30


ATTN_BLOCK = 256
ATTN_LOOKAHEAD = 4


def _attn_body(q_ref, k_ref, vt_ref, o_ref, m_ref, l_ref, acc_ref, *, tq):
    i = pl.program_id(1)
    m_ref[...] = jnp.full(m_ref.shape, MASK_VALUE, F32)
    l_ref[...] = jnp.zeros(l_ref.shape, F32)
    acc_ref[...] = jnp.zeros(acc_ref.shape, F32)
    key_idx = lax.broadcasted_iota(jnp.int32, (tq, tq), 0)
    qry_idx = lax.broadcasted_iota(jnp.int32, (tq, tq), 1)
    causal = key_idx <= qry_idx

    def block(j, masked):
        start = pl.multiple_of(j * tq, tq)

        def scores_t(h):
            kj = k_ref[0, h, pl.ds(start, tq), :]
            return lax.dot_general(kj, q_ref[0, h], (((1,), (1,)), ((), ())), preferred_element_type=F32)

        pending = [scores_t(h) for h in range(ATTN_LOOKAHEAD)]
        for h in range(MLA_HEADS):
            st = pending.pop(0)
            if h + ATTN_LOOKAHEAD < MLA_HEADS:
                pending.append(scores_t(h + ATTN_LOOKAHEAD))
            if masked:
                st = jnp.where(causal, st, MASK_VALUE)
            m_prev = m_ref[h]
            m_new = jnp.maximum(m_prev, jnp.max(st, axis=0, keepdims=True))
            alpha = jnp.exp(m_prev - m_new)
            pt = jnp.exp(st - m_new)
            l_ref[h] = alpha * l_ref[h] + jnp.sum(pt, axis=0, keepdims=True)
            acc_ref[h] = alpha * acc_ref[h] + jnp.dot(vt_ref[0, h, j], pt.astype(BF16), preferred_element_type=F32)
            m_ref[h] = m_new

    def loop_body(j, carry):
        block(j, False)
        return carry

    lax.fori_loop(0, i, loop_body, 0)
    block(i, True)
    for h in range(MLA_HEADS):
        out_t = acc_ref[h] / l_ref[h]
        o_ref[0, :, h * V_HEAD_DIM:(h + 1) * V_HEAD_DIM] = out_t.T.astype(o_ref.dtype)


def _attention(q, k, vt):
    b, nh, s, dp = q.shape
    tq = ATTN_BLOCK
    return pl.pallas_call(
        functools.partial(_attn_body, tq=tq),
        grid=(b, s // tq),
        in_specs=[pl.BlockSpec((1, nh, tq, dp), lambda bi, i: (bi, 0, i, 0)),
                  pl.BlockSpec((1, nh, s, dp), lambda bi, i: (bi, 0, 0, 0)),
                  pl.BlockSpec((1, nh, s // tq, V_HEAD_DIM, tq), lambda bi, i: (bi, 0, 0, 0, 0))],
        out_specs=pl.BlockSpec((1, tq, nh * V_HEAD_DIM), lambda bi, i: (bi, i, 0)),
        out_shape=jax.ShapeDtypeStruct((b, s, nh * V_HEAD_DIM), BF16),
        scratch_shapes=[pltpu.VMEM((nh, 1, tq), F32), pltpu.VMEM((nh, 1, tq), F32),
                        pltpu.VMEM((nh, V_HEAD_DIM, tq), F32)],
        compiler_params=_cparams(("parallel", "arbitrary")),
        name="mla_attention",
    )(q, k, vt)


SCAN_VR = 16
SCAN_VQ = RWKV_HEAD_DIM // SCAN_VR
SCAN_TENSORS = 5
PREP_TS = LANES


def _rwkv_prep_body(*refs, has_vres, ts, nb):
    if has_vres:
        (x_ref, xp_ref, vf_ref, mu_ref, w0_ref, w2_ref, a0_ref, a2_ref, g2_ref, kk_ref, ka_ref, rk_ref, e_ref,
         v0_ref, v2_ref, ops_o, val_o, g_o, bonus_o, xt_ref, vt_ref) = refs
    else:
        (x_ref, xp_ref, mu_ref, w0_ref, w2_ref, a0_ref, a2_ref, g2_ref, kk_ref, ka_ref, rk_ref, e_ref,
         ops_o, val_o, g_o, bonus_o, vfirst_o, xt_ref, vt_ref) = refs
    i = pl.program_id(0)
    tensor = pl.program_id(1)
    bh = nb * RWKV_HEADS

    @pl.when(tensor == 0)
    def _():
        def per_batch(b, carry):
            x = x_ref[b]
            prev = jnp.where(i > 0, xp_ref[b][7:8, :], 0.0)
            row = lax.broadcasted_iota(jnp.int32, (ts, 1), 0)
            shifted = jnp.where(row == 0, prev, pltpu.roll(x, 1, axis=0))
            xs = x + (shifted - x) * mu_ref[...]
            wd = RWKV_WIDTH
            r = xs[:, 0:wd]
            k = xs[:, wd:2 * wd]
            v = xs[:, 2 * wd:3 * wd]
            lora_in = xs[:, 3 * wd:3 * wd + LANES]
            xg = xs[:, 3 * wd + LANES:3 * wd + 2 * LANES]
            e = e_ref[...]
            zw = w0_ref[...] + _bdot(jnp.tanh(lora_in), w2_ref[...])
            nz = -zw
            softplus = jnp.maximum(nz, 0.0) + jnp.log(1.0 + jnp.exp(-jnp.abs(nz)))
            decay = jnp.exp(-jnp.exp(-softplus - 0.5))
            a_lr = jax.nn.sigmoid(a0_ref[...] + _bdot(lora_in, a2_ref[...]))
            g_o[b] = _bdot(jax.nn.sigmoid(xg), g2_ref[...])
            if has_vres:
                xvs = xs[:, RWKV_COLS:RWKV_COLS + LANES]
                v = v + (vf_ref[b] - v) * jax.nn.sigmoid(v0_ref[...] + _bdot(xvs, v2_ref[...]))
            else:
                vfirst_o[b] = v
            kk = k * kk_ref[...]
            norm = jnp.sqrt(_seg_sum(kk * kk, e))
            kk = kk / jnp.maximum(norm, 1e-12)
            k = k * (1.0 + (a_lr - 1.0) * ka_ref[...])
            bonus_o[b] = _seg_sum(r * k * rk_ref[...], e) * v
            rows = pl.ds(pl.multiple_of(b * wd, wd), wd)
            for idx, val in enumerate((-kk, decay, kk * a_lr, k, r)):
                xt_ref[idx, rows, :] = val.T
            vt_ref[rows, :] = v.T
            return carry

        lax.fori_loop(0, nb, per_batch, 0)
        for vr in range(SCAN_VR):
            slab = jnp.concatenate(
                [vt_ref[pl.ds(vq * SCAN_VR + vr, bh, stride=RWKV_HEAD_DIM), :] for vq in range(SCAN_VQ)], axis=0)
            val_o[pl.ds(vr, ts, stride=SCAN_VR), :] = slab.T

    def emit(key, carry):
        rows = xt_ref[tensor, pl.ds(key, bh, stride=RWKV_HEAD_DIM), :]
        by_t = jnp.concatenate([rows] * SCAN_VQ, axis=0).T
        for chunk in range(SCAN_SUB):
            ops_o[chunk, 0, key] = by_t[chunk * SCAN_TC:(chunk + 1) * SCAN_TC, :]
        return carry

    lax.fori_loop(0, RWKV_HEAD_DIM, emit, 0, unroll=4)


def _rwkv_prep(rwkv_cols, v_first, p):
    b, s, n = rwkv_cols.shape
    ts = PREP_TS
    has_vres = v_first is not None
    tok = lambda w: pl.BlockSpec((b, ts, w), lambda i, t: (0, i, 0))
    halo = pl.BlockSpec((b, 8, n), lambda i, t: (0, jnp.maximum(i * (ts // 8) - 1, 0), 0))
    consts = [p["mu"], p["w0"], p["w2"], p["a0"], p["a2"], p["g2"], p["k_k"], p["k_a"], p["r_k"], p["seg"]]
    args = [rwkv_cols, rwkv_cols]
    in_specs = [tok(n), halo]
    if has_vres:
        args.append(v_first)
        in_specs.append(tok(RWKV_WIDTH))
        consts += [p["v0"], p["v2"]]
    in_specs += [_const_spec(a.shape) for a in consts]
    n_tok_out = 2 if has_vres else 3
    return pl.pallas_call(
        functools.partial(_rwkv_prep_body, has_vres=has_vres, ts=ts, nb=b),
        grid=(s // ts, SCAN_TENSORS),
        in_specs=in_specs,
        out_specs=[pl.BlockSpec((SCAN_SUB, 1, RWKV_HEAD_DIM, SCAN_TC, LANES), lambda i, t: (i, t, 0, 0, 0)),
                   pl.BlockSpec((ts * SCAN_VR, LANES), lambda i, t: (i, 0))] + [tok(RWKV_WIDTH)] * n_tok_out,
        out_shape=[jax.ShapeDtypeStruct((s // SCAN_TC, SCAN_TENSORS, RWKV_HEAD_DIM, SCAN_TC, LANES), F32),
                   jax.ShapeDtypeStruct((s * SCAN_VR, LANES), F32)]
        + [jax.ShapeDtypeStruct((b, s, RWKV_WIDTH), F32)] * n_tok_out,
        scratch_shapes=[pltpu.VMEM((SCAN_TENSORS, b * RWKV_WIDTH, ts), F32), pltpu.VMEM((b * RWKV_WIDTH, ts), F32)],
        compiler_params=_cparams(("arbitrary", "arbitrary")),
        name="rwkv_prep",
    )(*args, *consts)


N_ACC = 4
SCAN_TC = 32
SCAN_SUB = PREP_TS // SCAN_TC


def _scan_body(ops_ref, v_ref, y_ref, s_ref, ych_ref, yt_ref, *, tc, nb):
    c = pl.program_id(1)

    @pl.when((pl.program_id(0) == 0) & (c == 0))
    def _():
        s_ref[...] = jnp.zeros_like(s_ref)

    def tree(acc):
        return (acc[0] + acc[1]) + (acc[2] + acc[3])

    def accumulate(acc, kk, term):
        acc[kk % N_ACC] = term if acc[kk % N_ACC] is None else acc[kk % N_ACC] + term

    acc = [None] * N_ACC
    for kk in range(RWKV_HEAD_DIM):
        accumulate(acc, kk, s_ref[kk] * ops_ref[0, 0, kk, pl.ds(0, 1), :])

    def step(t, u):
        op = lambda idx, kk: ops_ref[0, idx, kk, pl.ds(t, 1), :]
        t_next = jnp.minimum(t + 1, tc - 1)
        v = v_ref[pl.ds(pl.multiple_of(t * SCAN_VR, SCAN_VR), SCAN_VR), :]
        yacc = [None] * N_ACC
        uacc = [None] * N_ACC
        for kk in range(RWKV_HEAD_DIM):
            s_new = s_ref[kk] * op(1, kk) + u * op(2, kk) + v * op(3, kk)
            s_ref[kk] = s_new
            accumulate(yacc, kk, s_new * op(4, kk))
            accumulate(uacc, kk, s_new * ops_ref[0, 0, kk, pl.ds(t_next, 1), :])
        row0 = pl.multiple_of((c * tc + t) * SCAN_VR, SCAN_VR)
        ych_ref[pl.ds(row0, SCAN_VR), :] = tree(yacc)
        return tree(uacc)

    lax.fori_loop(0, tc, step, tree(acc), unroll=4)

    @pl.when(c == SCAN_SUB - 1)
    def _():
        bh = nb * RWKV_HEADS
        for vr in range(SCAN_VR):
            lanes_by_t = ych_ref[pl.ds(vr, PREP_TS, stride=SCAN_VR), :].T
            for vq in range(SCAN_VQ):
                yt_ref[pl.ds(vq * SCAN_VR + vr, bh, stride=RWKV_HEAD_DIM), :] = lanes_by_t[vq * bh:(vq + 1) * bh, :]
        for b in range(nb):
            y_ref[b] = yt_ref[b * RWKV_WIDTH:(b + 1) * RWKV_WIDTH, :].T


def _wkv_scan(ops, val, nb):
    tc = SCAN_TC
    s = ops.shape[0] * tc
    return pl.pallas_call(
        functools.partial(_scan_body, tc=tc, nb=nb),
        grid=(s // PREP_TS, SCAN_SUB),
        in_specs=[pl.BlockSpec((1, SCAN_TENSORS, RWKV_HEAD_DIM, tc, LANES), lambda i, c: (i * SCAN_SUB + c, 0, 0, 0, 0)),
                  pl.BlockSpec((tc * SCAN_VR, LANES), lambda i, c: (i * SCAN_SUB + c, 0))],
        out_specs=pl.BlockSpec((nb, PREP_TS, RWKV_WIDTH), lambda i, c: (0, i, 0)),
        out_shape=jax.ShapeDtypeStruct((nb, s, RWKV_WIDTH), F32),
        scratch_shapes=[pltpu.VMEM((RWKV_HEAD_DIM, SCAN_VR, LANES), F32),
                        pltpu.VMEM((PREP_TS * SCAN_VR, LANES), F32),
                        pltpu.VMEM((nb * RWKV_WIDTH, PREP_TS), F32)],
        compiler_params=_cparams(("arbitrary", "arbitrary")),
        name="wkv_scan",
    )(ops, val)


def _merge_body(x_ref, gate_ref, att_ref, y_ref, g_ref, bonus_ref, conv_ref, convp_ref,
                lnw_ref, lnb_ref, e_ref, cw_ref, wa_ref, wb_ref, wc_ref, wo_ref, out_ref, *, ts):
    i = pl.program_id(1)
    e = e_ref[...]
    y = y_ref[0]
    inv_n = 1.0 / RWKV_HEAD_DIM
    mean = _seg_sum(y, e) * inv_n
    d = y - mean
    var = _seg_sum(d * d, e) * inv_n
    yn = d * lax.rsqrt(var + GN_EPS) * lnw_ref[...] + lnb_ref[...] + bonus_ref[0]
    ob = _bdot(yn * g_ref[0], wb_ref[...])

    cw = CONV_WIDTH
    c = conv_ref[0]
    u = c[:, cw:2 * cw] * c[:, 2 * cw:3 * cw]
    cp = convp_ref[0]
    up = jnp.where(i > 0, cp[:, cw:2 * cw] * cp[:, 2 * cw:3 * cw], 0.0)
    p6 = up[6:7, :]
    p7 = up[7:8, :]
    row = lax.broadcasted_iota(jnp.int32, (ts, 1), 0)
    u1 = jnp.where(row == 0, p7, pltpu.roll(u, 1, axis=0))
    u2 = jnp.where(row == 0, p6, jnp.where(row == 1, p7, pltpu.roll(u, 2, axis=0)))
    taps = cw_ref[...]
    yc = taps[0:1, :] * u2 + taps[1:2, :] * u1 + taps[2:3, :] * u
    oc = _bdot(c[:, 0:cw] * yc, wc_ref[...])

    oa = jnp.dot(att_ref[0], wa_ref[...], preferred_element_type=F32)
    gates = gate_ref[0].astype(F32)
    dm = D_MODEL
    merged = gates[:, 0:dm] * oa + gates[:, dm:2 * dm] * ob + gates[:, 2 * dm:3 * dm] * oc
    out_ref[0] = x_ref[0] + _bdot(merged, wo_ref[...])


def _merge(x, gates, att, y, g, bonus, conv_cols, p, ts=256):
    b, s, d = x.shape
    tok = lambda w: pl.BlockSpec((1, ts, w), lambda bi, i: (bi, i, 0))
    nconv = conv_cols.shape[-1]
    halo = pl.BlockSpec((1, 8, nconv), lambda bi, i: (bi, jnp.maximum(i * (ts // 8) - 1, 0), 0))
    consts = (p["ln_w"], p["ln_b"], p["seg"], p["conv_w"], p["mla_w_o"], p["rwkv_w_o"], p["conv_w_o"], p["w_out"])
    return pl.pallas_call(
        functools.partial(_merge_body, ts=ts),
        grid=(b, s // ts),
        in_specs=[tok(d), tok(GATE_COLS), tok(att.shape[-1]), tok(RWKV_WIDTH), tok(RWKV_WIDTH), tok(RWKV_WIDTH),
                  tok(nconv), halo] + [_const_spec(a.shape) for a in consts],
        out_specs=tok(d),
        out_shape=jax.ShapeDtypeStruct((b, s, d), F32),
        compiler_params=_cparams(("parallel", "parallel")),
        name="branch_merge",
    )(x, gates, att, y, g, bonus, conv_cols, conv_cols, *consts)


def _mlp_body(x_ref, g_ref, wu_ref, wd_ref, o_ref):
    x = x_ref[...]
    ms = jnp.mean(x * x, axis=-1, keepdims=True)
    h = (x * lax.rsqrt(ms + NORM_EPS) * g_ref[...]).astype(BF16)
    up = jnp.dot(h, wu_ref[...], preferred_element_type=F32)
    act = jnp.square(jnp.maximum(up, 0.0)).astype(BF16)
    o_ref[...] = x + jnp.dot(act, wd_ref[...], preferred_element_type=F32)


def _mlp(x2d, gain, w_up, w_down, tm=256):
    t, d = x2d.shape
    row = pl.BlockSpec((tm, d), lambda i: (i, 0))
    return pl.pallas_call(
        _mlp_body,
        grid=(t // tm,),
        in_specs=[row, _const_spec((1, d)), _const_spec(w_up.shape), _const_spec(w_down.shape)],
        out_specs=row,
        out_shape=jax.ShapeDtypeStruct((t, d), F32),
        compiler_params=_cparams(("parallel",)),
        name="mlp",
    )(x2d, gain, w_up, w_down)


def _rope_partner_cols(w):
    half = QK_ROPE_DIM // 2
    return jnp.concatenate([-w[..., half:], w[..., :half]], axis=-1)


def _pad_lanes(w, lo, total=HEAD_PAD):
    n = w.shape[-1]
    pad = [(0, 0)] * (w.ndim - 1) + [(lo, total - lo - n)]
    return jnp.pad(w, pad)


def _layer_params(l, attn_norm, w_in, mla_q_a_norm, mla_wq_b, mla_kv_a_norm, mla_wkv_b, mla_q_norm, mla_k_norm,
                  mla_w_o, rwkv_mu, rwkv_w0, rwkv_w2, rwkv_a0, rwkv_a2, rwkv_g2, rwkv_k_k, rwkv_k_a, rwkv_r_k,
                  rwkv_ln_w, rwkv_ln_b, rwkv_w_o, rwkv_v1, rwkv_v_mu, rwkv_v0, rwkv_v2, conv_w, conv_w_o, w_out,
                  mlp_norm, w_up, w_down):
    p = {}
    row = lambda a: a.reshape(1, -1).astype(F32)
    w = w_in[l]
    o_mla = GATE_COLS
    o_rwkv = o_mla + MLA_COLS
    o_conv = o_rwkv + RWKV_COLS
    p["attn_norm"] = row(attn_norm[l])
    p["w_gate"] = w[:, :GATE_COLS].astype(BF16)
    w_kpe = w[:, o_mla + Q_LORA_RANK + KV_LORA_RANK:o_rwkv]
    p["w_mla"] = jnp.concatenate(
        [w[:, o_mla:o_mla + Q_LORA_RANK + KV_LORA_RANK], _pad_lanes(w_kpe, QK_NOPE_DIM),
         _pad_lanes(_rope_partner_cols(w_kpe), QK_NOPE_DIM)], axis=1).astype(BF16)
    w_rwkv = w[:, o_rwkv:o_conv]
    mu = rwkv_mu[l]
    if l > 0:
        w_rwkv = jnp.concatenate([w_rwkv, _pad_lanes(rwkv_v1[l - 1], 0)], axis=1)
        mu = jnp.concatenate([mu, _pad_lanes(rwkv_v_mu[l - 1], 0)])
        p["v0"] = row(rwkv_v0[l - 1])
        p["v2"] = jnp.pad(rwkv_v2[l - 1], ((0, LANES - MV_LORA), (0, 0)))
    p["w_rwkv"] = w_rwkv.astype(BF16)
    p["mu"] = row(mu)
    p["w_conv"] = w[:, o_conv:].astype(BF16)

    scale = QK_HEAD_DIM ** -0.5
    wq = mla_wq_b[l].reshape(Q_LORA_RANK, MLA_HEADS, QK_HEAD_DIM)
    p["wq"] = _pad_lanes(wq, 0).reshape(Q_LORA_RANK, -1).astype(BF16)
    p["wqr"] = _pad_lanes(_rope_partner_cols(wq[..., QK_NOPE_DIM:]), QK_NOPE_DIM).reshape(Q_LORA_RANK, -1).astype(BF16)
    wkv = mla_wkv_b[l].reshape(KV_LORA_RANK, MLA_HEADS, QK_NOPE_DIM + V_HEAD_DIM)
    p["wk"] = _pad_lanes(wkv[..., :QK_NOPE_DIM], 0).reshape(KV_LORA_RANK, -1).astype(BF16)
    p["wv"] = wkv[..., QK_NOPE_DIM:].reshape(KV_LORA_RANK, -1).astype(BF16)
    p["qan"] = row(mla_q_a_norm[l])
    p["kvan"] = row(mla_kv_a_norm[l])
    swap = lambda g: jnp.concatenate([g[QK_ROPE_DIM // 2:], g[:QK_ROPE_DIM // 2]])
    gq, gk = mla_q_norm[l] * scale, mla_k_norm[l]
    p["gq"] = row(_pad_lanes(gq, 0))
    p["gqr"] = row(_pad_lanes(swap(gq[QK_NOPE_DIM:]), QK_NOPE_DIM))
    p["gk"] = row(_pad_lanes(gk, 0))
    p["gkr"] = row(_pad_lanes(swap(gk[QK_NOPE_DIM:]), QK_NOPE_DIM))
    p["mla_w_o"] = mla_w_o[l].astype(BF16)

    p["w0"] = row(rwkv_w0[l])
    p["w2"] = jnp.pad(rwkv_w2[l], ((0, AAA_LORA), (0, 0)))
    p["a0"] = row(rwkv_a0[l])
    p["a2"] = jnp.pad(rwkv_a2[l], ((DECAY_LORA, 0), (0, 0)))
    p["g2"] = rwkv_g2[l]
    p["k_k"] = row(rwkv_k_k[l])
    p["k_a"] = row(rwkv_k_a[l])
    p["r_k"] = row(rwkv_r_k[l])
    p["ln_w"] = row(rwkv_ln_w[l])
    p["ln_b"] = row(rwkv_ln_b[l])
    head_of = np.arange(RWKV_WIDTH) // RWKV_HEAD_DIM
    p["seg"] = jnp.asarray(head_of[:, None] == head_of[None, :], F32)
    p["rwkv_w_o"] = rwkv_w_o[l].astype(BF16)
    p["conv_w"] = conv_w[l].astype(F32)
    p["conv_w_o"] = conv_w_o[l].astype(BF16)
    p["w_out"] = w_out[l].astype(BF16)
    p["mlp_norm"] = row(mlp_norm[l])
    p["w_up"] = w_up[l].astype(BF16)
    p["w_down"] = w_down[l].astype(BF16)
    return p


def _rope_tables(positions):
    half = QK_ROPE_DIM // 2
    freqs = ROPE_THETA ** (-(jnp.arange(half, dtype=F32) * 2.0 / QK_ROPE_DIM))
    ang = positions.astype(F32)[..., None] * freqs
    cos, sin = jnp.cos(ang), jnp.sin(ang)
    ones = jnp.ones(positions.shape + (QK_NOPE_DIM,), F32)
    tail = jnp.ones(positions.shape + (HEAD_PAD - QK_HEAD_DIM,), F32)
    cosf = jnp.concatenate([ones, cos, cos, tail], axis=-1)
    sinf = jnp.concatenate([0 * ones, sin, sin, 0 * tail], axis=-1)
    return cosf, sinf


def kernel(x, positions, attn_norm, w_in, mla_q_a_norm, mla_wq_b, mla_kv_a_norm, mla_wkv_b, mla_q_norm, mla_k_norm, mla_w_o, rwkv_mu, rwkv_w0, rwkv_w2, rwkv_a0, rwkv_a2, rwkv_g2, rwkv_k_k, rwkv_k_a, rwkv_r_k, rwkv_ln_w, rwkv_ln_b, rwkv_w_o, rwkv_v1, rwkv_v_mu, rwkv_v0, rwkv_v2, conv_w, conv_w_o, w_out, mlp_norm, w_up, w_down):
    weights = (attn_norm, w_in, mla_q_a_norm, mla_wq_b, mla_kv_a_norm, mla_wkv_b, mla_q_norm, mla_k_norm, mla_w_o,
               rwkv_mu, rwkv_w0, rwkv_w2, rwkv_a0, rwkv_a2, rwkv_g2, rwkv_k_k, rwkv_k_a, rwkv_r_k, rwkv_ln_w,
               rwkv_ln_b, rwkv_w_o, rwkv_v1, rwkv_v_mu, rwkv_v0, rwkv_v2, conv_w, conv_w_o, w_out, mlp_norm,
               w_up, w_down)
    b, s, d = x.shape
    cosf, sinf = _rope_tables(positions)
    v_first = None
    for l in range(DEPTH):
        p = _layer_params(l, *weights)
        gates, mla_cols, rwkv_cols, conv_cols = _in_proj(
            x.reshape(b * s, d), p["attn_norm"], p["w_gate"], p["w_mla"], p["w_rwkv"], p["w_conv"])
        q, k, v = _mla_prep(mla_cols.reshape(b, s, -1), cosf, sinf, p)
        att = _attention(q, k, v)
        prep = _rwkv_prep(rwkv_cols.reshape(b, s, -1), v_first, p)
        scan_ops, scan_val, g_, bonus = prep[:4]
        if l == 0:
            v_first = prep[4]
        y = _wkv_scan(scan_ops, scan_val, b)
        x = _merge(x, gates.reshape(b, s, -1), att, y, g_, bonus, conv_cols.reshape(b, s, -1), p)
        x = _mlp(x.reshape(b * s, d), p["mlp_norm"], p["w_up"], p["w_down"]).reshape(b, s, d)
    return x
```

```python
import functools

import jax
import jax.numpy as jnp
import numpy as np
from jax import lax
from jax.experimental import pallas as pl
from jax.experimental.pallas import tpu as pltpu

D_MODEL = 1024
DEPTH = 2
MLA_HEADS = 8
QK_NOPE_DIM = 64
QK_ROPE_DIM = 32
QK_HEAD_DIM = QK_NOPE_DIM + QK_ROPE_DIM
V_HEAD_DIM = 64
Q_LORA_RANK = 384
KV_LORA_RANK = 256
ROPE_THETA = 10000.0
RWKV_HEAD_DIM = 64
RWKV_HEADS = 4
RWKV_WIDTH = RWKV_HEADS * RWKV_HEAD_DIM
DECAY_LORA = 64
AAA_LORA = 64
GATE_LORA = 128
MV_LORA = 32
GN_EPS = 64e-5
CONV_WIDTH = 256
CONV_K = 3
D_FF = 4 * D_MODEL
N_BRANCH = 3
NORM_EPS = 1e-6
GATE_COLS = N_BRANCH * D_MODEL
MLA_COLS = Q_LORA_RANK + KV_LORA_RANK + QK_ROPE_DIM
RWKV_COLS = 3 * RWKV_WIDTH + DECAY_LORA + AAA_LORA + GATE_LORA

LANES = 128
HEAD_PAD = LANES
MLA_OUT_COLS = Q_LORA_RANK + KV_LORA_RANK + 2 * LANES
VMEM_LIMIT = 56 * 1024 * 1024

F32 = jnp.float32
BF16 = jnp.bfloat16


def _cparams(sem):
    return pltpu.CompilerParams(dimension_semantics=sem, vmem_limit_bytes=VMEM_LIMIT)


def _const_spec(shape):
    nd = len(shape)
    return pl.BlockSpec(shape, lambda *_: (0,) * nd, pipeline_mode=pl.Buffered(1))


def _bdot(a, b):
    return jnp.dot(a.astype(BF16), b.astype(BF16), preferred_element_type=F32)


def _seg_sum(x, e):
    return jnp.dot(x, e, preferred_element_type=F32, precision=lax.Precision.HIGHEST)


def _in_proj_body(x_ref, g_ref, wg_ref, wm_ref, wr_ref, wc_ref, gate_ref, mla_ref, rwkv_ref, conv_ref):
    x = x_ref[...]
    ms = jnp.mean(x * x, axis=-1, keepdims=True)
    h = (x * lax.rsqrt(ms + NORM_EPS) * g_ref[...]).astype(BF16)
    gate_ref[...] = jax.nn.sigmoid(jnp.dot(h, wg_ref[...], preferred_element_type=F32)).astype(gate_ref.dtype)
    mla_ref[...] = jnp.dot(h, wm_ref[...], preferred_element_type=F32)
    rwkv_ref[...] = jnp.dot(h, wr_ref[...], preferred_element_type=F32)
    conv_ref[...] = jnp.dot(h, wc_ref[...], preferred_element_type=F32)


def _in_proj(x2d, gain, wg, wm, wr, wc, tm=256):
    t, d = x2d.shape
    row = lambda n: pl.BlockSpec((tm, n), lambda i: (i, 0))
    widths = (wg.shape[1], wm.shape[1], wr.shape[1], wc.shape[1])
    return pl.pallas_call(
        _in_proj_body,
        grid=(t // tm,),
        in_specs=[row(d), _const_spec((1, d))] + [_const_spec(w.shape) for w in (wg, wm, wr, wc)],
        out_specs=[row(n) for n in widths],
        out_shape=[jax.ShapeDtypeStruct((t, widths[0]), BF16)]
        + [jax.ShapeDtypeStruct((t, n), F32) for n in widths[1:]],
        compiler_params=_cparams(("parallel",)),
        name="in_proj",
    )(x2d, gain, wg, wm, wr, wc)


def _mla_prep_body(c_ref, cos_ref, sin_ref, qan_ref, kvan_ref, wq_ref, wqr_ref, wk_ref, wv_ref,
                   gq_ref, gqr_ref, gk_ref, gkr_ref, q_ref, k_ref, v_ref):
    c = c_ref[0]
    cq = c[:, :Q_LORA_RANK]
    ckv = c[:, Q_LORA_RANK:Q_LORA_RANK + KV_LORA_RANK]
    kpe = c[:, Q_LORA_RANK + KV_LORA_RANK:Q_LORA_RANK + KV_LORA_RANK + LANES]
    kper = c[:, Q_LORA_RANK + KV_LORA_RANK + LANES:]

    def rms(z, g):
        return (z * lax.rsqrt(jnp.mean(z * z, axis=-1, keepdims=True) + NORM_EPS) * g).astype(BF16)

    cqn = rms(cq, qan_ref[...])
    ckvn = rms(ckv, kvan_ref[...])
    q = jnp.dot(cqn, wq_ref[...], preferred_element_type=F32)
    qr = jnp.dot(cqn, wqr_ref[...], preferred_element_type=F32)
    kn = jnp.dot(ckvn, wk_ref[...], preferred_element_type=F32)
    v = jnp.dot(ckvn, wv_ref[...], preferred_element_type=F32)
    cosf = cos_ref[0]
    sinf = sin_ref[0]
    gqc = gq_ref[...] * cosf
    gqs = gqr_ref[...] * sinf
    gkc = gk_ref[...] * cosf
    kper_s = kper * gkr_ref[...] * sinf
    inv_dim = 1.0 / QK_HEAD_DIM
    vt = v.T
    for h in range(MLA_HEADS):
        sl = slice(h * HEAD_PAD, (h + 1) * HEAD_PAD)
        qh = q[:, sl]
        rq = lax.rsqrt(jnp.sum(qh * qh, axis=-1, keepdims=True) * inv_dim + NORM_EPS)
        q_ref[0, h] = (rq * (qh * gqc + qr[:, sl] * gqs)).astype(q_ref.dtype)
        kh = kn[:, sl] + kpe
        rk = lax.rsqrt(jnp.sum(kh * kh, axis=-1, keepdims=True) * inv_dim + NORM_EPS)
        k_ref[0, h] = (rk * (kh * gkc + kper_s)).astype(k_ref.dtype)
        v_ref[0, h, 0] = vt[h * V_HEAD_DIM:(h + 1) * V_HEAD_DIM, :].astype(v_ref.dtype)


def _mla_prep(mla_cols, cosf, sinf, p):
    b, s, n = mla_cols.shape
    tm = ATTN_BLOCK
    tok = lambda w: pl.BlockSpec((1, tm, w), lambda bi, i: (bi, i, 0))
    head = lambda w: pl.BlockSpec((1, MLA_HEADS, tm, w), lambda bi, i: (bi, 0, i, 0))
    consts = (p["qan"], p["kvan"], p["wq"], p["wqr"], p["wk"], p["wv"], p["gq"], p["gqr"], p["gk"], p["gkr"])
    return pl.pallas_call(
        _mla_prep_body,
        grid=(b, s // tm),
        in_specs=[tok(n), tok(LANES), tok(LANES)] + [_const_spec(a.shape) for a in consts],
        out_specs=[head(HEAD_PAD), head(HEAD_PAD),
                   pl.BlockSpec((1, MLA_HEADS, 1, V_HEAD_DIM, tm), lambda bi, i: (bi, 0, i, 0, 0))],
        out_shape=[jax.ShapeDtypeStruct((b, MLA_HEADS, s, HEAD_PAD), BF16),
                   jax.ShapeDtypeStruct((b, MLA_HEADS, s, HEAD_PAD), BF16),
                   jax.ShapeDtypeStruct((b, MLA_HEADS, s // tm, V_HEAD_DIM, tm), BF16)],
        compiler_params=_cparams(("parallel", "parallel")),
        name="mla_prep",
    )(mla_cols, cosf, sinf, *consts)


MASK_VALUE = -1e30


ATTN_BLOCK = 256
ATTN_LOOKAHEAD = 4


def _attn_body(q_ref, k_ref, vt_ref, o_ref, m_ref, l_ref, acc_ref, *, tq):
    i = pl.program_id(1)
    m_ref[...] = jnp.full(m_ref.shape, MASK_VALUE, F32)
    l_ref[...] = jnp.zeros(l_ref.shape, F32)
    acc_ref[...] = jnp.zeros(acc_ref.shape, F32)
    key_idx = lax.broadcasted_iota(jnp.int32, (tq, tq), 0)
    qry_idx = lax.broadcasted_iota(jnp.int32, (tq, tq), 1)
    causal = key_idx <= qry_idx

    def block(j, masked):
        start = pl.multiple_of(j * tq, tq)

        def scores_t(h):
            kj = k_ref[0, h, pl.ds(start, tq), :]
            return lax.dot_general(kj, q_ref[0, h], (((1,), (1,)), ((), ())), preferred_element_type=F32)

        pending = [scores_t(h) for h in range(ATTN_LOOKAHEAD)]
        for h in range(MLA_HEADS):
            st = pending.pop(0)
            if h + ATTN_LOOKAHEAD < MLA_HEADS:
                pending.append(scores_t(h + ATTN_LOOKAHEAD))
            if masked:
                st = jnp.where(causal, st, MASK_VALUE)
            m_prev = m_ref[h]
            m_new = jnp.maximum(m_prev, jnp.max(st, axis=0, keepdims=True))
            alpha = jnp.exp2(m_prev - m_new)
            pt = jnp.exp2(st - m_new)
            l_ref[h] = alpha * l_ref[h] + jnp.sum(pt, axis=0, keepdims=True)
            acc_ref[h] = alpha * acc_ref[h] + jnp.dot(vt_ref[0, h, j], pt.astype(BF16), preferred_element_type=F32)
            m_ref[h] = m_new

    def loop_body(j, carry):
        block(j, False)
        return carry

    lax.fori_loop(0, i, loop_body, 0)
    block(i, True)
    for h in range(MLA_HEADS):
        out_t = acc_ref[h] / l_ref[h]
        o_ref[0, :, h * V_HEAD_DIM:(h + 1) * V_HEAD_DIM] = out_t.T.astype(o_ref.dtype)


def _attention(q, k, vt):
    b, nh, s, dp = q.shape
    tq = ATTN_BLOCK
    return pl.pallas_call(
        functools.partial(_attn_body, tq=tq),
        grid=(b, s // tq),
        in_specs=[pl.BlockSpec((1, nh, tq, dp), lambda bi, i: (bi, 0, i, 0)),
                  pl.BlockSpec((1, nh, s, dp), lambda bi, i: (bi, 0, 0, 0)),
                  pl.BlockSpec((1, nh, s // tq, V_HEAD_DIM, tq), lambda bi, i: (bi, 0, 0, 0, 0))],
        out_specs=pl.BlockSpec((1, tq, nh * V_HEAD_DIM), lambda bi, i: (bi, i, 0)),
        out_shape=jax.ShapeDtypeStruct((b, s, nh * V_HEAD_DIM), BF16),
        scratch_shapes=[pltpu.VMEM((nh, 1, tq), F32), pltpu.VMEM((nh, 1, tq), F32),
                        pltpu.VMEM((nh, V_HEAD_DIM, tq), F32)],
        compiler_params=_cparams(("parallel", "arbitrary")),
        name="mla_attention",
    )(q, k, vt)


SCAN_VR = 16
SCAN_VQ = RWKV_HEAD_DIM // SCAN_VR
SCAN_TENSORS = 5
PREP_TS = LANES
SCAN_KEY_GROUPS = 4
SCAN_KEYS_PER_GROUP = RWKV_HEAD_DIM // SCAN_KEY_GROUPS


def _rwkv_prep_body(*refs, has_vres, ts, nb):
    if has_vres:
        (x_ref, xp_ref, vf_ref, mu_ref, w0_ref, w2_ref, a0_ref, a2_ref, g2_ref, kk_ref, ka_ref, rk_ref, e_ref,
         v0_ref, v2_ref, *ops_o, val_o, g_o, bonus_o, xt_ref, vt_ref) = refs
    else:
        (x_ref, xp_ref, mu_ref, w0_ref, w2_ref, a0_ref, a2_ref, g2_ref, kk_ref, ka_ref, rk_ref, e_ref,
         *ops_o, val_o, g_o, bonus_o, vfirst_o, xt_ref, vt_ref) = refs
    i = pl.program_id(0)
    tensor = pl.program_id(1)
    bh = nb * RWKV_HEADS

    @pl.when(tensor == 0)
    def _():
        def per_batch(b, carry):
            x = x_ref[b]
            prev = jnp.where(i > 0, xp_ref[b][7:8, :], 0.0)
            row = lax.broadcasted_iota(jnp.int32, (ts, 1), 0)
            shifted = jnp.where(row == 0, prev, pltpu.roll(x, 1, axis=0))
            xs = x + (shifted - x) * mu_ref[...]
            wd = RWKV_WIDTH
            r = xs[:, 0:wd]
            k = xs[:, wd:2 * wd]
            v = xs[:, 2 * wd:3 * wd]
            lora_in = xs[:, 3 * wd:3 * wd + LANES]
            xg = xs[:, 3 * wd + LANES:3 * wd + 2 * LANES]
            e = e_ref[...]
            zw = w0_ref[...] + _bdot(jnp.tanh(lora_in), w2_ref[...])
            nz = -zw
            softplus = jnp.maximum(nz, 0.0) + jnp.log(1.0 + jnp.exp(-jnp.abs(nz)))
            decay = jnp.exp(-jnp.exp(-softplus - 0.5))
            a_lr = jax.nn.sigmoid(a0_ref[...] + _bdot(lora_in, a2_ref[...]))
            g_o[b] = _bdot(jax.nn.sigmoid(xg), g2_ref[...])
            if has_vres:
                xvs = xs[:, RWKV_COLS:RWKV_COLS + LANES]
                v = v + (vf_ref[b] - v) * jax.nn.sigmoid(v0_ref[...] + _bdot(xvs, v2_ref[...]))
            else:
                vfirst_o[b] = v
            kk = k * kk_ref[...]
            norm = jnp.sqrt(_seg_sum(kk * kk, e))
            kk = kk / jnp.maximum(norm, 1e-12)
            k = k * (1.0 + (a_lr - 1.0) * ka_ref[...])
            bonus_o[b] = _seg_sum(r * k * rk_ref[...], e) * v
            rows = pl.ds(pl.multiple_of(b * wd, wd), wd)
            for idx, val in enumerate((-kk, decay, kk * a_lr, k, r)):
                xt_ref[idx, rows, :] = val.T
            vt_ref[rows, :] = v.T
            return carry

        lax.fori_loop(0, nb, per_batch, 0)
        for vr in range(SCAN_VR):
            slab = jnp.concatenate(
                [vt_ref[pl.ds(vq * SCAN_VR + vr, bh, stride=RWKV_HEAD_DIM), :] for vq in range(SCAN_VQ)], axis=0)
            val_o[pl.ds(vr, ts, stride=SCAN_VR), :] = slab.T

    for group, out in enumerate(ops_o):
        def emit(key, carry, group=group, out=out):
            rows = xt_ref[tensor, pl.ds(group * SCAN_KEYS_PER_GROUP + key, bh, stride=RWKV_HEAD_DIM), :]
            by_t = jnp.concatenate([rows] * SCAN_VQ, axis=0).T
            for chunk in range(SCAN_SUB):
                out[chunk, 0, key] = by_t[chunk * SCAN_TC:(chunk + 1) * SCAN_TC, :]
            return carry

        lax.fori_loop(0, SCAN_KEYS_PER_GROUP, emit, 0, unroll=4)


def _rwkv_prep(rwkv_cols, v_first, p):
    b, s, n = rwkv_cols.shape
    ts = PREP_TS
    has_vres = v_first is not None
    tok = lambda w: pl.BlockSpec((b, ts, w), lambda i, t: (0, i, 0))
    halo = pl.BlockSpec((b, 8, n), lambda i, t: (0, jnp.maximum(i * (ts // 8) - 1, 0), 0))
    consts = [p["mu"], p["w0"], p["w2"], p["a0"], p["a2"], p["g2"], p["k_k"], p["k_a"], p["r_k"], p["seg"]]
    args = [rwkv_cols, rwkv_cols]
    in_specs = [tok(n), halo]
    if has_vres:
        args.append(v_first)
        in_specs.append(tok(RWKV_WIDTH))
        consts += [p["v0"], p["v2"]]
    in_specs += [_const_spec(a.shape) for a in consts]
    n_tok_out = 2 if has_vres else 3
    return pl.pallas_call(
        functools.partial(_rwkv_prep_body, has_vres=has_vres, ts=ts, nb=b),
        grid=(s // ts, SCAN_TENSORS),
        in_specs=in_specs,
        out_specs=[pl.BlockSpec((SCAN_SUB, 1, SCAN_KEYS_PER_GROUP, SCAN_TC, LANES), lambda i, t: (i, t, 0, 0, 0))]
        * SCAN_KEY_GROUPS
        + [pl.BlockSpec((ts * SCAN_VR, LANES), lambda i, t: (i, 0))] + [tok(RWKV_WIDTH)] * n_tok_out,
        out_shape=[jax.ShapeDtypeStruct((s // SCAN_TC, SCAN_TENSORS, SCAN_KEYS_PER_GROUP, SCAN_TC, LANES), F32)]
        * SCAN_KEY_GROUPS
        + [jax.ShapeDtypeStruct((s * SCAN_VR, LANES), F32)]
        + [jax.ShapeDtypeStruct((b, s, RWKV_WIDTH), F32)] * n_tok_out,
        scratch_shapes=[pltpu.VMEM((SCAN_TENSORS, b * RWKV_WIDTH, ts), F32), pltpu.VMEM((b * RWKV_WIDTH, ts), F32)],
        compiler_params=_cparams(("arbitrary", "arbitrary")),
        name="rwkv_prep",
    )(*args, *consts)


N_ACC = 4
SCAN_TC = 32
SCAN_SUB = PREP_TS // SCAN_TC


def _scan_body(*refs, tc, nb):
    n_ops = SCAN_TENSORS * SCAN_KEY_GROUPS
    ops_refs = refs[:n_ops]
    v_ref, y_ref, s_ref, ych_ref, yt_ref = refs[n_ops:]
    c = pl.program_id(1)

    def operand(idx, kk, t):
        ref = ops_refs[idx * SCAN_KEY_GROUPS + kk // SCAN_KEYS_PER_GROUP]
        return ref[0, 0, kk % SCAN_KEYS_PER_GROUP, pl.ds(t, 1), :]

    @pl.when((pl.program_id(0) == 0) & (c == 0))
    def _():
        s_ref[...] = jnp.zeros_like(s_ref)

    def tree(acc):
        return (acc[0] + acc[1]) + (acc[2] + acc[3])

    def accumulate(acc, kk, term):
        acc[kk % N_ACC] = term if acc[kk % N_ACC] is None else acc[kk % N_ACC] + term

    acc = [None] * N_ACC
    for kk in range(RWKV_HEAD_DIM):
        accumulate(acc, kk, s_ref[kk] * operand(0, kk, 0))

    def step(t, u):
        op = lambda idx, kk: operand(idx, kk, t)
        t_next = jnp.minimum(t + 1, tc - 1)
        v = v_ref[pl.ds(pl.multiple_of(t * SCAN_VR, SCAN_VR), SCAN_VR), :]
        yacc = [None] * N_ACC
        uacc = [None] * N_ACC
        for kk in range(RWKV_HEAD_DIM):
            s_new = s_ref[kk] * op(1, kk) + u * op(2, kk) + v * op(3, kk)
            s_ref[kk] = s_new
            accumulate(yacc, kk, s_new * op(4, kk))
            accumulate(uacc, kk, s_new * operand(0, kk, t_next))
        row0 = pl.multiple_of((c * tc + t) * SCAN_VR, SCAN_VR)
        ych_ref[pl.ds(row0, SCAN_VR), :] = tree(yacc)
        return tree(uacc)

    lax.fori_loop(0, tc, step, tree(acc), unroll=4)

    @pl.when(c == SCAN_SUB - 1)
    def _():
        bh = nb * RWKV_HEADS
        for vr in range(SCAN_VR):
            lanes_by_t = ych_ref[pl.ds(vr, PREP_TS, stride=SCAN_VR), :].T
            for vq in range(SCAN_VQ):
                yt_ref[pl.ds(vq * SCAN_VR + vr, bh, stride=RWKV_HEAD_DIM), :] = lanes_by_t[vq * bh:(vq + 1) * bh, :]
        for b in range(nb):
            y_ref[b] = yt_ref[b * RWKV_WIDTH:(b + 1) * RWKV_WIDTH, :].T


def _wkv_scan(ops_groups, val, nb):
    tc = SCAN_TC
    s = ops_groups[0].shape[0] * tc
    ops_args, ops_specs = [], []
    for tensor in range(SCAN_TENSORS):
        for group in ops_groups:
            ops_args.append(group)
            ops_specs.append(pl.BlockSpec((1, 1, SCAN_KEYS_PER_GROUP, tc, LANES),
                                          lambda i, c, tensor=tensor: (i * SCAN_SUB + c, tensor, 0, 0, 0)))
    return pl.pallas_call(
        functools.partial(_scan_body, tc=tc, nb=nb),
        grid=(s // PREP_TS, SCAN_SUB),
        in_specs=ops_specs + [pl.BlockSpec((tc * SCAN_VR, LANES), lambda i, c: (i * SCAN_SUB + c, 0))],
        out_specs=pl.BlockSpec((nb, PREP_TS, RWKV_WIDTH), lambda i, c: (0, i, 0)),
        out_shape=jax.ShapeDtypeStruct((nb, s, RWKV_WIDTH), F32),
        scratch_shapes=[pltpu.VMEM((RWKV_HEAD_DIM, SCAN_VR, LANES), F32),
                        pltpu.VMEM((PREP_TS * SCAN_VR, LANES), F32),
                        pltpu.VMEM((nb * RWKV_WIDTH, PREP_TS), F32)],
        compiler_params=_cparams(("arbitrary", "arbitrary")),
        name="wkv_scan",
    )(*ops_args, val)


def _merge_body(x_ref, gate_ref, att_ref, y_ref, g_ref, bonus_ref, conv_ref, convp_ref,
                lnw_ref, lnb_ref, e_ref, cw_ref, wa_ref, wb_ref, wc_ref, wo_ref, out_ref, *, ts):
    i = pl.program_id(1)
    e = e_ref[...]
    y = y_ref[0]
    inv_n = 1.0 / RWKV_HEAD_DIM
    mean = _seg_sum(y, e) * inv_n
    d = y - mean
    var = _seg_sum(d * d, e) * inv_n
    yn = d * lax.rsqrt(var + GN_EPS) * lnw_ref[...] + lnb_ref[...] + bonus_ref[0]
    ob = _bdot(yn * g_ref[0], wb_ref[...])

    cw = CONV_WIDTH
    c = conv_ref[0]
    u = c[:, cw:2 * cw] * c[:, 2 * cw:3 * cw]
    cp = convp_ref[0]
    up = jnp.where(i > 0, cp[:, cw:2 * cw] * cp[:, 2 * cw:3 * cw], 0.0)
    p6 = up[6:7, :]
    p7 = up[7:8, :]
    row = lax.broadcasted_iota(jnp.int32, (ts, 1), 0)
    u1 = jnp.where(row == 0, p7, pltpu.roll(u, 1, axis=0))
    u2 = jnp.where(row == 0, p6, jnp.where(row == 1, p7, pltpu.roll(u, 2, axis=0)))
    taps = cw_ref[...]
    yc = taps[0:1, :] * u2 + taps[1:2, :] * u1 + taps[2:3, :] * u
    oc = _bdot(c[:, 0:cw] * yc, wc_ref[...])

    oa = jnp.dot(att_ref[0], wa_ref[...], preferred_element_type=F32)
    gates = gate_ref[0].astype(F32)
    dm = D_MODEL
    merged = gates[:, 0:dm] * oa + gates[:, dm:2 * dm] * ob + gates[:, 2 * dm:3 * dm] * oc
    out_ref[0] = x_ref[0] + _bdot(merged, wo_ref[...])


def _merge(x, gates, att, y, g, bonus, conv_cols, p, ts=256):
    b, s, d = x.shape
    tok = lambda w: pl.BlockSpec((1, ts, w), lambda bi, i: (bi, i, 0))
    nconv = conv_cols.shape[-1]
    halo = pl.BlockSpec((1, 8, nconv), lambda bi, i: (bi, jnp.maximum(i * (ts // 8) - 1, 0), 0))
    consts = (p["ln_w"], p["ln_b"], p["seg"], p["conv_w"], p["mla_w_o"], p["rwkv_w_o"], p["conv_w_o"], p["w_out"])
    return pl.pallas_call(
        functools.partial(_merge_body, ts=ts),
        grid=(b, s // ts),
        in_specs=[tok(d), tok(GATE_COLS), tok(att.shape[-1]), tok(RWKV_WIDTH), tok(RWKV_WIDTH), tok(RWKV_WIDTH),
                  tok(nconv), halo] + [_const_spec(a.shape) for a in consts],
        out_specs=tok(d),
        out_shape=jax.ShapeDtypeStruct((b, s, d), F32),
        compiler_params=_cparams(("parallel", "parallel")),
        name="branch_merge",
    )(x, gates, att, y, g, bonus, conv_cols, conv_cols, *consts)


def _mlp_body(x_ref, g_ref, wu_ref, wd_ref, o_ref):
    x = x_ref[...]
    ms = jnp.mean(x * x, axis=-1, keepdims=True)
    h = (x * lax.rsqrt(ms + NORM_EPS) * g_ref[...]).astype(BF16)
    up = jnp.dot(h, wu_ref[...], preferred_element_type=F32)
    act = jnp.square(jnp.maximum(up, 0.0)).astype(BF16)
    o_ref[...] = x + jnp.dot(act, wd_ref[...], preferred_element_type=F32)


def _mlp(x2d, gain, w_up, w_down, tm=256):
    t, d = x2d.shape
    row = pl.BlockSpec((tm, d), lambda i: (i, 0))
    return pl.pallas_call(
        _mlp_body,
        grid=(t // tm,),
        in_specs=[row, _const_spec((1, d)), _const_spec(w_up.shape), _const_spec(w_down.shape)],
        out_specs=row,
        out_shape=jax.ShapeDtypeStruct((t, d), F32),
        compiler_params=_cparams(("parallel",)),
        name="mlp",
    )(x2d, gain, w_up, w_down)


def _rope_partner_cols(w):
    half = QK_ROPE_DIM // 2
    return jnp.concatenate([-w[..., half:], w[..., :half]], axis=-1)


def _pad_lanes(w, lo, total=HEAD_PAD):
    n = w.shape[-1]
    pad = [(0, 0)] * (w.ndim - 1) + [(lo, total - lo - n)]
    return jnp.pad(w, pad)


def _layer_params(l, attn_norm, w_in, mla_q_a_norm, mla_wq_b, mla_kv_a_norm, mla_wkv_b, mla_q_norm, mla_k_norm,
                  mla_w_o, rwkv_mu, rwkv_w0, rwkv_w2, rwkv_a0, rwkv_a2, rwkv_g2, rwkv_k_k, rwkv_k_a, rwkv_r_k,
                  rwkv_ln_w, rwkv_ln_b, rwkv_w_o, rwkv_v1, rwkv_v_mu, rwkv_v0, rwkv_v2, conv_w, conv_w_o, w_out,
                  mlp_norm, w_up, w_down):
    p = {}
    row = lambda a: a.reshape(1, -1).astype(F32)
    w = w_in[l]
    o_mla = GATE_COLS
    o_rwkv = o_mla + MLA_COLS
    o_conv = o_rwkv + RWKV_COLS
    p["attn_norm"] = row(attn_norm[l])
    p["w_gate"] = w[:, :GATE_COLS].astype(BF16)
    w_kpe = w[:, o_mla + Q_LORA_RANK + KV_LORA_RANK:o_rwkv]
    p["w_mla"] = jnp.concatenate(
        [w[:, o_mla:o_mla + Q_LORA_RANK + KV_LORA_RANK], _pad_lanes(w_kpe, QK_NOPE_DIM),
         _pad_lanes(_rope_partner_cols(w_kpe), QK_NOPE_DIM)], axis=1).astype(BF16)
    w_rwkv = w[:, o_rwkv:o_conv]
    mu = rwkv_mu[l]
    if l > 0:
        w_rwkv = jnp.concatenate([w_rwkv, _pad_lanes(rwkv_v1[l - 1], 0)], axis=1)
        mu = jnp.concatenate([mu, _pad_lanes(rwkv_v_mu[l - 1], 0)])
        p["v0"] = row(rwkv_v0[l - 1])
        p["v2"] = jnp.pad(rwkv_v2[l - 1], ((0, LANES - MV_LORA), (0, 0)))
    p["w_rwkv"] = w_rwkv.astype(BF16)
    p["mu"] = row(mu)
    p["w_conv"] = w[:, o_conv:].astype(BF16)

    scale = QK_HEAD_DIM ** -0.5 * float(np.log2(np.e))
    wq = mla_wq_b[l].reshape(Q_LORA_RANK, MLA_HEADS, QK_HEAD_DIM)
    p["wq"] = _pad_lanes(wq, 0).reshape(Q_LORA_RANK, -1).astype(BF16)
    p["wqr"] = _pad_lanes(_rope_partner_cols(wq[..., QK_NOPE_DIM:]), QK_NOPE_DIM).reshape(Q_LORA_RANK, -1).astype(BF16)
    wkv = mla_wkv_b[l].reshape(KV_LORA_RANK, MLA_HEADS, QK_NOPE_DIM + V_HEAD_DIM)
    p["wk"] = _pad_lanes(wkv[..., :QK_NOPE_DIM], 0).reshape(KV_LORA_RANK, -1).astype(BF16)
    p["wv"] = wkv[..., QK_NOPE_DIM:].reshape(KV_LORA_RANK, -1).astype(BF16)
    p["qan"] = row(mla_q_a_norm[l])
    p["kvan"] = row(mla_kv_a_norm[l])
    swap = lambda g: jnp.concatenate([g[QK_ROPE_DIM // 2:], g[:QK_ROPE_DIM // 2]])
    gq, gk = mla_q_norm[l] * scale, mla_k_norm[l]
    p["gq"] = row(_pad_lanes(gq, 0))
    p["gqr"] = row(_pad_lanes(swap(gq[QK_NOPE_DIM:]), QK_NOPE_DIM))
    p["gk"] = row(_pad_lanes(gk, 0))
    p["gkr"] = row(_pad_lanes(swap(gk[QK_NOPE_DIM:]), QK_NOPE_DIM))
    p["mla_w_o"] = mla_w_o[l].astype(BF16)

    p["w0"] = row(rwkv_w0[l])
    p["w2"] = jnp.pad(rwkv_w2[l], ((0, AAA_LORA), (0, 0)))
    p["a0"] = row(rwkv_a0[l])
    p["a2"] = jnp.pad(rwkv_a2[l], ((DECAY_LORA, 0), (0, 0)))
    p["g2"] = rwkv_g2[l]
    p["k_k"] = row(rwkv_k_k[l])
    p["k_a"] = row(rwkv_k_a[l])
    p["r_k"] = row(rwkv_r_k[l])
    p["ln_w"] = row(rwkv_ln_w[l])
    p["ln_b"] = row(rwkv_ln_b[l])
    head_of = np.arange(RWKV_WIDTH) // RWKV_HEAD_DIM
    p["seg"] = jnp.asarray(head_of[:, None] == head_of[None, :], F32)
    p["rwkv_w_o"] = rwkv_w_o[l].astype(BF16)
    p["conv_w"] = conv_w[l].astype(F32)
    p["conv_w_o"] = conv_w_o[l].astype(BF16)
    p["w_out"] = w_out[l].astype(BF16)
    p["mlp_norm"] = row(mlp_norm[l])
    p["w_up"] = w_up[l].astype(BF16)
    p["w_down"] = w_down[l].astype(BF16)
    return p


def _rope_tables(positions):
    half = QK_ROPE_DIM // 2
    freqs = ROPE_THETA ** (-(jnp.arange(half, dtype=F32) * 2.0 / QK_ROPE_DIM))
    ang = positions.astype(F32)[..., None] * freqs
    cos, sin = jnp.cos(ang), jnp.sin(ang)
    ones = jnp.ones(positions.shape + (QK_NOPE_DIM,), F32)
    tail = jnp.ones(positions.shape + (HEAD_PAD - QK_HEAD_DIM,), F32)
    cosf = jnp.concatenate([ones, cos, cos, tail], axis=-1)
    sinf = jnp.concatenate([0 * ones, sin, sin, 0 * tail], axis=-1)
    return cosf, sinf


def kernel(x, positions, attn_norm, w_in, mla_q_a_norm, mla_wq_b, mla_kv_a_norm, mla_wkv_b, mla_q_norm, mla_k_norm, mla_w_o, rwkv_mu, rwkv_w0, rwkv_w2, rwkv_a0, rwkv_a2, rwkv_g2, rwkv_k_k, rwkv_k_a, rwkv_r_k, rwkv_ln_w, rwkv_ln_b, rwkv_w_o, rwkv_v1, rwkv_v_mu, rwkv_v0, rwkv_v2, conv_w, conv_w_o, w_out, mlp_norm, w_up, w_down):
    weights = (attn_norm, w_in, mla_q_a_norm, mla_wq_b, mla_kv_a_norm, mla_wkv_b, mla_q_norm, mla_k_norm, mla_w_o,
               rwkv_mu, rwkv_w0, rwkv_w2, rwkv_a0, rwkv_a2, rwkv_g2, rwkv_k_k, rwkv_k_a, rwkv_r_k, rwkv_ln_w,
               rwkv_ln_b, rwkv_w_o, rwkv_v1, rwkv_v_mu, rwkv_v0, rwkv_v2, conv_w, conv_w_o, w_out, mlp_norm,
               w_up, w_down)
    b, s, d = x.shape
    cosf, sinf = _rope_tables(positions)
    v_first = None
    for l in range(DEPTH):
        p = _layer_params(l, *weights)
        gates, mla_cols, rwkv_cols, conv_cols = _in_proj(
            x.reshape(b * s, d), p["attn_norm"], p["w_gate"], p["w_mla"], p["w_rwkv"], p["w_conv"])
        q, k, v = _mla_prep(mla_cols.reshape(b, s, -1), cosf, sinf, p)
        att = _attention(q, k, v)
        prep = _rwkv_prep(rwkv_cols.reshape(b, s, -1), v_first, p)
        scan_ops = prep[:SCAN_KEY_GROUPS]
        scan_val, g_, bonus = prep[SCAN_KEY_GROUPS:SCAN_KEY_GROUPS + 3]
        if l == 0:
            v_first = prep[SCAN_KEY_GROUPS + 3]
        y = _wkv_scan(scan_ops, scan_val, b)
        x = _merge(x, gates.reshape(b, s, -1), att, y, g_, bonus, conv_cols.reshape(b, s, -1), p)
        x = _mlp(x.reshape(b * s, d), p["mlp_norm"], p["w_up"], p["w_down"]).reshape(b, s, d)
    return x
```

```python
import functools

import jax
import jax.numpy as jnp
import numpy as np
from jax import lax
from jax.experimental import pallas as pl
from jax.experimental.pallas import tpu as pltpu

D_MODEL = 1024
DEPTH = 2
MLA_HEADS = 8
QK_NOPE_DIM = 64
QK_ROPE_DIM = 32
QK_HEAD_DIM = QK_NOPE_DIM + QK_ROPE_DIM
V_HEAD_DIM = 64
Q_LORA_RANK = 384
KV_LORA_RANK = 256
ROPE_THETA = 10000.0
RWKV_HEAD_DIM = 64
RWKV_HEADS = 4
RWKV_WIDTH = RWKV_HEADS * RWKV_HEAD_DIM
DECAY_LORA = 64
AAA_LORA = 64
GATE_LORA = 128
MV_LORA = 32
GN_EPS = 64e-5
CONV_WIDTH = 256
CONV_K = 3
D_FF = 4 * D_MODEL
N_BRANCH = 3
NORM_EPS = 1e-6
GATE_COLS = N_BRANCH * D_MODEL
MLA_COLS = Q_LORA_RANK + KV_LORA_RANK + QK_ROPE_DIM
RWKV_COLS = 3 * RWKV_WIDTH + DECAY_LORA + AAA_LORA + GATE_LORA

LANES = 128
HEAD_PAD = LANES
VT_ROWS = V_HEAD_DIM + 16
MLA_OUT_COLS = Q_LORA_RANK + KV_LORA_RANK + 2 * LANES
VMEM_LIMIT = 56 * 1024 * 1024

F32 = jnp.float32
BF16 = jnp.bfloat16


def _cparams(sem):
    return pltpu.CompilerParams(dimension_semantics=sem, vmem_limit_bytes=VMEM_LIMIT)


def _const_spec(shape):
    nd = len(shape)
    return pl.BlockSpec(shape, lambda *_: (0,) * nd, pipeline_mode=pl.Buffered(1))


def _bdot(a, b):
    return jnp.dot(a.astype(BF16), b.astype(BF16), preferred_element_type=F32)


def _seg_sum(x, e):
    return jnp.dot(x, e, preferred_element_type=F32, precision=lax.Precision.HIGHEST)


def _in_proj_body(x_ref, g_ref, wg_ref, wm_ref, wr_ref, wc_ref, gate_ref, mla_ref, rwkv_ref, conv_ref):
    x = x_ref[...]
    ms = jnp.mean(x * x, axis=-1, keepdims=True)
    h = (x * lax.rsqrt(ms + NORM_EPS) * g_ref[...]).astype(BF16)
    gate_ref[...] = jax.nn.sigmoid(jnp.dot(h, wg_ref[...], preferred_element_type=F32)).astype(gate_ref.dtype)
    mla_ref[...] = jnp.dot(h, wm_ref[...], preferred_element_type=F32)
    rwkv_ref[...] = jnp.dot(h, wr_ref[...], preferred_element_type=F32)
    conv_ref[...] = jnp.dot(h, wc_ref[...], preferred_element_type=F32)


def _in_proj(x2d, gain, wg, wm, wr, wc, tm=256):
    t, d = x2d.shape
    row = lambda n: pl.BlockSpec((tm, n), lambda i: (i, 0))
    widths = (wg.shape[1], wm.shape[1], wr.shape[1], wc.shape[1])
    return pl.pallas_call(
        _in_proj_body,
        grid=(t // tm,),
        in_specs=[row(d), _const_spec((1, d))] + [_const_spec(w.shape) for w in (wg, wm, wr, wc)],
        out_specs=[row(n) for n in widths],
        out_shape=[jax.ShapeDtypeStruct((t, widths[0]), BF16)]
        + [jax.ShapeDtypeStruct((t, n), F32) for n in widths[1:]],
        compiler_params=_cparams(("parallel",)),
        name="in_proj",
    )(x2d, gain, wg, wm, wr, wc)


def _mla_prep_body(c_ref, cos_ref, sin_ref, qan_ref, kvan_ref, wq_ref, wqr_ref, wk_ref, wv_ref,
                   gq_ref, gqr_ref, gk_ref, gkr_ref, q_ref, k_ref, v_ref):
    c = c_ref[0]
    cq = c[:, :Q_LORA_RANK]
    ckv = c[:, Q_LORA_RANK:Q_LORA_RANK + KV_LORA_RANK]
    kpe = c[:, Q_LORA_RANK + KV_LORA_RANK:Q_LORA_RANK + KV_LORA_RANK + LANES]
    kper = c[:, Q_LORA_RANK + KV_LORA_RANK + LANES:]

    def rms(z, g):
        return (z * lax.rsqrt(jnp.mean(z * z, axis=-1, keepdims=True) + NORM_EPS) * g).astype(BF16)

    cqn = rms(cq, qan_ref[...])
    ckvn = rms(ckv, kvan_ref[...])
    q = jnp.dot(cqn, wq_ref[...], preferred_element_type=F32)
    qr = jnp.dot(cqn, wqr_ref[...], preferred_element_type=F32)
    kn = jnp.dot(ckvn, wk_ref[...], preferred_element_type=F32)
    v = jnp.dot(ckvn, wv_ref[...], preferred_element_type=F32)
    cosf = cos_ref[0]
    sinf = sin_ref[0]
    gqc = gq_ref[...] * cosf
    gqs = gqr_ref[...] * sinf
    gkc = gk_ref[...] * cosf
    kper_s = kper * gkr_ref[...] * sinf
    inv_dim = 1.0 / QK_HEAD_DIM
    vt = v.T
    pad_row = lax.broadcasted_iota(jnp.int32, (VT_ROWS - V_HEAD_DIM, vt.shape[1]), 0)
    ones_rows = jnp.where(pad_row == 0, 1.0, 0.0)
    for h in range(MLA_HEADS):
        sl = slice(h * HEAD_PAD, (h + 1) * HEAD_PAD)
        qh = q[:, sl]
        rq = lax.rsqrt(jnp.sum(qh * qh, axis=-1, keepdims=True) * inv_dim + NORM_EPS)
        q_ref[0, h] = (rq * (qh * gqc + qr[:, sl] * gqs)).astype(q_ref.dtype)
        kh = kn[:, sl] + kpe
        rk = lax.rsqrt(jnp.sum(kh * kh, axis=-1, keepdims=True) * inv_dim + NORM_EPS)
        k_ref[0, h] = (rk * (kh * gkc + kper_s)).astype(k_ref.dtype)
        v_ref[0, h, 0] = jnp.concatenate([vt[h * V_HEAD_DIM:(h + 1) * V_HEAD_DIM, :], ones_rows],
                                         axis=0).astype(v_ref.dtype)


def _mla_prep(mla_cols, cosf, sinf, p):
    b, s, n = mla_cols.shape
    tm = ATTN_BLOCK
    tok = lambda w: pl.BlockSpec((1, tm, w), lambda bi, i: (bi, i, 0))
    head = lambda w: pl.BlockSpec((1, MLA_HEADS, tm, w), lambda bi, i: (bi, 0, i, 0))
    consts = (p["qan"], p["kvan"], p["wq"], p["wqr"], p["wk"], p["wv"], p["gq"], p["gqr"], p["gk"], p["gkr"])
    return pl.pallas_call(
        _mla_prep_body,
        grid=(b, s // tm),
        in_specs=[tok(n), tok(LANES), tok(LANES)] + [_const_spec(a.shape) for a in consts],
        out_specs=[head(HEAD_PAD), head(HEAD_PAD),
                   pl.BlockSpec((1, MLA_HEADS, 1, VT_ROWS, tm), lambda bi, i: (bi, 0, i, 0, 0))],
        out_shape=[jax.ShapeDtypeStruct((b, MLA_HEADS, s, HEAD_PAD), BF16),
                   jax.ShapeDtypeStruct((b, MLA_HEADS, s, HEAD_PAD), BF16),
                   jax.ShapeDtypeStruct((b, MLA_HEADS, s // tm, VT_ROWS, tm), BF16)],
        compiler_params=_cparams(("parallel", "parallel")),
        name="mla_prep",
    )(mla_cols, cosf, sinf, *consts)


MASK_VALUE = -1e30


ATTN_BLOCK = 256
ATTN_LOOKAHEAD = 4


def _attn_body(q_ref, k_ref, vt_ref, o_ref, m_ref, acc_ref, *, tq):
    i = pl.program_id(1)
    m_ref[...] = jnp.full(m_ref.shape, MASK_VALUE, F32)
    acc_ref[...] = jnp.zeros(acc_ref.shape, F32)
    key_idx = lax.broadcasted_iota(jnp.int32, (tq, tq), 0)
    qry_idx = lax.broadcasted_iota(jnp.int32, (tq, tq), 1)
    causal = key_idx <= qry_idx

    def block(j, masked):
        start = pl.multiple_of(j * tq, tq)

        def scores_t(h):
            kj = k_ref[0, h, pl.ds(start, tq), :]
            return lax.dot_general(kj, q_ref[0, h], (((1,), (1,)), ((), ())), preferred_element_type=F32)

        pending = [scores_t(h) for h in range(ATTN_LOOKAHEAD)]
        for h in range(MLA_HEADS):
            st = pending.pop(0)
            if h + ATTN_LOOKAHEAD < MLA_HEADS:
                pending.append(scores_t(h + ATTN_LOOKAHEAD))
            if masked:
                st = jnp.where(causal, st, MASK_VALUE)
            m_prev = m_ref[h]
            m_new = jnp.maximum(m_prev, jnp.max(st, axis=0, keepdims=True))
            alpha = jnp.exp2(m_prev - m_new)
            pt = jnp.exp2(st - m_new)
            acc_ref[h] = alpha * acc_ref[h] + jnp.dot(vt_ref[0, h, j], pt.astype(BF16), preferred_element_type=F32)
            m_ref[h] = m_new

    def loop_body(j, carry):
        block(j, False)
        return carry

    lax.fori_loop(0, i, loop_body, 0)
    block(i, True)
    for h in range(MLA_HEADS):
        out_t = acc_ref[h, 0:V_HEAD_DIM, :] / acc_ref[h, V_HEAD_DIM:V_HEAD_DIM + 1, :]
        o_ref[0, :, h * V_HEAD_DIM:(h + 1) * V_HEAD_DIM] = out_t.T.astype(o_ref.dtype)


def _attention(q, k, vt):
    b, nh, s, dp = q.shape
    tq = ATTN_BLOCK
    return pl.pallas_call(
        functools.partial(_attn_body, tq=tq),
        grid=(b, s // tq),
        in_specs=[pl.BlockSpec((1, nh, tq, dp), lambda bi, i: (bi, 0, i, 0)),
                  pl.BlockSpec((1, nh, s, dp), lambda bi, i: (bi, 0, 0, 0)),
                  pl.BlockSpec((1, nh, s // tq, VT_ROWS, tq), lambda bi, i: (bi, 0, 0, 0, 0))],
        out_specs=pl.BlockSpec((1, tq, nh * V_HEAD_DIM), lambda bi, i: (bi, i, 0)),
        out_shape=jax.ShapeDtypeStruct((b, s, nh * V_HEAD_DIM), BF16),
        scratch_shapes=[pltpu.VMEM((nh, 1, tq), F32), pltpu.VMEM((nh, VT_ROWS, tq), F32)],
        compiler_params=_cparams(("parallel", "arbitrary")),
        name="mla_attention",
    )(q, k, vt)


SCAN_VR = 16
SCAN_VQ = RWKV_HEAD_DIM // SCAN_VR
SCAN_TENSORS = 5
PREP_TS = LANES
SCAN_KEY_GROUPS = 4
SCAN_KEYS_PER_GROUP = RWKV_HEAD_DIM // SCAN_KEY_GROUPS


def _rwkv_prep_body(*refs, has_vres, ts, nb):
    if has_vres:
        (x_ref, xp_ref, vf_ref, mu_ref, w0_ref, w2_ref, a0_ref, a2_ref, g2_ref, kk_ref, ka_ref, rk_ref, e_ref,
         v0_ref, v2_ref, *ops_o, val_o, g_o, bonus_o, xt_ref, vt_ref) = refs
    else:
        (x_ref, xp_ref, mu_ref, w0_ref, w2_ref, a0_ref, a2_ref, g2_ref, kk_ref, ka_ref, rk_ref, e_ref,
         *ops_o, val_o, g_o, bonus_o, vfirst_o, xt_ref, vt_ref) = refs
    i = pl.program_id(0)
    tensor = pl.program_id(1)
    bh = nb * RWKV_HEADS

    @pl.when(tensor == 0)
    def _():
        def per_batch(b, carry):
            x = x_ref[b]
            prev = jnp.where(i > 0, xp_ref[b][7:8, :], 0.0)
            row = lax.broadcasted_iota(jnp.int32, (ts, 1), 0)
            shifted = jnp.where(row == 0, prev, pltpu.roll(x, 1, axis=0))
            xs = x + (shifted - x) * mu_ref[...]
            wd = RWKV_WIDTH
            r = xs[:, 0:wd]
            k = xs[:, wd:2 * wd]
            v = xs[:, 2 * wd:3 * wd]
            lora_in = xs[:, 3 * wd:3 * wd + LANES]
            xg = xs[:, 3 * wd + LANES:3 * wd + 2 * LANES]
            e = e_ref[...]
            zw = w0_ref[...] + _bdot(jnp.tanh(lora_in), w2_ref[...])
            nz = -zw
            softplus = jnp.maximum(nz, 0.0) + jnp.log(1.0 + jnp.exp(-jnp.abs(nz)))
            decay = jnp.exp(-jnp.exp(-softplus - 0.5))
            a_lr = jax.nn.sigmoid(a0_ref[...] + _bdot(lora_in, a2_ref[...]))
            g_o[b] = _bdot(jax.nn.sigmoid(xg), g2_ref[...])
            if has_vres:
                xvs = xs[:, RWKV_COLS:RWKV_COLS + LANES]
                v = v + (vf_ref[b] - v) * jax.nn.sigmoid(v0_ref[...] + _bdot(xvs, v2_ref[...]))
            else:
                vfirst_o[b] = v
            kk = k * kk_ref[...]
            norm = jnp.sqrt(_seg_sum(kk * kk, e))
            kk = kk / jnp.maximum(norm, 1e-12)
            k = k * (1.0 + (a_lr - 1.0) * ka_ref[...])
            bonus_o[b] = _seg_sum(r * k * rk_ref[...], e) * v
            rows = pl.ds(pl.multiple_of(b * wd, wd), wd)
            for idx, val in enumerate((-kk, decay, kk * a_lr, k, r)):
                xt_ref[idx, rows, :] = val.T
            vt_ref[rows, :] = v.T
            return carry

        lax.fori_loop(0, nb, per_batch, 0)
        for vr in range(SCAN_VR):
            slab = jnp.concatenate(
                [vt_ref[pl.ds(vq * SCAN_VR + vr, bh, stride=RWKV_HEAD_DIM), :] for vq in range(SCAN_VQ)], axis=0)
            val_o[pl.ds(vr, ts, stride=SCAN_VR), :] = slab.T

    for group, out in enumerate(ops_o):
        def emit(key, carry, group=group, out=out):
            rows = xt_ref[tensor, pl.ds(group * SCAN_KEYS_PER_GROUP + key, bh, stride=RWKV_HEAD_DIM), :]
            by_t = jnp.concatenate([rows] * SCAN_VQ, axis=0).T
            for chunk in range(SCAN_SUB):
                out[chunk, 0, key] = by_t[chunk * SCAN_TC:(chunk + 1) * SCAN_TC, :]
            return carry

        lax.fori_loop(0, SCAN_KEYS_PER_GROUP, emit, 0, unroll=16)


def _rwkv_prep(rwkv_cols, v_first, p):
    b, s, n = rwkv_cols.shape
    ts = PREP_TS
    has_vres = v_first is not None
    tok = lambda w: pl.BlockSpec((b, ts, w), lambda i, t: (0, i, 0))
    halo = pl.BlockSpec((b, 8, n), lambda i, t: (0, jnp.maximum(i * (ts // 8) - 1, 0), 0))
    consts = [p["mu"], p["w0"], p["w2"], p["a0"], p["a2"], p["g2"], p["k_k"], p["k_a"], p["r_k"], p["seg"]]
    args = [rwkv_cols, rwkv_cols]
    in_specs = [tok(n), halo]
    if has_vres:
        args.append(v_first)
        in_specs.append(tok(RWKV_WIDTH))
        consts += [p["v0"], p["v2"]]
    in_specs += [_const_spec(a.shape) for a in consts]
    n_tok_out = 2 if has_vres else 3
    return pl.pallas_call(
        functools.partial(_rwkv_prep_body, has_vres=has_vres, ts=ts, nb=b),
        grid=(s // ts, SCAN_TENSORS),
        in_specs=in_specs,
        out_specs=[pl.BlockSpec((SCAN_SUB, 1, SCAN_KEYS_PER_GROUP, SCAN_TC, LANES), lambda i, t: (i, t, 0, 0, 0))]
        * SCAN_KEY_GROUPS
        + [pl.BlockSpec((ts * SCAN_VR, LANES), lambda i, t: (i, 0))] + [tok(RWKV_WIDTH)] * n_tok_out,
        out_shape=[jax.ShapeDtypeStruct((s // SCAN_TC, SCAN_TENSORS, SCAN_KEYS_PER_GROUP, SCAN_TC, LANES), F32)]
        * SCAN_KEY_GROUPS
        + [jax.ShapeDtypeStruct((s * SCAN_VR, LANES), F32)]
        + [jax.ShapeDtypeStruct((b, s, RWKV_WIDTH), F32)] * n_tok_out,
        scratch_shapes=[pltpu.VMEM((SCAN_TENSORS, b * RWKV_WIDTH, ts), F32), pltpu.VMEM((b * RWKV_WIDTH, ts), F32)],
        compiler_params=_cparams(("arbitrary", "arbitrary")),
        name="rwkv_prep",
    )(*args, *consts)


N_ACC = 4
SCAN_TC = 32
SCAN_SUB = PREP_TS // SCAN_TC


def _scan_body(*refs, tc, nb):
    n_ops = SCAN_TENSORS * SCAN_KEY_GROUPS
    ops_refs = refs[:n_ops]
    v_ref, y_ref, s_ref, ych_ref, yt_ref = refs[n_ops:]
    c = pl.program_id(1)

    def operand(idx, kk, t):
        ref = ops_refs[idx * SCAN_KEY_GROUPS + kk // SCAN_KEYS_PER_GROUP]
        return ref[0, 0, kk % SCAN_KEYS_PER_GROUP, pl.ds(t, 1), :]

    @pl.when((pl.program_id(0) == 0) & (c == 0))
    def _():
        s_ref[...] = jnp.zeros_like(s_ref)

    def tree(acc):
        return (acc[0] + acc[1]) + (acc[2] + acc[3])

    def accumulate(acc, kk, term):
        acc[kk % N_ACC] = term if acc[kk % N_ACC] is None else acc[kk % N_ACC] + term

    acc = [None] * N_ACC
    for kk in range(RWKV_HEAD_DIM):
        accumulate(acc, kk, s_ref[kk] * operand(0, kk, 0))

    def step(t, u):
        op = lambda idx, kk: operand(idx, kk, t)
        t_next = jnp.minimum(t + 1, tc - 1)
        v = v_ref[pl.ds(pl.multiple_of(t * SCAN_VR, SCAN_VR), SCAN_VR), :]
        yacc = [None] * N_ACC
        uacc = [None] * N_ACC
        for kk in range(RWKV_HEAD_DIM):
            s_new = s_ref[kk] * op(1, kk) + u * op(2, kk) + v * op(3, kk)
            s_ref[kk] = s_new
            accumulate(yacc, kk, s_new * op(4, kk))
            accumulate(uacc, kk, s_new * operand(0, kk, t_next))
        row0 = pl.multiple_of((c * tc + t) * SCAN_VR, SCAN_VR)
        ych_ref[pl.ds(row0, SCAN_VR), :] = tree(yacc)
        return tree(uacc)

    lax.fori_loop(0, tc, step, tree(acc), unroll=8)

    @pl.when(c == SCAN_SUB - 1)
    def _():
        bh = nb * RWKV_HEADS
        for vr in range(SCAN_VR):
            lanes_by_t = ych_ref[pl.ds(vr, PREP_TS, stride=SCAN_VR), :].T
            for vq in range(SCAN_VQ):
                yt_ref[pl.ds(vq * SCAN_VR + vr, bh, stride=RWKV_HEAD_DIM), :] = lanes_by_t[vq * bh:(vq + 1) * bh, :]
        for b in range(nb):
            y_ref[b] = yt_ref[b * RWKV_WIDTH:(b + 1) * RWKV_WIDTH, :].T


def _wkv_scan(ops_groups, val, nb):
    tc = SCAN_TC
    s = ops_groups[0].shape[0] * tc
    ops_args, ops_specs = [], []
    for tensor in range(SCAN_TENSORS):
        for group in ops_groups:
            ops_args.append(group)
            ops_specs.append(pl.BlockSpec((1, 1, SCAN_KEYS_PER_GROUP, tc, LANES),
                                          lambda i, c, tensor=tensor: (i * SCAN_SUB + c, tensor, 0, 0, 0)))
    return pl.pallas_call(
        functools.partial(_scan_body, tc=tc, nb=nb),
        grid=(s // PREP_TS, SCAN_SUB),
        in_specs=ops_specs + [pl.BlockSpec((tc * SCAN_VR, LANES), lambda i, c: (i * SCAN_SUB + c, 0))],
        out_specs=pl.BlockSpec((nb, PREP_TS, RWKV_WIDTH), lambda i, c: (0, i, 0)),
        out_shape=jax.ShapeDtypeStruct((nb, s, RWKV_WIDTH), F32),
        scratch_shapes=[pltpu.VMEM((RWKV_HEAD_DIM, SCAN_VR, LANES), F32),
                        pltpu.VMEM((PREP_TS * SCAN_VR, LANES), F32),
                        pltpu.VMEM((nb * RWKV_WIDTH, PREP_TS), F32)],
        compiler_params=_cparams(("arbitrary", "arbitrary")),
        name="wkv_scan",
    )(*ops_args, val)


def _merge_body(x_ref, gate_ref, att_ref, y_ref, g_ref, bonus_ref, conv_ref, convp_ref,
                lnw_ref, lnb_ref, e_ref, cw_ref, wa_ref, wb_ref, wc_ref, wo_ref, out_ref, *, ts):
    i = pl.program_id(1)
    e = e_ref[...]
    y = y_ref[0]
    inv_n = 1.0 / RWKV_HEAD_DIM
    mean = _seg_sum(y, e) * inv_n
    d = y - mean
    var = _seg_sum(d * d, e) * inv_n
    yn = d * lax.rsqrt(var + GN_EPS) * lnw_ref[...] + lnb_ref[...] + bonus_ref[0]
    ob = _bdot(yn * g_ref[0], wb_ref[...])

    cw = CONV_WIDTH
    c = conv_ref[0]
    u = c[:, cw:2 * cw] * c[:, 2 * cw:3 * cw]
    cp = convp_ref[0]
    up = jnp.where(i > 0, cp[:, cw:2 * cw] * cp[:, 2 * cw:3 * cw], 0.0)
    p6 = up[6:7, :]
    p7 = up[7:8, :]
    row = lax.broadcasted_iota(jnp.int32, (ts, 1), 0)
    u1 = jnp.where(row == 0, p7, pltpu.roll(u, 1, axis=0))
    u2 = jnp.where(row == 0, p6, jnp.where(row == 1, p7, pltpu.roll(u, 2, axis=0)))
    taps = cw_ref[...]
    yc = taps[0:1, :] * u2 + taps[1:2, :] * u1 + taps[2:3, :] * u
    oc = _bdot(c[:, 0:cw] * yc, wc_ref[...])

    oa = jnp.dot(att_ref[0], wa_ref[...], preferred_element_type=F32)
    gates = gate_ref[0].astype(F32)
    dm = D_MODEL
    merged = gates[:, 0:dm] * oa + gates[:, dm:2 * dm] * ob + gates[:, 2 * dm:3 * dm] * oc
    out_ref[0] = x_ref[0] + _bdot(merged, wo_ref[...])


def _merge(x, gates, att, y, g, bonus, conv_cols, p, ts=256):
    b, s, d = x.shape
    tok = lambda w: pl.BlockSpec((1, ts, w), lambda bi, i: (bi, i, 0))
    nconv = conv_cols.shape[-1]
    halo = pl.BlockSpec((1, 8, nconv), lambda bi, i: (bi, jnp.maximum(i * (ts // 8) - 1, 0), 0))
    consts = (p["ln_w"], p["ln_b"], p["seg"], p["conv_w"], p["mla_w_o"], p["rwkv_w_o"], p["conv_w_o"], p["w_out"])
    return pl.pallas_call(
        functools.partial(_merge_body, ts=ts),
        grid=(b, s // ts),
        in_specs=[tok(d), tok(GATE_COLS), tok(att.shape[-1]), tok(RWKV_WIDTH), tok(RWKV_WIDTH), tok(RWKV_WIDTH),
                  tok(nconv), halo] + [_const_spec(a.shape) for a in consts],
        out_specs=tok(d),
        out_shape=jax.ShapeDtypeStruct((b, s, d), F32),
        compiler_params=_cparams(("parallel", "parallel")),
        name="branch_merge",
    )(x, gates, att, y, g, bonus, conv_cols, conv_cols, *consts)


def _mlp_body(x_ref, g_ref, wu_ref, wd_ref, o_ref):
    x = x_ref[...]
    ms = jnp.mean(x * x, axis=-1, keepdims=True)
    h = (x * lax.rsqrt(ms + NORM_EPS) * g_ref[...]).astype(BF16)
    up = jnp.dot(h, wu_ref[...], preferred_element_type=F32)
    act = jnp.square(jnp.maximum(up, 0.0)).astype(BF16)
    o_ref[...] = x + jnp.dot(act, wd_ref[...], preferred_element_type=F32)


def _mlp(x2d, gain, w_up, w_down, tm=256):
    t, d = x2d.shape
    row = pl.BlockSpec((tm, d), lambda i: (i, 0))
    return pl.pallas_call(
        _mlp_body,
        grid=(t // tm,),
        in_specs=[row, _const_spec((1, d)), _const_spec(w_up.shape), _const_spec(w_down.shape)],
        out_specs=row,
        out_shape=jax.ShapeDtypeStruct((t, d), F32),
        compiler_params=_cparams(("parallel",)),
        name="mlp",
    )(x2d, gain, w_up, w_down)


def _rope_partner_cols(w):
    half = QK_ROPE_DIM // 2
    return jnp.concatenate([-w[..., half:], w[..., :half]], axis=-1)


def _pad_lanes(w, lo, total=HEAD_PAD):
    n = w.shape[-1]
    pad = [(0, 0)] * (w.ndim - 1) + [(lo, total - lo - n)]
    return jnp.pad(w, pad)


def _layer_params(l, attn_norm, w_in, mla_q_a_norm, mla_wq_b, mla_kv_a_norm, mla_wkv_b, mla_q_norm, mla_k_norm,
                  mla_w_o, rwkv_mu, rwkv_w0, rwkv_w2, rwkv_a0, rwkv_a2, rwkv_g2, rwkv_k_k, rwkv_k_a, rwkv_r_k,
                  rwkv_ln_w, rwkv_ln_b, rwkv_w_o, rwkv_v1, rwkv_v_mu, rwkv_v0, rwkv_v2, conv_w, conv_w_o, w_out,
                  mlp_norm, w_up, w_down):
    p = {}
    row = lambda a: a.reshape(1, -1).astype(F32)
    w = w_in[l]
    o_mla = GATE_COLS
    o_rwkv = o_mla + MLA_COLS
    o_conv = o_rwkv + RWKV_COLS
    p["attn_norm"] = row(attn_norm[l])
    p["w_gate"] = w[:, :GATE_COLS].astype(BF16)
    w_kpe = w[:, o_mla + Q_LORA_RANK + KV_LORA_RANK:o_rwkv]
    p["w_mla"] = jnp.concatenate(
        [w[:, o_mla:o_mla + Q_LORA_RANK + KV_LORA_RANK], _pad_lanes(w_kpe, QK_NOPE_DIM),
         _pad_lanes(_rope_partner_cols(w_kpe), QK_NOPE_DIM)], axis=1).astype(BF16)
    w_rwkv = w[:, o_rwkv:o_conv]
    mu = rwkv_mu[l]
    if l > 0:
        w_rwkv = jnp.concatenate([w_rwkv, _pad_lanes(rwkv_v1[l - 1], 0)], axis=1)
        mu = jnp.concatenate([mu, _pad_lanes(rwkv_v_mu[l - 1], 0)])
        p["v0"] = row(rwkv_v0[l - 1])
        p["v2"] = jnp.pad(rwkv_v2[l - 1], ((0, LANES - MV_LORA), (0, 0)))
    p["w_rwkv"] = w_rwkv.astype(BF16)
    p["mu"] = row(mu)
    p["w_conv"] = w[:, o_conv:].astype(BF16)

    scale = QK_HEAD_DIM ** -0.5 * float(np.log2(np.e))
    wq = mla_wq_b[l].reshape(Q_LORA_RANK, MLA_HEADS, QK_HEAD_DIM)
    p["wq"] = _pad_lanes(wq, 0).reshape(Q_LORA_RANK, -1).astype(BF16)
    p["wqr"] = _pad_lanes(_rope_partner_cols(wq[..., QK_NOPE_DIM:]), QK_NOPE_DIM).reshape(Q_LORA_RANK, -1).astype(BF16)
    wkv = mla_wkv_b[l].reshape(KV_LORA_RANK, MLA_HEADS, QK_NOPE_DIM + V_HEAD_DIM)
    p["wk"] = _pad_lanes(wkv[..., :QK_NOPE_DIM], 0).reshape(KV_LORA_RANK, -1).astype(BF16)
    p["wv"] = wkv[..., QK_NOPE_DIM:].reshape(KV_LORA_RANK, -1).astype(BF16)
    p["qan"] = row(mla_q_a_norm[l])
    p["kvan"] = row(mla_kv_a_norm[l])
    swap = lambda g: jnp.concatenate([g[QK_ROPE_DIM // 2:], g[:QK_ROPE_DIM // 2]])
    gq, gk = mla_q_norm[l] * scale, mla_k_norm[l]
    p["gq"] = row(_pad_lanes(gq, 0))
    p["gqr"] = row(_pad_lanes(swap(gq[QK_NOPE_DIM:]), QK_NOPE_DIM))
    p["gk"] = row(_pad_lanes(gk, 0))
    p["gkr"] = row(_pad_lanes(swap(gk[QK_NOPE_DIM:]), QK_NOPE_DIM))
    p["mla_w_o"] = mla_w_o[l].astype(BF16)

    p["w0"] = row(rwkv_w0[l])
    p["w2"] = jnp.pad(rwkv_w2[l], ((0, AAA_LORA), (0, 0)))
    p["a0"] = row(rwkv_a0[l])
    p["a2"] = jnp.pad(rwkv_a2[l], ((DECAY_LORA, 0), (0, 0)))
    p["g2"] = rwkv_g2[l]
    p["k_k"] = row(rwkv_k_k[l])
    p["k_a"] = row(rwkv_k_a[l])
    p["r_k"] = row(rwkv_r_k[l])
    p["ln_w"] = row(rwkv_ln_w[l])
    p["ln_b"] = row(rwkv_ln_b[l])
    head_of = np.arange(RWKV_WIDTH) // RWKV_HEAD_DIM
    p["seg"] = jnp.asarray(head_of[:, None] == head_of[None, :], F32)
    p["rwkv_w_o"] = rwkv_w_o[l].astype(BF16)
    p["conv_w"] = conv_w[l].astype(F32)
    p["conv_w_o"] = conv_w_o[l].astype(BF16)
    p["w_out"] = w_out[l].astype(BF16)
    p["mlp_norm"] = row(mlp_norm[l])
    p["w_up"] = w_up[l].astype(BF16)
    p["w_down"] = w_down[l].astype(BF16)
    return p


def _rope_tables(positions):
    half = QK_ROPE_DIM // 2
    freqs = ROPE_THETA ** (-(jnp.arange(half, dtype=F32) * 2.0 / QK_ROPE_DIM))
    ang = positions.astype(F32)[..., None] * freqs
    cos, sin = jnp.cos(ang), jnp.sin(ang)
    ones = jnp.ones(positions.shape + (QK_NOPE_DIM,), F32)
    tail = jnp.ones(positions.shape + (HEAD_PAD - QK_HEAD_DIM,), F32)
    cosf = jnp.concatenate([ones, cos, cos, tail], axis=-1)
    sinf = jnp.concatenate([0 * ones, sin, sin, 0 * tail], axis=-1)
    return cosf, sinf


def kernel(x, positions, attn_norm, w_in, mla_q_a_norm, mla_wq_b, mla_kv_a_norm, mla_wkv_b, mla_q_norm, mla_k_norm, mla_w_o, rwkv_mu, rwkv_w0, rwkv_w2, rwkv_a0, rwkv_a2, rwkv_g2, rwkv_k_k, rwkv_k_a, rwkv_r_k, rwkv_ln_w, rwkv_ln_b, rwkv_w_o, rwkv_v1, rwkv_v_mu, rwkv_v0, rwkv_v2, conv_w, conv_w_o, w_out, mlp_norm, w_up, w_down):
    weights = (attn_norm, w_in, mla_q_a_norm, mla_wq_b, mla_kv_a_norm, mla_wkv_b, mla_q_norm, mla_k_norm, mla_w_o,
               rwkv_mu, rwkv_w0, rwkv_w2, rwkv_a0, rwkv_a2, rwkv_g2, rwkv_k_k, rwkv_k_a, rwkv_r_k, rwkv_ln_w,
               rwkv_ln_b, rwkv_w_o, rwkv_v1, rwkv_v_mu, rwkv_v0, rwkv_v2, conv_w, conv_w_o, w_out, mlp_norm,
               w_up, w_down)
    b, s, d = x.shape
    cosf, sinf = _rope_tables(positions)
    v_first = None
    for l in range(DEPTH):
        p = _layer_params(l, *weights)
        gates, mla_cols, rwkv_cols, conv_cols = _in_proj(
            x.reshape(b * s, d), p["attn_norm"], p["w_gate"], p["w_mla"], p["w_rwkv"], p["w_conv"])
        q, k, v = _mla_prep(mla_cols.reshape(b, s, -1), cosf, sinf, p)
        att = _attention(q, k, v)
        prep = _rwkv_prep(rwkv_cols.reshape(b, s, -1), v_first, p)
        scan_ops = prep[:SCAN_KEY_GROUPS]
        scan_val, g_, bonus = prep[SCAN_KEY_GROUPS:SCAN_KEY_GROUPS + 3]
        if l == 0:
            v_first = prep[SCAN_KEY_GROUPS + 3]
        y = _wkv_scan(scan_ops, scan_val, b)
        x = _merge(x, gates.reshape(b, s, -1), att, y, g_, bonus, conv_cols.reshape(b, s, -1), p)
        x = _mlp(x.reshape(b * s, d), p["mlp_norm"], p["w_up"], p["w_down"]).reshape(b, s, d)
    return x
```

```python
import functools

import jax
import jax.numpy as jnp
import numpy as np
from jax import lax
from jax.experimental import pallas as pl
from jax.experimental.pallas import tpu as pltpu

D_MODEL = 1024
DEPTH = 2
MLA_HEADS = 8
QK_NOPE_DIM = 64
QK_ROPE_DIM = 32
QK_HEAD_DIM = QK_NOPE_DIM + QK_ROPE_DIM
V_HEAD_DIM = 64
Q_LORA_RANK = 384
KV_LORA_RANK = 256
ROPE_THETA = 10000.0
RWKV_HEAD_DIM = 64
RWKV_HEADS = 4
RWKV_WIDTH = RWKV_HEADS * RWKV_HEAD_DIM
DECAY_LORA = 64
AAA_LORA = 64
GATE_LORA = 128
MV_LORA = 32
GN_EPS = 64e-5
CONV_WIDTH = 256
CONV_K = 3
D_FF = 4 * D_MODEL
N_BRANCH = 3
NORM_EPS = 1e-6
GATE_COLS = N_BRANCH * D_MODEL
MLA_COLS = Q_LORA_RANK + KV_LORA_RANK + QK_ROPE_DIM
RWKV_COLS = 3 * RWKV_WIDTH + DECAY_LORA + AAA_LORA + GATE_LORA

LANES = 128
HEAD_PAD = LANES
VT_ROWS = V_HEAD_DIM + 16
MLA_OUT_COLS = Q_LORA_RANK + KV_LORA_RANK + 2 * LANES
VMEM_LIMIT = 56 * 1024 * 1024

F32 = jnp.float32
BF16 = jnp.bfloat16


def _cparams(sem):
    return pltpu.CompilerParams(dimension_semantics=sem, vmem_limit_bytes=VMEM_LIMIT)


def _const_spec(shape):
    nd = len(shape)
    return pl.BlockSpec(shape, lambda *_: (0,) * nd, pipeline_mode=pl.Buffered(1))


def _bdot(a, b):
    return jnp.dot(a.astype(BF16), b.astype(BF16), preferred_element_type=F32)


def _seg_sum(x, e):
    return jnp.dot(x, e, preferred_element_type=F32, precision=lax.Precision.HIGHEST)


def _in_proj_body(x_ref, g_ref, wg_ref, wm_ref, wr_ref, wc_ref, gate_ref, mla_ref, rwkv_ref, conv_ref):
    x = x_ref[...]
    ms = jnp.mean(x * x, axis=-1, keepdims=True)
    h = (x * lax.rsqrt(ms + NORM_EPS) * g_ref[...]).astype(BF16)
    gate_ref[...] = jax.nn.sigmoid(jnp.dot(h, wg_ref[...], preferred_element_type=F32)).astype(gate_ref.dtype)
    mla_ref[...] = jnp.dot(h, wm_ref[...], preferred_element_type=F32)
    rwkv_ref[...] = jnp.dot(h, wr_ref[...], preferred_element_type=F32)
    conv_ref[...] = jnp.dot(h, wc_ref[...], preferred_element_type=F32)


def _in_proj(x2d, gain, wg, wm, wr, wc, tm=256):
    t, d = x2d.shape
    row = lambda n: pl.BlockSpec((tm, n), lambda i: (i, 0))
    widths = (wg.shape[1], wm.shape[1], wr.shape[1], wc.shape[1])
    return pl.pallas_call(
        _in_proj_body,
        grid=(t // tm,),
        in_specs=[row(d), _const_spec((1, d))] + [_const_spec(w.shape) for w in (wg, wm, wr, wc)],
        out_specs=[row(n) for n in widths],
        out_shape=[jax.ShapeDtypeStruct((t, widths[0]), BF16)]
        + [jax.ShapeDtypeStruct((t, n), F32) for n in widths[1:]],
        compiler_params=_cparams(("parallel",)),
        name="in_proj",
    )(x2d, gain, wg, wm, wr, wc)


def _mla_prep_body(c_ref, cos_ref, sin_ref, qan_ref, kvan_ref, wq_ref, wqr_ref, wk_ref, wv_ref,
                   gq_ref, gqr_ref, gk_ref, gkr_ref, q_ref, k_ref, v_ref):
    c = c_ref[0]
    cq = c[:, :Q_LORA_RANK]
    ckv = c[:, Q_LORA_RANK:Q_LORA_RANK + KV_LORA_RANK]
    kpe = c[:, Q_LORA_RANK + KV_LORA_RANK:Q_LORA_RANK + KV_LORA_RANK + LANES]
    kper = c[:, Q_LORA_RANK + KV_LORA_RANK + LANES:]

    def rms(z, g):
        return (z * lax.rsqrt(jnp.mean(z * z, axis=-1, keepdims=True) + NORM_EPS) * g).astype(BF16)

    cqn = rms(cq, qan_ref[...])
    ckvn = rms(ckv, kvan_ref[...])
    q = jnp.dot(cqn, wq_ref[...], preferred_element_type=F32)
    qr = jnp.dot(cqn, wqr_ref[...], preferred_element_type=F32)
    kn = jnp.dot(ckvn, wk_ref[...], preferred_element_type=F32)
    v = jnp.dot(ckvn, wv_ref[...], preferred_element_type=F32)
    cosf = cos_ref[0]
    sinf = sin_ref[0]
    gqc = gq_ref[...] * cosf
    gqs = gqr_ref[...] * sinf
    gkc = gk_ref[...] * cosf
    kper_s = kper * gkr_ref[...] * sinf
    inv_dim = 1.0 / QK_HEAD_DIM
    vt = v.T
    pad_row = lax.broadcasted_iota(jnp.int32, (VT_ROWS - V_HEAD_DIM, vt.shape[1]), 0)
    ones_rows = jnp.where(pad_row == 0, 1.0, 0.0)
    for h in range(MLA_HEADS):
        sl = slice(h * HEAD_PAD, (h + 1) * HEAD_PAD)
        qh = q[:, sl]
        rq = lax.rsqrt(jnp.sum(qh * qh, axis=-1, keepdims=True) * inv_dim + NORM_EPS)
        q_ref[0, h] = (rq * (qh * gqc + qr[:, sl] * gqs)).astype(q_ref.dtype)
        kh = kn[:, sl] + kpe
        rk = lax.rsqrt(jnp.sum(kh * kh, axis=-1, keepdims=True) * inv_dim + NORM_EPS)
        k_ref[0, h] = (rk * (kh * gkc + kper_s)).astype(k_ref.dtype)
        v_ref[0, h, 0] = jnp.concatenate([vt[h * V_HEAD_DIM:(h + 1) * V_HEAD_DIM, :], ones_rows],
                                         axis=0).astype(v_ref.dtype)


def _mla_prep(mla_cols, cosf, sinf, p):
    b, s, n = mla_cols.shape
    tm = ATTN_BLOCK
    tok = lambda w: pl.BlockSpec((1, tm, w), lambda bi, i: (bi, i, 0))
    head = lambda w: pl.BlockSpec((1, MLA_HEADS, tm, w), lambda bi, i: (bi, 0, i, 0))
    consts = (p["qan"], p["kvan"], p["wq"], p["wqr"], p["wk"], p["wv"], p["gq"], p["gqr"], p["gk"], p["gkr"])
    return pl.pallas_call(
        _mla_prep_body,
        grid=(b, s // tm),
        in_specs=[tok(n), tok(LANES), tok(LANES)] + [_const_spec(a.shape) for a in consts],
        out_specs=[head(HEAD_PAD), head(HEAD_PAD),
                   pl.BlockSpec((1, MLA_HEADS, 1, VT_ROWS, tm), lambda bi, i: (bi, 0, i, 0, 0))],
        out_shape=[jax.ShapeDtypeStruct((b, MLA_HEADS, s, HEAD_PAD), BF16),
                   jax.ShapeDtypeStruct((b, MLA_HEADS, s, HEAD_PAD), BF16),
                   jax.ShapeDtypeStruct((b, MLA_HEADS, s // tm, VT_ROWS, tm), BF16)],
        compiler_params=_cparams(("parallel", "parallel")),
        name="mla_prep",
    )(mla_cols, cosf, sinf, *consts)


MASK_VALUE = -1e30


ATTN_BLOCK = 256
ATTN_LOOKAHEAD = 4


def _attn_body(q_ref, k_ref, vt_ref, o_ref, m_ref, acc_ref, *, tq):
    i = pl.program_id(1)
    m_ref[...] = jnp.full(m_ref.shape, MASK_VALUE, F32)
    acc_ref[...] = jnp.zeros(acc_ref.shape, F32)
    key_idx = lax.broadcasted_iota(jnp.int32, (tq, tq), 0)
    qry_idx = lax.broadcasted_iota(jnp.int32, (tq, tq), 1)
    causal = key_idx <= qry_idx

    def block(j, masked):
        start = pl.multiple_of(j * tq, tq)

        def scores_t(h):
            kj = k_ref[0, h, pl.ds(start, tq), :]
            return lax.dot_general(kj, q_ref[0, h], (((1,), (1,)), ((), ())), preferred_element_type=F32)

        pending = [scores_t(h) for h in range(ATTN_LOOKAHEAD)]
        for h in range(MLA_HEADS):
            st = pending.pop(0)
            if h + ATTN_LOOKAHEAD < MLA_HEADS:
                pending.append(scores_t(h + ATTN_LOOKAHEAD))
            if masked:
                st = jnp.where(causal, st, MASK_VALUE)
            m_prev = m_ref[h]
            m_new = jnp.maximum(m_prev, jnp.max(st, axis=0, keepdims=True))
            alpha = jnp.exp2(m_prev - m_new)
            pt = jnp.exp2(st - m_new)
            acc_ref[h] = alpha * acc_ref[h] + jnp.dot(vt_ref[0, h, j], pt.astype(BF16), preferred_element_type=F32)
            m_ref[h] = m_new

    def loop_body(j, carry):
        block(j, False)
        return carry

    lax.fori_loop(0, i, loop_body, 0)
    block(i, True)
    for h in range(MLA_HEADS):
        out_t = acc_ref[h, 0:V_HEAD_DIM, :] / acc_ref[h, V_HEAD_DIM:V_HEAD_DIM + 1, :]
        o_ref[0, :, h * V_HEAD_DIM:(h + 1) * V_HEAD_DIM] = out_t.T.astype(o_ref.dtype)


def _attention(q, k, vt):
    b, nh, s, dp = q.shape
    tq = ATTN_BLOCK
    return pl.pallas_call(
        functools.partial(_attn_body, tq=tq),
        grid=(b, s // tq),
        in_specs=[pl.BlockSpec((1, nh, tq, dp), lambda bi, i: (bi, 0, i, 0)),
                  pl.BlockSpec((1, nh, s, dp), lambda bi, i: (bi, 0, 0, 0)),
                  pl.BlockSpec((1, nh, s // tq, VT_ROWS, tq), lambda bi, i: (bi, 0, 0, 0, 0))],
        out_specs=pl.BlockSpec((1, tq, nh * V_HEAD_DIM), lambda bi, i: (bi, i, 0)),
        out_shape=jax.ShapeDtypeStruct((b, s, nh * V_HEAD_DIM), BF16),
        scratch_shapes=[pltpu.VMEM((nh, 1, tq), F32), pltpu.VMEM((nh, VT_ROWS, tq), F32)],
        compiler_params=_cparams(("parallel", "arbitrary")),
        name="mla_attention",
    )(q, k, vt)


SCAN_VR = 16
SCAN_VQ = RWKV_HEAD_DIM // SCAN_VR
SCAN_TENSORS = 4
SCAN_TC = 64
PREP_TS = LANES
SCAN_KEY_GROUPS = 4
SCAN_KEYS_PER_GROUP = RWKV_HEAD_DIM // SCAN_KEY_GROUPS


def _rwkv_prep_body(*refs, has_vres, ts, nb):
    if has_vres:
        (x_ref, xp_ref, vf_ref, mu_ref, w0_ref, w2_ref, a0_ref, a2_ref, g2_ref, kk_ref, ka_ref, rk_ref, e_ref, tri_ref,
         v0_ref, v2_ref, *ops_o, val_o, dend_o, g_o, bonus_o, xt_ref, vt_ref) = refs
    else:
        (x_ref, xp_ref, mu_ref, w0_ref, w2_ref, a0_ref, a2_ref, g2_ref, kk_ref, ka_ref, rk_ref, e_ref, tri_ref,
         *ops_o, val_o, dend_o, g_o, bonus_o, vfirst_o, xt_ref, vt_ref) = refs
    i = pl.program_id(0)
    tensor = pl.program_id(1)
    bh = nb * RWKV_HEADS

    @pl.when(tensor == 0)
    def _():
        def per_batch(b, carry):
            x = x_ref[b]
            prev = jnp.where(i > 0, xp_ref[b][7:8, :], 0.0)
            row = lax.broadcasted_iota(jnp.int32, (ts, 1), 0)
            shifted = jnp.where(row == 0, prev, pltpu.roll(x, 1, axis=0))
            xs = x + (shifted - x) * mu_ref[...]
            wd = RWKV_WIDTH
            r = xs[:, 0:wd]
            k = xs[:, wd:2 * wd]
            v = xs[:, 2 * wd:3 * wd]
            lora_in = xs[:, 3 * wd:3 * wd + LANES]
            xg = xs[:, 3 * wd + LANES:3 * wd + 2 * LANES]
            e = e_ref[...]
            zw = w0_ref[...] + _bdot(jnp.tanh(lora_in), w2_ref[...])
            nz = -zw
            softplus = jnp.maximum(nz, 0.0) + jnp.log(1.0 + jnp.exp(-jnp.abs(nz)))
            log_decay = -jnp.exp(-softplus - 0.5)
            log_d = jnp.dot(tri_ref[...], log_decay, preferred_element_type=F32, precision=lax.Precision.HIGHEST)
            d_incl = jnp.exp(log_d)
            d_prev = jnp.exp(log_d - log_decay)
            d_inv = jnp.exp(-log_d)
            ends = [d_incl[(c + 1) * SCAN_TC - 1:(c + 1) * SCAN_TC, :] for c in range(ts // SCAN_TC)]
            dend_o[0, pl.ds(b, 1), :] = jnp.concatenate(ends, axis=1)
            a_lr = jax.nn.sigmoid(a0_ref[...] + _bdot(lora_in, a2_ref[...]))
            g_o[b] = _bdot(jax.nn.sigmoid(xg), g2_ref[...])
            if has_vres:
                xvs = xs[:, RWKV_COLS:RWKV_COLS + LANES]
                v = v + (vf_ref[b] - v) * jax.nn.sigmoid(v0_ref[...] + _bdot(xvs, v2_ref[...]))
            else:
                vfirst_o[b] = v
            kk = k * kk_ref[...]
            norm = jnp.sqrt(_seg_sum(kk * kk, e))
            kk = kk / jnp.maximum(norm, 1e-12)
            k = k * (1.0 + (a_lr - 1.0) * ka_ref[...])
            bonus_o[b] = _seg_sum(r * k * rk_ref[...], e) * v
            rows = pl.ds(pl.multiple_of(b * wd, wd), wd)
            for idx, val in enumerate((-kk * d_prev, kk * a_lr * d_inv, k * d_inv, r * d_incl)):
                xt_ref[idx, rows, :] = val.T
            vt_ref[rows, :] = v.T
            return carry

        lax.fori_loop(0, nb, per_batch, 0)
        for vr in range(SCAN_VR):
            slab = jnp.concatenate(
                [vt_ref[pl.ds(vq * SCAN_VR + vr, bh, stride=RWKV_HEAD_DIM), :] for vq in range(SCAN_VQ)], axis=0)
            val_o[pl.ds(vr, ts, stride=SCAN_VR), :] = slab.T

    for group, out in enumerate(ops_o):
        def emit(key, carry, group=group, out=out):
            rows = xt_ref[tensor, pl.ds(group * SCAN_KEYS_PER_GROUP + key, bh, stride=RWKV_HEAD_DIM), :]
            by_t = jnp.concatenate([rows] * SCAN_VQ, axis=0).T
            for chunk in range(SCAN_SUB):
                out[chunk, 0, key] = by_t[chunk * SCAN_TC:(chunk + 1) * SCAN_TC, :]
            return carry

        lax.fori_loop(0, SCAN_KEYS_PER_GROUP, emit, 0, unroll=16)


def _rwkv_prep(rwkv_cols, v_first, p):
    b, s, n = rwkv_cols.shape
    ts = PREP_TS
    has_vres = v_first is not None
    tok = lambda w: pl.BlockSpec((b, ts, w), lambda i, t: (0, i, 0))
    halo = pl.BlockSpec((b, 8, n), lambda i, t: (0, jnp.maximum(i * (ts // 8) - 1, 0), 0))
    step = np.arange(ts)
    same_chunk = (step[:, None] // SCAN_TC) == (step[None, :] // SCAN_TC)
    chunk_tri = jnp.asarray(same_chunk & (step[None, :] <= step[:, None]), F32)
    consts = [p["mu"], p["w0"], p["w2"], p["a0"], p["a2"], p["g2"], p["k_k"], p["k_a"], p["r_k"], p["seg"], chunk_tri]
    args = [rwkv_cols, rwkv_cols]
    in_specs = [tok(n), halo]
    if has_vres:
        args.append(v_first)
        in_specs.append(tok(RWKV_WIDTH))
        consts += [p["v0"], p["v2"]]
    in_specs += [_const_spec(a.shape) for a in consts]
    n_tok_out = 2 if has_vres else 3
    return pl.pallas_call(
        functools.partial(_rwkv_prep_body, has_vres=has_vres, ts=ts, nb=b),
        grid=(s // ts, SCAN_TENSORS),
        in_specs=in_specs,
        out_specs=[pl.BlockSpec((SCAN_SUB, 1, SCAN_KEYS_PER_GROUP, SCAN_TC, LANES), lambda i, t: (i, t, 0, 0, 0))]
        * SCAN_KEY_GROUPS
        + [pl.BlockSpec((ts * SCAN_VR, LANES), lambda i, t: (i, 0)),
           pl.BlockSpec((1, b, SCAN_SUB * RWKV_WIDTH), lambda i, t: (i, 0, 0))] + [tok(RWKV_WIDTH)] * n_tok_out,
        out_shape=[jax.ShapeDtypeStruct((s // SCAN_TC, SCAN_TENSORS, SCAN_KEYS_PER_GROUP, SCAN_TC, LANES), F32)]
        * SCAN_KEY_GROUPS
        + [jax.ShapeDtypeStruct((s * SCAN_VR, LANES), F32),
           jax.ShapeDtypeStruct((s // ts, b, SCAN_SUB * RWKV_WIDTH), F32)]
        + [jax.ShapeDtypeStruct((b, s, RWKV_WIDTH), F32)] * n_tok_out,
        scratch_shapes=[pltpu.VMEM((SCAN_TENSORS, b * RWKV_WIDTH, ts), F32), pltpu.VMEM((b * RWKV_WIDTH, ts), F32)],
        compiler_params=_cparams(("arbitrary", "arbitrary")),
        name="rwkv_prep",
    )(*args, *consts)


N_ACC = 4
SCAN_SUB = PREP_TS // SCAN_TC


def _scan_body(*refs, tc, nb):
    n_ops = SCAN_TENSORS * SCAN_KEY_GROUPS
    ops_refs = refs[:n_ops]
    v_ref, dend_ref, y_ref, s_ref, ych_ref, yt_ref = refs[n_ops:]
    c = pl.program_id(1)

    def operand(idx, kk, t):
        ref = ops_refs[idx * SCAN_KEY_GROUPS + kk // SCAN_KEYS_PER_GROUP]
        return ref[0, 0, kk % SCAN_KEYS_PER_GROUP, pl.ds(t, 1), :]

    @pl.when((pl.program_id(0) == 0) & (c == 0))
    def _():
        s_ref[...] = jnp.zeros_like(s_ref)

    def tree(acc):
        return (acc[0] + acc[1]) + (acc[2] + acc[3])

    def accumulate(acc, kk, term):
        acc[kk % N_ACC] = term if acc[kk % N_ACC] is None else acc[kk % N_ACC] + term

    acc = [None] * N_ACC
    for kk in range(RWKV_HEAD_DIM):
        s_start = s_ref[kk] * dend_ref[0, pl.ds(kk, 1), :]
        s_ref[kk] = s_start
        accumulate(acc, kk, s_start * operand(0, kk, 0))

    def step(t, u):
        op = lambda idx, kk: operand(idx, kk, t)
        t_next = jnp.minimum(t + 1, tc - 1)
        v = v_ref[pl.ds(pl.multiple_of(t * SCAN_VR, SCAN_VR), SCAN_VR), :]
        yacc = [None] * N_ACC
        uacc = [None] * N_ACC
        for kk in range(RWKV_HEAD_DIM):
            s_new = s_ref[kk] + (u * op(1, kk) + v * op(2, kk))
            s_ref[kk] = s_new
            accumulate(yacc, kk, s_new * op(3, kk))
            accumulate(uacc, kk, s_new * operand(0, kk, t_next))
        row0 = pl.multiple_of((c * tc + t) * SCAN_VR, SCAN_VR)
        ych_ref[pl.ds(row0, SCAN_VR), :] = tree(yacc)
        return tree(uacc)

    lax.fori_loop(0, tc, step, tree(acc), unroll=8)

    @pl.when(c == SCAN_SUB - 1)
    def _():
        bh = nb * RWKV_HEADS
        for vr in range(SCAN_VR):
            lanes_by_t = ych_ref[pl.ds(vr, PREP_TS, stride=SCAN_VR), :].T
            for vq in range(SCAN_VQ):
                yt_ref[pl.ds(vq * SCAN_VR + vr, bh, stride=RWKV_HEAD_DIM), :] = lanes_by_t[vq * bh:(vq + 1) * bh, :]
        for b in range(nb):
            y_ref[b] = yt_ref[b * RWKV_WIDTH:(b + 1) * RWKV_WIDTH, :].T


def _wkv_scan(ops_groups, val, dend, nb):
    tc = SCAN_TC
    s = ops_groups[0].shape[0] * tc
    ops_args, ops_specs = [], []
    for tensor in range(SCAN_TENSORS):
        for group in ops_groups:
            ops_args.append(group)
            ops_specs.append(pl.BlockSpec((1, 1, SCAN_KEYS_PER_GROUP, tc, LANES),
                                          lambda i, c, tensor=tensor: (i * SCAN_SUB + c, tensor, 0, 0, 0)))
    return pl.pallas_call(
        functools.partial(_scan_body, tc=tc, nb=nb),
        grid=(s // PREP_TS, SCAN_SUB),
        in_specs=ops_specs + [
            pl.BlockSpec((tc * SCAN_VR, LANES), lambda i, c: (i * SCAN_SUB + c, 0)),
            pl.BlockSpec((1, RWKV_HEAD_DIM, LANES), lambda i, c: (jnp.maximum(i * SCAN_SUB + c - 1, 0), 0, 0))],
        out_specs=pl.BlockSpec((nb, PREP_TS, RWKV_WIDTH), lambda i, c: (0, i, 0)),
        out_shape=jax.ShapeDtypeStruct((nb, s, RWKV_WIDTH), F32),
        scratch_shapes=[pltpu.VMEM((RWKV_HEAD_DIM, SCAN_VR, LANES), F32),
                        pltpu.VMEM((PREP_TS * SCAN_VR, LANES), F32),
                        pltpu.VMEM((nb * RWKV_WIDTH, PREP_TS), F32)],
        compiler_params=_cparams(("arbitrary", "arbitrary")),
        name="wkv_scan",
    )(*ops_args, val, dend)


def _merge_body(x_ref, gate_ref, att_ref, y_ref, g_ref, bonus_ref, conv_ref, convp_ref,
                lnw_ref, lnb_ref, e_ref, cw_ref, wa_ref, wb_ref, wc_ref, wo_ref, out_ref, *, ts):
    i = pl.program_id(1)
    e = e_ref[...]
    y = y_ref[0]
    inv_n = 1.0 / RWKV_HEAD_DIM
    mean = _seg_sum(y, e) * inv_n
    d = y - mean
    var = _seg_sum(d * d, e) * inv_n
    yn = d * lax.rsqrt(var + GN_EPS) * lnw_ref[...] + lnb_ref[...] + bonus_ref[0]
    ob = _bdot(yn * g_ref[0], wb_ref[...])

    cw = CONV_WIDTH
    c = conv_ref[0]
    u = c[:, cw:2 * cw] * c[:, 2 * cw:3 * cw]
    cp = convp_ref[0]
    up = jnp.where(i > 0, cp[:, cw:2 * cw] * cp[:, 2 * cw:3 * cw], 0.0)
    p6 = up[6:7, :]
    p7 = up[7:8, :]
    row = lax.broadcasted_iota(jnp.int32, (ts, 1), 0)
    u1 = jnp.where(row == 0, p7, pltpu.roll(u, 1, axis=0))
    u2 = jnp.where(row == 0, p6, jnp.where(row == 1, p7, pltpu.roll(u, 2, axis=0)))
    taps = cw_ref[...]
    yc = taps[0:1, :] * u2 + taps[1:2, :] * u1 + taps[2:3, :] * u
    oc = _bdot(c[:, 0:cw] * yc, wc_ref[...])

    oa = jnp.dot(att_ref[0], wa_ref[...], preferred_element_type=F32)
    gates = gate_ref[0].astype(F32)
    dm = D_MODEL
    merged = gates[:, 0:dm] * oa + gates[:, dm:2 * dm] * ob + gates[:, 2 * dm:3 * dm] * oc
    out_ref[0] = x_ref[0] + _bdot(merged, wo_ref[...])


def _merge(x, gates, att, y, g, bonus, conv_cols, p, ts=256):
    b, s, d = x.shape
    tok = lambda w: pl.BlockSpec((1, ts, w), lambda bi, i: (bi, i, 0))
    nconv = conv_cols.shape[-1]
    halo = pl.BlockSpec((1, 8, nconv), lambda bi, i: (bi, jnp.maximum(i * (ts // 8) - 1, 0), 0))
    consts = (p["ln_w"], p["ln_b"], p["seg"], p["conv_w"], p["mla_w_o"], p["rwkv_w_o"], p["conv_w_o"], p["w_out"])
    return pl.pallas_call(
        functools.partial(_merge_body, ts=ts),
        grid=(b, s // ts),
        in_specs=[tok(d), tok(GATE_COLS), tok(att.shape[-1]), tok(RWKV_WIDTH), tok(RWKV_WIDTH), tok(RWKV_WIDTH),
                  tok(nconv), halo] + [_const_spec(a.shape) for a in consts],
        out_specs=tok(d),
        out_shape=jax.ShapeDtypeStruct((b, s, d), F32),
        compiler_params=_cparams(("parallel", "parallel")),
        name="branch_merge",
    )(x, gates, att, y, g, bonus, conv_cols, conv_cols, *consts)


def _mlp_body(x_ref, g_ref, wu_ref, wd_ref, o_ref):
    x = x_ref[...]
    ms = jnp.mean(x * x, axis=-1, keepdims=True)
    h = (x * lax.rsqrt(ms + NORM_EPS) * g_ref[...]).astype(BF16)
    up = jnp.dot(h, wu_ref[...], preferred_element_type=F32)
    act = jnp.square(jnp.maximum(up, 0.0)).astype(BF16)
    o_ref[...] = x + jnp.dot(act, wd_ref[...], preferred_element_type=F32)


def _mlp(x2d, gain, w_up, w_down, tm=256):
    t, d = x2d.shape
    row = pl.BlockSpec((tm, d), lambda i: (i, 0))
    return pl.pallas_call(
        _mlp_body,
        grid=(t // tm,),
        in_specs=[row, _const_spec((1, d)), _const_spec(w_up.shape), _const_spec(w_down.shape)],
        out_specs=row,
        out_shape=jax.ShapeDtypeStruct((t, d), F32),
        compiler_params=_cparams(("parallel",)),
        name="mlp",
    )(x2d, gain, w_up, w_down)


def _rope_partner_cols(w):
    half = QK_ROPE_DIM // 2
    return jnp.concatenate([-w[..., half:], w[..., :half]], axis=-1)


def _pad_lanes(w, lo, total=HEAD_PAD):
    n = w.shape[-1]
    pad = [(0, 0)] * (w.ndim - 1) + [(lo, total - lo - n)]
    return jnp.pad(w, pad)


def _layer_params(l, attn_norm, w_in, mla_q_a_norm, mla_wq_b, mla_kv_a_norm, mla_wkv_b, mla_q_norm, mla_k_norm,
                  mla_w_o, rwkv_mu, rwkv_w0, rwkv_w2, rwkv_a0, rwkv_a2, rwkv_g2, rwkv_k_k, rwkv_k_a, rwkv_r_k,
                  rwkv_ln_w, rwkv_ln_b, rwkv_w_o, rwkv_v1, rwkv_v_mu, rwkv_v0, rwkv_v2, conv_w, conv_w_o, w_out,
                  mlp_norm, w_up, w_down):
    p = {}
    row = lambda a: a.reshape(1, -1).astype(F32)
    w = w_in[l]
    o_mla = GATE_COLS
    o_rwkv = o_mla + MLA_COLS
    o_conv = o_rwkv + RWKV_COLS
    p["attn_norm"] = row(attn_norm[l])
    p["w_gate"] = w[:, :GATE_COLS].astype(BF16)
    w_kpe = w[:, o_mla + Q_LORA_RANK + KV_LORA_RANK:o_rwkv]
    p["w_mla"] = jnp.concatenate(
        [w[:, o_mla:o_mla + Q_LORA_RANK + KV_LORA_RANK], _pad_lanes(w_kpe, QK_NOPE_DIM),
         _pad_lanes(_rope_partner_cols(w_kpe), QK_NOPE_DIM)], axis=1).astype(BF16)
    w_rwkv = w[:, o_rwkv:o_conv]
    mu = rwkv_mu[l]
    if l > 0:
        w_rwkv = jnp.concatenate([w_rwkv, _pad_lanes(rwkv_v1[l - 1], 0)], axis=1)
        mu = jnp.concatenate([mu, _pad_lanes(rwkv_v_mu[l - 1], 0)])
        p["v0"] = row(rwkv_v0[l - 1])
        p["v2"] = jnp.pad(rwkv_v2[l - 1], ((0, LANES - MV_LORA), (0, 0)))
    p["w_rwkv"] = w_rwkv.astype(BF16)
    p["mu"] = row(mu)
    p["w_conv"] = w[:, o_conv:].astype(BF16)

    scale = QK_HEAD_DIM ** -0.5 * float(np.log2(np.e))
    wq = mla_wq_b[l].reshape(Q_LORA_RANK, MLA_HEADS, QK_HEAD_DIM)
    p["wq"] = _pad_lanes(wq, 0).reshape(Q_LORA_RANK, -1).astype(BF16)
    p["wqr"] = _pad_lanes(_rope_partner_cols(wq[..., QK_NOPE_DIM:]), QK_NOPE_DIM).reshape(Q_LORA_RANK, -1).astype(BF16)
    wkv = mla_wkv_b[l].reshape(KV_LORA_RANK, MLA_HEADS, QK_NOPE_DIM + V_HEAD_DIM)
    p["wk"] = _pad_lanes(wkv[..., :QK_NOPE_DIM], 0).reshape(KV_LORA_RANK, -1).astype(BF16)
    p["wv"] = wkv[..., QK_NOPE_DIM:].reshape(KV_LORA_RANK, -1).astype(BF16)
    p["qan"] = row(mla_q_a_norm[l])
    p["kvan"] = row(mla_kv_a_norm[l])
    swap = lambda g: jnp.concatenate([g[QK_ROPE_DIM // 2:], g[:QK_ROPE_DIM // 2]])
    gq, gk = mla_q_norm[l] * scale, mla_k_norm[l]
    p["gq"] = row(_pad_lanes(gq, 0))
    p["gqr"] = row(_pad_lanes(swap(gq[QK_NOPE_DIM:]), QK_NOPE_DIM))
    p["gk"] = row(_pad_lanes(gk, 0))
    p["gkr"] = row(_pad_lanes(swap(gk[QK_NOPE_DIM:]), QK_NOPE_DIM))
    p["mla_w_o"] = mla_w_o[l].astype(BF16)

    p["w0"] = row(rwkv_w0[l])
    p["w2"] = jnp.pad(rwkv_w2[l], ((0, AAA_LORA), (0, 0)))
    p["a0"] = row(rwkv_a0[l])
    p["a2"] = jnp.pad(rwkv_a2[l], ((DECAY_LORA, 0), (0, 0)))
    p["g2"] = rwkv_g2[l]
    p["k_k"] = row(rwkv_k_k[l])
    p["k_a"] = row(rwkv_k_a[l])
    p["r_k"] = row(rwkv_r_k[l])
    p["ln_w"] = row(rwkv_ln_w[l])
    p["ln_b"] = row(rwkv_ln_b[l])
    head_of = np.arange(RWKV_WIDTH) // RWKV_HEAD_DIM
    p["seg"] = jnp.asarray(head_of[:, None] == head_of[None, :], F32)
    p["rwkv_w_o"] = rwkv_w_o[l].astype(BF16)
    p["conv_w"] = conv_w[l].astype(F32)
    p["conv_w_o"] = conv_w_o[l].astype(BF16)
    p["w_out"] = w_out[l].astype(BF16)
    p["mlp_norm"] = row(mlp_norm[l])
    p["w_up"] = w_up[l].astype(BF16)
    p["w_down"] = w_down[l].astype(BF16)
    return p


def _rope_tables(positions):
    half = QK_ROPE_DIM // 2
    freqs = ROPE_THETA ** (-(jnp.arange(half, dtype=F32) * 2.0 / QK_ROPE_DIM))
    ang = positions.astype(F32)[..., None] * freqs
    cos, sin = jnp.cos(ang), jnp.sin(ang)
    ones = jnp.ones(positions.shape + (QK_NOPE_DIM,), F32)
    tail = jnp.ones(positions.shape + (HEAD_PAD - QK_HEAD_DIM,), F32)
    cosf = jnp.concatenate([ones, cos, cos, tail], axis=-1)
    sinf = jnp.concatenate([0 * ones, sin, sin, 0 * tail], axis=-1)
    return cosf, sinf


def kernel(x, positions, attn_norm, w_in, mla_q_a_norm, mla_wq_b, mla_kv_a_norm, mla_wkv_b, mla_q_norm, mla_k_norm, mla_w_o, rwkv_mu, rwkv_w0, rwkv_w2, rwkv_a0, rwkv_a2, rwkv_g2, rwkv_k_k, rwkv_k_a, rwkv_r_k, rwkv_ln_w, rwkv_ln_b, rwkv_w_o, rwkv_v1, rwkv_v_mu, rwkv_v0, rwkv_v2, conv_w, conv_w_o, w_out, mlp_norm, w_up, w_down):
    weights = (attn_norm, w_in, mla_q_a_norm, mla_wq_b, mla_kv_a_norm, mla_wkv_b, mla_q_norm, mla_k_norm, mla_w_o,
               rwkv_mu, rwkv_w0, rwkv_w2, rwkv_a0, rwkv_a2, rwkv_g2, rwkv_k_k, rwkv_k_a, rwkv_r_k, rwkv_ln_w,
               rwkv_ln_b, rwkv_w_o, rwkv_v1, rwkv_v_mu, rwkv_v0, rwkv_v2, conv_w, conv_w_o, w_out, mlp_norm,
               w_up, w_down)
    b, s, d = x.shape
    cosf, sinf = _rope_tables(positions)
    v_first = None
    for l in range(DEPTH):
        p = _layer_params(l, *weights)
        gates, mla_cols, rwkv_cols, conv_cols = _in_proj(
            x.reshape(b * s, d), p["attn_norm"], p["w_gate"], p["w_mla"], p["w_rwkv"], p["w_conv"])
        q, k, v = _mla_prep(mla_cols.reshape(b, s, -1), cosf, sinf, p)
        att = _attention(q, k, v)
        prep = _rwkv_prep(rwkv_cols.reshape(b, s, -1), v_first, p)
        scan_ops = prep[:SCAN_KEY_GROUPS]
        scan_val, chunk_decay, g_, bonus = prep[SCAN_KEY_GROUPS:SCAN_KEY_GROUPS + 4]
        if l == 0:
            v_first = prep[SCAN_KEY_GROUPS + 4]
        dend = chunk_decay.reshape(-1, b, SCAN_SUB, RWKV_HEADS, RWKV_HEAD_DIM).transpose(0, 2, 4, 1, 3)
        dend = jnp.tile(dend.reshape(-1, RWKV_HEAD_DIM, b * RWKV_HEADS), (1, 1, SCAN_VQ))
        y = _wkv_scan(scan_ops, scan_val, dend, b)
        x = _merge(x, gates.reshape(b, s, -1), att, y, g_, bonus, conv_cols.reshape(b, s, -1), p)
        x = _mlp(x.reshape(b * s, d), p["mlp_norm"], p["w_up"], p["w_down"]).reshape(b, s, d)
    return x
```

```python
import functools

import jax
import jax.numpy as jnp
import numpy as np
from jax import lax
from jax.experimental import pallas as pl
from jax.experimental.pallas import tpu as pltpu

D_MODEL = 1024
DEPTH = 2
MLA_HEADS = 8
QK_NOPE_DIM = 64
QK_ROPE_DIM = 32
QK_HEAD_DIM = QK_NOPE_DIM + QK_ROPE_DIM
V_HEAD_DIM = 64
Q_LORA_RANK = 384
KV_LORA_RANK = 256
ROPE_THETA = 10000.0
RWKV_HEAD_DIM = 64
RWKV_HEADS = 4
RWKV_WIDTH = RWKV_HEADS * RWKV_HEAD_DIM
DECAY_LORA = 64
AAA_LORA = 64
GATE_LORA = 128
MV_LORA = 32
GN_EPS = 64e-5
CONV_WIDTH = 256
CONV_K = 3
D_FF = 4 * D_MODEL
N_BRANCH = 3
NORM_EPS = 1e-6
GATE_COLS = N_BRANCH * D_MODEL
MLA_COLS = Q_LORA_RANK + KV_LORA_RANK + QK_ROPE_DIM
RWKV_COLS = 3 * RWKV_WIDTH + DECAY_LORA + AAA_LORA + GATE_LORA

LANES = 128
HEAD_PAD = LANES
VT_ROWS = V_HEAD_DIM + 16
MLA_OUT_COLS = Q_LORA_RANK + KV_LORA_RANK + 2 * LANES
VMEM_LIMIT = 56 * 1024 * 1024

F32 = jnp.float32
BF16 = jnp.bfloat16


def _cparams(sem):
    return pltpu.CompilerParams(dimension_semantics=sem, vmem_limit_bytes=VMEM_LIMIT)


def _const_spec(shape):
    nd = len(shape)
    return pl.BlockSpec(shape, lambda *_: (0,) * nd, pipeline_mode=pl.Buffered(1))


def _bdot(a, b):
    return jnp.dot(a.astype(BF16), b.astype(BF16), preferred_element_type=F32)


def _seg_sum(x, e):
    return jnp.dot(x, e, preferred_element_type=F32, precision=lax.Precision.HIGHEST)


def _in_proj_body(x_ref, g_ref, wg_ref, wm_ref, wr_ref, wc_ref, gate_ref, mla_ref, rwkv_ref, conv_ref):
    x = x_ref[...]
    ms = jnp.mean(x * x, axis=-1, keepdims=True)
    h = (x * lax.rsqrt(ms + NORM_EPS) * g_ref[...]).astype(BF16)
    gate_ref[...] = jax.nn.sigmoid(jnp.dot(h, wg_ref[...], preferred_element_type=F32)).astype(gate_ref.dtype)
    mla_ref[...] = jnp.dot(h, wm_ref[...], preferred_element_type=F32)
    rwkv_ref[...] = jnp.dot(h, wr_ref[...], preferred_element_type=F32)
    conv_ref[...] = jnp.dot(h, wc_ref[...], preferred_element_type=F32)


def _in_proj(x2d, gain, wg, wm, wr, wc, tm=256):
    t, d = x2d.shape
    row = lambda n: pl.BlockSpec((tm, n), lambda i: (i, 0))
    widths = (wg.shape[1], wm.shape[1], wr.shape[1], wc.shape[1])
    return pl.pallas_call(
        _in_proj_body,
        grid=(t // tm,),
        in_specs=[row(d), _const_spec((1, d))] + [_const_spec(w.shape) for w in (wg, wm, wr, wc)],
        out_specs=[row(n) for n in widths],
        out_shape=[jax.ShapeDtypeStruct((t, widths[0]), BF16)]
        + [jax.ShapeDtypeStruct((t, n), F32) for n in widths[1:]],
        compiler_params=_cparams(("parallel",)),
        name="in_proj",
    )(x2d, gain, wg, wm, wr, wc)


def _mla_prep_body(c_ref, cos_ref, sin_ref, qan_ref, kvan_ref, wq_ref, wqr_ref, wk_ref, wv_ref,
                   gq_ref, gqr_ref, gk_ref, gkr_ref, q_ref, k_ref, v_ref):
    c = c_ref[0]
    cq = c[:, :Q_LORA_RANK]
    ckv = c[:, Q_LORA_RANK:Q_LORA_RANK + KV_LORA_RANK]
    kpe = c[:, Q_LORA_RANK + KV_LORA_RANK:Q_LORA_RANK + KV_LORA_RANK + LANES]
    kper = c[:, Q_LORA_RANK + KV_LORA_RANK + LANES:]

    def rms(z, g):
        return (z * lax.rsqrt(jnp.mean(z * z, axis=-1, keepdims=True) + NORM_EPS) * g).astype(BF16)

    cqn = rms(cq, qan_ref[...])
    ckvn = rms(ckv, kvan_ref[...])
    q = jnp.dot(cqn, wq_ref[...], preferred_element_type=F32)
    qr = jnp.dot(cqn, wqr_ref[...], preferred_element_type=F32)
    kn = jnp.dot(ckvn, wk_ref[...], preferred_element_type=F32)
    v = jnp.dot(ckvn, wv_ref[...], preferred_element_type=F32)
    cosf = cos_ref[0]
    sinf = sin_ref[0]
    gqc = gq_ref[...] * cosf
    gqs = gqr_ref[...] * sinf
    gkc = gk_ref[...] * cosf
    kper_s = kper * gkr_ref[...] * sinf
    inv_dim = 1.0 / QK_HEAD_DIM
    vt = v.T
    pad_row = lax.broadcasted_iota(jnp.int32, (VT_ROWS - V_HEAD_DIM, vt.shape[1]), 0)
    ones_rows = jnp.where(pad_row == 0, 1.0, 0.0)
    for h in range(MLA_HEADS):
        sl = slice(h * HEAD_PAD, (h + 1) * HEAD_PAD)
        qh = q[:, sl]
        rq = lax.rsqrt(jnp.sum(qh * qh, axis=-1, keepdims=True) * inv_dim + NORM_EPS)
        q_ref[0, h] = (rq * (qh * gqc + qr[:, sl] * gqs)).astype(q_ref.dtype)
        kh = kn[:, sl] + kpe
        rk = lax.rsqrt(jnp.sum(kh * kh, axis=-1, keepdims=True) * inv_dim + NORM_EPS)
        k_ref[0, h] = (rk * (kh * gkc + kper_s)).astype(k_ref.dtype)
        v_ref[0, h, 0] = jnp.concatenate([vt[h * V_HEAD_DIM:(h + 1) * V_HEAD_DIM, :], ones_rows],
                                         axis=0).astype(v_ref.dtype)


def _mla_prep(mla_cols, cosf, sinf, p):
    b, s, n = mla_cols.shape
    tm = ATTN_BLOCK
    tok = lambda w: pl.BlockSpec((1, tm, w), lambda bi, i: (bi, i, 0))
    head = lambda w: pl.BlockSpec((1, MLA_HEADS, tm, w), lambda bi, i: (bi, 0, i, 0))
    consts = (p["qan"], p["kvan"], p["wq"], p["wqr"], p["wk"], p["wv"], p["gq"], p["gqr"], p["gk"], p["gkr"])
    return pl.pallas_call(
        _mla_prep_body,
        grid=(b, s // tm),
        in_specs=[tok(n), tok(LANES), tok(LANES)] + [_const_spec(a.shape) for a in consts],
        out_specs=[head(HEAD_PAD), head(HEAD_PAD),
                   pl.BlockSpec((1, MLA_HEADS, 1, VT_ROWS, tm), lambda bi, i: (bi, 0, i, 0, 0))],
        out_shape=[jax.ShapeDtypeStruct((b, MLA_HEADS, s, HEAD_PAD), BF16),
                   jax.ShapeDtypeStruct((b, MLA_HEADS, s, HEAD_PAD), BF16),
                   jax.ShapeDtypeStruct((b, MLA_HEADS, s // tm, VT_ROWS, tm), BF16)],
        compiler_params=_cparams(("parallel", "parallel")),
        name="mla_prep",
    )(mla_cols, cosf, sinf, *consts)


MASK_VALUE = -1e30


ATTN_BLOCK = 256
ATTN_LOOKAHEAD = 4


def _attn_body(q_ref, k_ref, vt_ref, o_ref, m_ref, acc_ref, *, tq):
    i = pl.program_id(1)
    m_ref[...] = jnp.full(m_ref.shape, MASK_VALUE, F32)
    acc_ref[...] = jnp.zeros(acc_ref.shape, F32)
    key_idx = lax.broadcasted_iota(jnp.int32, (tq, tq), 0)
    qry_idx = lax.broadcasted_iota(jnp.int32, (tq, tq), 1)
    causal = key_idx <= qry_idx

    def block(j, masked):
        start = pl.multiple_of(j * tq, tq)

        def scores_t(h):
            kj = k_ref[0, h, pl.ds(start, tq), :]
            return lax.dot_general(kj, q_ref[0, h], (((1,), (1,)), ((), ())), preferred_element_type=F32)

        pending = [scores_t(h) for h in range(ATTN_LOOKAHEAD)]
        for h in range(MLA_HEADS):
            st = pending.pop(0)
            if h + ATTN_LOOKAHEAD < MLA_HEADS:
                pending.append(scores_t(h + ATTN_LOOKAHEAD))
            if masked:
                st = jnp.where(causal, st, MASK_VALUE)
            m_prev = m_ref[h]
            m_new = jnp.maximum(m_prev, jnp.max(st, axis=0, keepdims=True))
            alpha = jnp.exp2(m_prev - m_new)
            pt = jnp.exp2(st - m_new)
            acc_ref[h] = alpha * acc_ref[h] + jnp.dot(vt_ref[0, h, j], pt.astype(BF16), preferred_element_type=F32)
            m_ref[h] = m_new

    def loop_body(j, carry):
        block(j, False)
        return carry

    lax.fori_loop(0, i, loop_body, 0)
    block(i, True)
    for h in range(MLA_HEADS):
        out_t = acc_ref[h, 0:V_HEAD_DIM, :] / acc_ref[h, V_HEAD_DIM:V_HEAD_DIM + 1, :]
        o_ref[0, :, h * V_HEAD_DIM:(h + 1) * V_HEAD_DIM] = out_t.T.astype(o_ref.dtype)


def _attention(q, k, vt):
    b, nh, s, dp = q.shape
    tq = ATTN_BLOCK
    return pl.pallas_call(
        functools.partial(_attn_body, tq=tq),
        grid=(b, s // tq),
        in_specs=[pl.BlockSpec((1, nh, tq, dp), lambda bi, i: (bi, 0, i, 0)),
                  pl.BlockSpec((1, nh, s, dp), lambda bi, i: (bi, 0, 0, 0)),
                  pl.BlockSpec((1, nh, s // tq, VT_ROWS, tq), lambda bi, i: (bi, 0, 0, 0, 0))],
        out_specs=pl.BlockSpec((1, tq, nh * V_HEAD_DIM), lambda bi, i: (bi, i, 0)),
        out_shape=jax.ShapeDtypeStruct((b, s, nh * V_HEAD_DIM), BF16),
        scratch_shapes=[pltpu.VMEM((nh, 1, tq), F32), pltpu.VMEM((nh, VT_ROWS, tq), F32)],
        compiler_params=_cparams(("parallel", "arbitrary")),
        name="mla_attention",
    )(q, k, vt)


SCAN_VR = 16
SCAN_VQ = RWKV_HEAD_DIM // SCAN_VR
SCAN_TENSORS = 4
SCAN_TC = 64
PREP_TS = LANES
SCAN_KEY_GROUPS = 4
SCAN_KEYS_PER_GROUP = RWKV_HEAD_DIM // SCAN_KEY_GROUPS


def _rwkv_prep_body(*refs, has_vres, ts, nb):
    if has_vres:
        (x_ref, xp_ref, vf_ref, mu_ref, w0_ref, w2_ref, a0_ref, a2_ref, g2_ref, kk_ref, ka_ref, rk_ref, e_ref, tri_ref,
         v0_ref, v2_ref, xt_ref, val_o, dend_o, g_o, bonus_o, vt_ref) = refs
    else:
        (x_ref, xp_ref, mu_ref, w0_ref, w2_ref, a0_ref, a2_ref, g2_ref, kk_ref, ka_ref, rk_ref, e_ref, tri_ref,
         xt_ref, val_o, dend_o, g_o, bonus_o, vfirst_o, vt_ref) = refs
    i = pl.program_id(0)
    bh = nb * RWKV_HEADS

    def per_batch(b, carry):
        x = x_ref[b]
        prev = jnp.where(i > 0, xp_ref[b][7:8, :], 0.0)
        row = lax.broadcasted_iota(jnp.int32, (ts, 1), 0)
        shifted = jnp.where(row == 0, prev, pltpu.roll(x, 1, axis=0))
        xs = x + (shifted - x) * mu_ref[...]
        wd = RWKV_WIDTH
        r = xs[:, 0:wd]
        k = xs[:, wd:2 * wd]
        v = xs[:, 2 * wd:3 * wd]
        lora_in = xs[:, 3 * wd:3 * wd + LANES]
        xg = xs[:, 3 * wd + LANES:3 * wd + 2 * LANES]
        e = e_ref[...]
        zw = w0_ref[...] + _bdot(jnp.tanh(lora_in), w2_ref[...])
        nz = -zw
        softplus = jnp.maximum(nz, 0.0) + jnp.log(1.0 + jnp.exp(-jnp.abs(nz)))
        log_decay = -jnp.exp(-softplus - 0.5)
        log_d = jnp.dot(tri_ref[...], log_decay, preferred_element_type=F32, precision=lax.Precision.HIGHEST)
        d_incl = jnp.exp(log_d)
        d_prev = jnp.exp(log_d - log_decay)
        d_inv = jnp.exp(-log_d)
        ends = [d_incl[(c + 1) * SCAN_TC - 1:(c + 1) * SCAN_TC, :] for c in range(ts // SCAN_TC)]
        dend_o[0, pl.ds(b, 1), :] = jnp.concatenate(ends, axis=1)
        a_lr = jax.nn.sigmoid(a0_ref[...] + _bdot(lora_in, a2_ref[...]))
        g_o[b] = _bdot(jax.nn.sigmoid(xg), g2_ref[...])
        if has_vres:
            xvs = xs[:, RWKV_COLS:RWKV_COLS + LANES]
            v = v + (vf_ref[b] - v) * jax.nn.sigmoid(v0_ref[...] + _bdot(xvs, v2_ref[...]))
        else:
            vfirst_o[b] = v
        kk = k * kk_ref[...]
        norm = jnp.sqrt(_seg_sum(kk * kk, e))
        kk = kk / jnp.maximum(norm, 1e-12)
        k = k * (1.0 + (a_lr - 1.0) * ka_ref[...])
        bonus_o[b] = _seg_sum(r * k * rk_ref[...], e) * v
        rows = pl.ds(pl.multiple_of(b * wd, wd), wd)
        for idx, val in enumerate((-kk * d_prev, kk * a_lr * d_inv, k * d_inv, r * d_incl)):
            xt_ref[idx, rows, :] = val.T
        vt_ref[rows, :] = v.T
        return carry

    lax.fori_loop(0, nb, per_batch, 0)
    for vr in range(SCAN_VR):
        slab = jnp.concatenate(
            [vt_ref[pl.ds(vq * SCAN_VR + vr, bh, stride=RWKV_HEAD_DIM), :] for vq in range(SCAN_VQ)], axis=0)
        val_o[pl.ds(vr, ts, stride=SCAN_VR), :] = slab.T


def _rwkv_prep(rwkv_cols, v_first, p):
    b, s, n = rwkv_cols.shape
    ts = PREP_TS
    has_vres = v_first is not None
    tok = lambda w: pl.BlockSpec((b, ts, w), lambda i: (0, i, 0))
    halo = pl.BlockSpec((b, 8, n), lambda i: (0, jnp.maximum(i * (ts // 8) - 1, 0), 0))
    step = np.arange(ts)
    same_chunk = (step[:, None] // SCAN_TC) == (step[None, :] // SCAN_TC)
    chunk_tri = jnp.asarray(same_chunk & (step[None, :] <= step[:, None]), F32)
    consts = [p["mu"], p["w0"], p["w2"], p["a0"], p["a2"], p["g2"], p["k_k"], p["k_a"], p["r_k"], p["seg"], chunk_tri]
    args = [rwkv_cols, rwkv_cols]
    in_specs = [tok(n), halo]
    if has_vres:
        args.append(v_first)
        in_specs.append(tok(RWKV_WIDTH))
        consts += [p["v0"], p["v2"]]
    in_specs += [_const_spec(a.shape) for a in consts]
    n_tok_out = 2 if has_vres else 3
    return pl.pallas_call(
        functools.partial(_rwkv_prep_body, has_vres=has_vres, ts=ts, nb=b),
        grid=(s // ts,),
        in_specs=in_specs,
        out_specs=[pl.BlockSpec((SCAN_TENSORS, b * RWKV_WIDTH, ts), lambda i: (0, 0, i)),
                   pl.BlockSpec((ts * SCAN_VR, LANES), lambda i: (i, 0)),
                   pl.BlockSpec((1, b, SCAN_SUB * RWKV_WIDTH), lambda i: (i, 0, 0))] + [tok(RWKV_WIDTH)] * n_tok_out,
        out_shape=[jax.ShapeDtypeStruct((SCAN_TENSORS, b * RWKV_WIDTH, s), F32),
                   jax.ShapeDtypeStruct((s * SCAN_VR, LANES), F32),
                   jax.ShapeDtypeStruct((s // ts, b, SCAN_SUB * RWKV_WIDTH), F32)]
        + [jax.ShapeDtypeStruct((b, s, RWKV_WIDTH), F32)] * n_tok_out,
        scratch_shapes=[pltpu.VMEM((b * RWKV_WIDTH, ts), F32)],
        compiler_params=_cparams(("parallel",)),
        name="rwkv_prep",
    )(*args, *consts)


N_ACC = 4
SCAN_SUB = PREP_TS // SCAN_TC


SCAN_UNROLL = 8
SLABS_PER_TILE = SCAN_TENSORS * RWKV_HEAD_DIM
SLABS_PER_BODY = SLABS_PER_TILE * SCAN_UNROLL // PREP_TS
BODIES_PER_TENSOR = RWKV_HEAD_DIM // SLABS_PER_BODY


def _scan_body(xt_ref, v_ref, dprev_ref, dcur_ref, y_ref, s_ref, stage_ref, ych_ref, yt_ref, *, nb):
    g = pl.program_id(0)
    bh = nb * RWKV_HEADS
    wslot = g % 2
    rslot = 1 - wslot

    def stage_slabs(body_idx):
        tensor = body_idx // BODIES_PER_TENSOR
        key0 = (body_idx % BODIES_PER_TENSOR) * SLABS_PER_BODY
        for j in range(SLABS_PER_BODY):
            rows = xt_ref[tensor, pl.ds(key0 + j, bh, stride=RWKV_HEAD_DIM), :]
            stage_ref[wslot, tensor, key0 + j] = jnp.concatenate([rows] * SCAN_VQ, axis=0).T

    @pl.when(g == 0)
    def _():
        s_ref[...] = jnp.zeros_like(s_ref)

        def warm(body_idx, carry):
            stage_slabs(body_idx)
            return carry

        lax.fori_loop(0, SLABS_PER_TILE // SLABS_PER_BODY, warm, 0)

    def tree(acc):
        return (acc[0] + acc[1]) + (acc[2] + acc[3])

    def accumulate(acc, kk, term):
        acc[kk % N_ACC] = term if acc[kk % N_ACC] is None else acc[kk % N_ACC] + term

    def scan_chunk(chunk, dend_ref):
        t0 = chunk * SCAN_TC

        def operand(idx, kk, t):
            return stage_ref[rslot, idx, kk, pl.ds(t, 1), :]

        acc = [None] * N_ACC
        for kk in range(RWKV_HEAD_DIM):
            s_start = s_ref[kk] * dend_ref[0, pl.ds(kk, 1), :]
            s_ref[kk] = s_start
            accumulate(acc, kk, s_start * operand(0, kk, t0))

        def step(t, u):
            op = lambda idx, kk: operand(idx, kk, t)
            t_next = jnp.minimum(t + 1, t0 + SCAN_TC - 1)
            v = v_ref[pl.ds(pl.multiple_of(t * SCAN_VR, SCAN_VR), SCAN_VR), :]
            yacc = [None] * N_ACC
            uacc = [None] * N_ACC
            for kk in range(RWKV_HEAD_DIM):
                s_new = s_ref[kk] + (u * op(1, kk) + v * op(2, kk))
                s_ref[kk] = s_new
                accumulate(yacc, kk, s_new * op(3, kk))
                accumulate(uacc, kk, s_new * operand(0, kk, t_next))
            ych_ref[pl.ds(pl.multiple_of(t * SCAN_VR, SCAN_VR), SCAN_VR), :] = tree(yacc)
            return tree(uacc)

        def body(body_idx, u):
            stage_slabs(chunk * (SCAN_TC // SCAN_UNROLL) + body_idx)
            for j in range(SCAN_UNROLL):
                u = step(t0 + body_idx * SCAN_UNROLL + j, u)
            return u

        lax.fori_loop(0, SCAN_TC // SCAN_UNROLL, body, tree(acc))

    @pl.when(g > 0)
    def _():
        scan_chunk(0, dprev_ref)
        scan_chunk(1, dcur_ref)
        for vr in range(SCAN_VR):
            lanes_by_t = ych_ref[pl.ds(vr, PREP_TS, stride=SCAN_VR), :].T
            for vq in range(SCAN_VQ):
                yt_ref[pl.ds(vq * SCAN_VR + vr, bh, stride=RWKV_HEAD_DIM), :] = lanes_by_t[vq * bh:(vq + 1) * bh, :]
        for b in range(nb):
            y_ref[b] = yt_ref[b * RWKV_WIDTH:(b + 1) * RWKV_WIDTH, :].T


def _wkv_scan(xt, val, dend, nb):
    assert SCAN_SUB == 2
    s = xt.shape[-1]
    n_tiles = s // PREP_TS
    prev_tile = lambda g: jnp.maximum(g - 1, 0)
    return pl.pallas_call(
        functools.partial(_scan_body, nb=nb),
        grid=(n_tiles + 1,),
        in_specs=[
            pl.BlockSpec((SCAN_TENSORS, nb * RWKV_WIDTH, PREP_TS), lambda g: (0, 0, jnp.minimum(g, n_tiles - 1))),
            pl.BlockSpec((PREP_TS * SCAN_VR, LANES), lambda g: (prev_tile(g), 0)),
            pl.BlockSpec((1, RWKV_HEAD_DIM, LANES), lambda g: (jnp.maximum(prev_tile(g) * SCAN_SUB - 1, 0), 0, 0)),
            pl.BlockSpec((1, RWKV_HEAD_DIM, LANES), lambda g: (prev_tile(g) * SCAN_SUB, 0, 0))],
        out_specs=pl.BlockSpec((nb, PREP_TS, RWKV_WIDTH), lambda g: (0, prev_tile(g), 0)),
        out_shape=jax.ShapeDtypeStruct((nb, s, RWKV_WIDTH), F32),
        scratch_shapes=[pltpu.VMEM((RWKV_HEAD_DIM, SCAN_VR, LANES), F32),
                        pltpu.VMEM((2, SCAN_TENSORS, RWKV_HEAD_DIM, PREP_TS, LANES), F32),
                        pltpu.VMEM((PREP_TS * SCAN_VR, LANES), F32),
                        pltpu.VMEM((nb * RWKV_WIDTH, PREP_TS), F32)],
        compiler_params=_cparams(("arbitrary",)),
        name="wkv_scan",
    )(xt, val, dend, dend)


def _merge_body(x_ref, gate_ref, att_ref, y_ref, g_ref, bonus_ref, conv_ref, convp_ref,
                lnw_ref, lnb_ref, e_ref, cw_ref, wa_ref, wb_ref, wc_ref, wo_ref, out_ref, *, ts):
    i = pl.program_id(1)
    e = e_ref[...]
    y = y_ref[0]
    inv_n = 1.0 / RWKV_HEAD_DIM
    mean = _seg_sum(y, e) * inv_n
    d = y - mean
    var = _seg_sum(d * d, e) * inv_n
    yn = d * lax.rsqrt(var + GN_EPS) * lnw_ref[...] + lnb_ref[...] + bonus_ref[0]
    ob = _bdot(yn * g_ref[0], wb_ref[...])

    cw = CONV_WIDTH
    c = conv_ref[0]
    u = c[:, cw:2 * cw] * c[:, 2 * cw:3 * cw]
    cp = convp_ref[0]
    up = jnp.where(i > 0, cp[:, cw:2 * cw] * cp[:, 2 * cw:3 * cw], 0.0)
    p6 = up[6:7, :]
    p7 = up[7:8, :]
    row = lax.broadcasted_iota(jnp.int32, (ts, 1), 0)
    u1 = jnp.where(row == 0, p7, pltpu.roll(u, 1, axis=0))
    u2 = jnp.where(row == 0, p6, jnp.where(row == 1, p7, pltpu.roll(u, 2, axis=0)))
    taps = cw_ref[...]
    yc = taps[0:1, :] * u2 + taps[1:2, :] * u1 + taps[2:3, :] * u
    oc = _bdot(c[:, 0:cw] * yc, wc_ref[...])

    oa = jnp.dot(att_ref[0], wa_ref[...], preferred_element_type=F32)
    gates = gate_ref[0].astype(F32)
    dm = D_MODEL
    merged = gates[:, 0:dm] * oa + gates[:, dm:2 * dm] * ob + gates[:, 2 * dm:3 * dm] * oc
    out_ref[0] = x_ref[0] + _bdot(merged, wo_ref[...])


def _merge(x, gates, att, y, g, bonus, conv_cols, p, ts=256):
    b, s, d = x.shape
    tok = lambda w: pl.BlockSpec((1, ts, w), lambda bi, i: (bi, i, 0))
    nconv = conv_cols.shape[-1]
    halo = pl.BlockSpec((1, 8, nconv), lambda bi, i: (bi, jnp.maximum(i * (ts // 8) - 1, 0), 0))
    consts = (p["ln_w"], p["ln_b"], p["seg"], p["conv_w"], p["mla_w_o"], p["rwkv_w_o"], p["conv_w_o"], p["w_out"])
    return pl.pallas_call(
        functools.partial(_merge_body, ts=ts),
        grid=(b, s // ts),
        in_specs=[tok(d), tok(GATE_COLS), tok(att.shape[-1]), tok(RWKV_WIDTH), tok(RWKV_WIDTH), tok(RWKV_WIDTH),
                  tok(nconv), halo] + [_const_spec(a.shape) for a in consts],
        out_specs=tok(d),
        out_shape=jax.ShapeDtypeStruct((b, s, d), F32),
        compiler_params=_cparams(("parallel", "parallel")),
        name="branch_merge",
    )(x, gates, att, y, g, bonus, conv_cols, conv_cols, *consts)


def _mlp_body(x_ref, g_ref, wu_ref, wd_ref, o_ref):
    x = x_ref[...]
    ms = jnp.mean(x * x, axis=-1, keepdims=True)
    h = (x * lax.rsqrt(ms + NORM_EPS) * g_ref[...]).astype(BF16)
    up = jnp.dot(h, wu_ref[...], preferred_element_type=F32)
    act = jnp.square(jnp.maximum(up, 0.0)).astype(BF16)
    o_ref[...] = x + jnp.dot(act, wd_ref[...], preferred_element_type=F32)


def _mlp(x2d, gain, w_up, w_down, tm=256):
    t, d = x2d.shape
    row = pl.BlockSpec((tm, d), lambda i: (i, 0))
    return pl.pallas_call(
        _mlp_body,
        grid=(t // tm,),
        in_specs=[row, _const_spec((1, d)), _const_spec(w_up.shape), _const_spec(w_down.shape)],
        out_specs=row,
        out_shape=jax.ShapeDtypeStruct((t, d), F32),
        compiler_params=_cparams(("parallel",)),
        name="mlp",
    )(x2d, gain, w_up, w_down)


def _rope_partner_cols(w):
    half = QK_ROPE_DIM // 2
    return jnp.concatenate([-w[..., half:], w[..., :half]], axis=-1)


def _pad_lanes(w, lo, total=HEAD_PAD):
    n = w.shape[-1]
    pad = [(0, 0)] * (w.ndim - 1) + [(lo, total - lo - n)]
    return jnp.pad(w, pad)


def _layer_params(l, attn_norm, w_in, mla_q_a_norm, mla_wq_b, mla_kv_a_norm, mla_wkv_b, mla_q_norm, mla_k_norm,
                  mla_w_o, rwkv_mu, rwkv_w0, rwkv_w2, rwkv_a0, rwkv_a2, rwkv_g2, rwkv_k_k, rwkv_k_a, rwkv_r_k,
                  rwkv_ln_w, rwkv_ln_b, rwkv_w_o, rwkv_v1, rwkv_v_mu, rwkv_v0, rwkv_v2, conv_w, conv_w_o, w_out,
                  mlp_norm, w_up, w_down):
    p = {}
    row = lambda a: a.reshape(1, -1).astype(F32)
    w = w_in[l]
    o_mla = GATE_COLS
    o_rwkv = o_mla + MLA_COLS
    o_conv = o_rwkv + RWKV_COLS
    p["attn_norm"] = row(attn_norm[l])
    p["w_gate"] = w[:, :GATE_COLS].astype(BF16)
    w_kpe = w[:, o_mla + Q_LORA_RANK + KV_LORA_RANK:o_rwkv]
    p["w_mla"] = jnp.concatenate(
        [w[:, o_mla:o_mla + Q_LORA_RANK + KV_LORA_RANK], _pad_lanes(w_kpe, QK_NOPE_DIM),
         _pad_lanes(_rope_partner_cols(w_kpe), QK_NOPE_DIM)], axis=1).astype(BF16)
    w_rwkv = w[:, o_rwkv:o_conv]
    mu = rwkv_mu[l]
    if l > 0:
        w_rwkv = jnp.concatenate([w_rwkv, _pad_lanes(rwkv_v1[l - 1], 0)], axis=1)
        mu = jnp.concatenate([mu, _pad_lanes(rwkv_v_mu[l - 1], 0)])
        p["v0"] = row(rwkv_v0[l - 1])
        p["v2"] = jnp.pad(rwkv_v2[l - 1], ((0, LANES - MV_LORA), (0, 0)))
    p["w_rwkv"] = w_rwkv.astype(BF16)
    p["mu"] = row(mu)
    p["w_conv"] = w[:, o_conv:].astype(BF16)

    scale = QK_HEAD_DIM ** -0.5 * float(np.log2(np.e))
    wq = mla_wq_b[l].reshape(Q_LORA_RANK, MLA_HEADS, QK_HEAD_DIM)
    p["wq"] = _pad_lanes(wq, 0).reshape(Q_LORA_RANK, -1).astype(BF16)
    p["wqr"] = _pad_lanes(_rope_partner_cols(wq[..., QK_NOPE_DIM:]), QK_NOPE_DIM).reshape(Q_LORA_RANK, -1).astype(BF16)
    wkv = mla_wkv_b[l].reshape(KV_LORA_RANK, MLA_HEADS, QK_NOPE_DIM + V_HEAD_DIM)
    p["wk"] = _pad_lanes(wkv[..., :QK_NOPE_DIM], 0).reshape(KV_LORA_RANK, -1).astype(BF16)
    p["wv"] = wkv[..., QK_NOPE_DIM:].reshape(KV_LORA_RANK, -1).astype(BF16)
    p["qan"] = row(mla_q_a_norm[l])
    p["kvan"] = row(mla_kv_a_norm[l])
    swap = lambda g: jnp.concatenate([g[QK_ROPE_DIM // 2:], g[:QK_ROPE_DIM // 2]])
    gq, gk = mla_q_norm[l] * scale, mla_k_norm[l]
    p["gq"] = row(_pad_lanes(gq, 0))
    p["gqr"] = row(_pad_lanes(swap(gq[QK_NOPE_DIM:]), QK_NOPE_DIM))
    p["gk"] = row(_pad_lanes(gk, 0))
    p["gkr"] = row(_pad_lanes(swap(gk[QK_NOPE_DIM:]), QK_NOPE_DIM))
    p["mla_w_o"] = mla_w_o[l].astype(BF16)

    p["w0"] = row(rwkv_w0[l])
    p["w2"] = jnp.pad(rwkv_w2[l], ((0, AAA_LORA), (0, 0)))
    p["a0"] = row(rwkv_a0[l])
    p["a2"] = jnp.pad(rwkv_a2[l], ((DECAY_LORA, 0), (0, 0)))
    p["g2"] = rwkv_g2[l]
    p["k_k"] = row(rwkv_k_k[l])
    p["k_a"] = row(rwkv_k_a[l])
    p["r_k"] = row(rwkv_r_k[l])
    p["ln_w"] = row(rwkv_ln_w[l])
    p["ln_b"] = row(rwkv_ln_b[l])
    head_of = np.arange(RWKV_WIDTH) // RWKV_HEAD_DIM
    p["seg"] = jnp.asarray(head_of[:, None] == head_of[None, :], F32)
    p["rwkv_w_o"] = rwkv_w_o[l].astype(BF16)
    p["conv_w"] = conv_w[l].astype(F32)
    p["conv_w_o"] = conv_w_o[l].astype(BF16)
    p["w_out"] = w_out[l].astype(BF16)
    p["mlp_norm"] = row(mlp_norm[l])
    p["w_up"] = w_up[l].astype(BF16)
    p["w_down"] = w_down[l].astype(BF16)
    return p


def _rope_tables(positions):
    half = QK_ROPE_DIM // 2
    freqs = ROPE_THETA ** (-(jnp.arange(half, dtype=F32) * 2.0 / QK_ROPE_DIM))
    ang = positions.astype(F32)[..., None] * freqs
    cos, sin = jnp.cos(ang), jnp.sin(ang)
    ones = jnp.ones(positions.shape + (QK_NOPE_DIM,), F32)
    tail = jnp.ones(positions.shape + (HEAD_PAD - QK_HEAD_DIM,), F32)
    cosf = jnp.concatenate([ones, cos, cos, tail], axis=-1)
    sinf = jnp.concatenate([0 * ones, sin, sin, 0 * tail], axis=-1)
    return cosf, sinf


def kernel(x, positions, attn_norm, w_in, mla_q_a_norm, mla_wq_b, mla_kv_a_norm, mla_wkv_b, mla_q_norm, mla_k_norm, mla_w_o, rwkv_mu, rwkv_w0, rwkv_w2, rwkv_a0, rwkv_a2, rwkv_g2, rwkv_k_k, rwkv_k_a, rwkv_r_k, rwkv_ln_w, rwkv_ln_b, rwkv_w_o, rwkv_v1, rwkv_v_mu, rwkv_v0, rwkv_v2, conv_w, conv_w_o, w_out, mlp_norm, w_up, w_down):
    weights = (attn_norm, w_in, mla_q_a_norm, mla_wq_b, mla_kv_a_norm, mla_wkv_b, mla_q_norm, mla_k_norm, mla_w_o,
               rwkv_mu, rwkv_w0, rwkv_w2, rwkv_a0, rwkv_a2, rwkv_g2, rwkv_k_k, rwkv_k_a, rwkv_r_k, rwkv_ln_w,
               rwkv_ln_b, rwkv_w_o, rwkv_v1, rwkv_v_mu, rwkv_v0, rwkv_v2, conv_w, conv_w_o, w_out, mlp_norm,
               w_up, w_down)
    b, s, d = x.shape
    cosf, sinf = _rope_tables(positions)
    v_first = None
    for l in range(DEPTH):
        p = _layer_params(l, *weights)
        gates, mla_cols, rwkv_cols, conv_cols = _in_proj(
            x.reshape(b * s, d), p["attn_norm"], p["w_gate"], p["w_mla"], p["w_rwkv"], p["w_conv"])
        q, k, v = _mla_prep(mla_cols.reshape(b, s, -1), cosf, sinf, p)
        att = _attention(q, k, v)
        prep = _rwkv_prep(rwkv_cols.reshape(b, s, -1), v_first, p)
        scan_xt, scan_val, chunk_decay, g_, bonus = prep[:5]
        if l == 0:
            v_first = prep[5]
        dend = chunk_decay.reshape(-1, b, SCAN_SUB, RWKV_HEADS, RWKV_HEAD_DIM).transpose(0, 2, 4, 1, 3)
        dend = jnp.tile(dend.reshape(-1, RWKV_HEAD_DIM, b * RWKV_HEADS), (1, 1, SCAN_VQ))
        y = _wkv_scan(scan_xt, scan_val, dend, b)
        x = _merge(x, gates.reshape(b, s, -1), att, y, g_, bonus, conv_cols.reshape(b, s, -1), p)
        x = _mlp(x.reshape(b * s, d), p["mlp_norm"], p["w_up"], p["w_down"]).reshape(b, s, d)
    return x
```

```python
import functools

import jax
import jax.numpy as jnp
import numpy as np
from jax import lax
from jax.experimental import pallas as pl
from jax.experimental.pallas import tpu as pltpu

D_MODEL = 1024
DEPTH = 2
MLA_HEADS = 8
QK_NOPE_DIM = 64
QK_ROPE_DIM = 32
QK_HEAD_DIM = QK_NOPE_DIM + QK_ROPE_DIM
V_HEAD_DIM = 64
Q_LORA_RANK = 384
KV_LORA_RANK = 256
ROPE_THETA = 10000.0
RWKV_HEAD_DIM = 64
RWKV_HEADS = 4
RWKV_WIDTH = RWKV_HEADS * RWKV_HEAD_DIM
DECAY_LORA = 64
AAA_LORA = 64
GATE_LORA = 128
MV_LORA = 32
GN_EPS = 64e-5
CONV_WIDTH = 256
CONV_K = 3
D_FF = 4 * D_MODEL
N_BRANCH = 3
NORM_EPS = 1e-6
GATE_COLS = N_BRANCH * D_MODEL
MLA_COLS = Q_LORA_RANK + KV_LORA_RANK + QK_ROPE_DIM
RWKV_COLS = 3 * RWKV_WIDTH + DECAY_LORA + AAA_LORA + GATE_LORA

LANES = 128
HEAD_PAD = LANES
VT_ROWS = V_HEAD_DIM + 16
MLA_OUT_COLS = Q_LORA_RANK + KV_LORA_RANK + 2 * LANES
VMEM_LIMIT = 56 * 1024 * 1024

F32 = jnp.float32
BF16 = jnp.bfloat16


def _cparams(sem):
    return pltpu.CompilerParams(dimension_semantics=sem, vmem_limit_bytes=VMEM_LIMIT)


def _const_spec(shape):
    nd = len(shape)
    return pl.BlockSpec(shape, lambda *_: (0,) * nd, pipeline_mode=pl.Buffered(1))


def _bdot(a, b):
    return jnp.dot(a.astype(BF16), b.astype(BF16), preferred_element_type=F32)


def _seg_sum(x, e):
    return sum(jnp.dot(part, e, preferred_element_type=F32) for part in _split_bf16(x, 2))


def _split_bf16(x, parts):
    terms = []
    for _ in range(parts - 1):
        term = x.astype(BF16)
        terms.append(term)
        x = x - term.astype(F32)
    return terms + [x.astype(BF16)]


def _in_proj_body(x_ref, g_ref, wg_ref, wm_ref, wr_ref, wc_ref, gate_ref, mla_ref, rwkv_ref, conv_ref):
    x = x_ref[...]
    ms = jnp.mean(x * x, axis=-1, keepdims=True)
    h = (x * lax.rsqrt(ms + NORM_EPS) * g_ref[...]).astype(BF16)
    gate_ref[...] = jax.nn.sigmoid(jnp.dot(h, wg_ref[...], preferred_element_type=F32)).astype(gate_ref.dtype)
    mla_ref[...] = jnp.dot(h, wm_ref[...], preferred_element_type=F32)
    rwkv_ref[...] = jnp.dot(h, wr_ref[...], preferred_element_type=F32)
    conv_ref[...] = jnp.dot(h, wc_ref[...], preferred_element_type=F32)


def _in_proj(x2d, gain, wg, wm, wr, wc, tm=256):
    t, d = x2d.shape
    row = lambda n: pl.BlockSpec((tm, n), lambda i: (i, 0))
    widths = (wg.shape[1], wm.shape[1], wr.shape[1], wc.shape[1])
    return pl.pallas_call(
        _in_proj_body,
        grid=(t // tm,),
        in_specs=[row(d), _const_spec((1, d))] + [_const_spec(w.shape) for w in (wg, wm, wr, wc)],
        out_specs=[row(n) for n in widths],
        out_shape=[jax.ShapeDtypeStruct((t, widths[0]), BF16)]
        + [jax.ShapeDtypeStruct((t, n), F32) for n in widths[1:]],
        compiler_params=_cparams(("parallel",)),
        name="in_proj",
    )(x2d, gain, wg, wm, wr, wc)


def _mla_prep_body(c_ref, cos_ref, sin_ref, qan_ref, kvan_ref, wq_ref, wqr_ref, wk_ref, wv_ref,
                   gq_ref, gqr_ref, gk_ref, gkr_ref, q_ref, k_ref, v_ref):
    c = c_ref[0]
    cq = c[:, :Q_LORA_RANK]
    ckv = c[:, Q_LORA_RANK:Q_LORA_RANK + KV_LORA_RANK]
    kpe = c[:, Q_LORA_RANK + KV_LORA_RANK:Q_LORA_RANK + KV_LORA_RANK + LANES]
    kper = c[:, Q_LORA_RANK + KV_LORA_RANK + LANES:]

    def rms(z, g):
        return (z * lax.rsqrt(jnp.mean(z * z, axis=-1, keepdims=True) + NORM_EPS) * g).astype(BF16)

    cqn = rms(cq, qan_ref[...])
    ckvn = rms(ckv, kvan_ref[...])
    q = jnp.dot(cqn, wq_ref[...], preferred_element_type=F32)
    qr = jnp.dot(cqn, wqr_ref[...], preferred_element_type=F32)
    kn = jnp.dot(ckvn, wk_ref[...], preferred_element_type=F32)
    v = jnp.dot(ckvn, wv_ref[...], preferred_element_type=F32)
    cosf = cos_ref[0]
    sinf = sin_ref[0]
    gqc = gq_ref[...] * cosf
    gqs = gqr_ref[...] * sinf
    gkc = gk_ref[...] * cosf
    kper_s = kper * gkr_ref[...] * sinf
    inv_dim = 1.0 / QK_HEAD_DIM
    vt = v.T
    pad_row = lax.broadcasted_iota(jnp.int32, (VT_ROWS - V_HEAD_DIM, vt.shape[1]), 0)
    ones_rows = jnp.where(pad_row == 0, 1.0, 0.0)
    for h in range(MLA_HEADS):
        sl = slice(h * HEAD_PAD, (h + 1) * HEAD_PAD)
        qh = q[:, sl]
        rq = lax.rsqrt(jnp.sum(qh * qh, axis=-1, keepdims=True) * inv_dim + NORM_EPS)
        q_ref[0, h] = (rq * (qh * gqc + qr[:, sl] * gqs)).astype(q_ref.dtype)
        kh = kn[:, sl] + kpe
        rk = lax.rsqrt(jnp.sum(kh * kh, axis=-1, keepdims=True) * inv_dim + NORM_EPS)
        k_ref[0, h] = (rk * (kh * gkc + kper_s)).astype(k_ref.dtype)
        v_ref[0, h, 0] = jnp.concatenate([vt[h * V_HEAD_DIM:(h + 1) * V_HEAD_DIM, :], ones_rows],
                                         axis=0).astype(v_ref.dtype)


def _mla_prep(mla_cols, cosf, sinf, p):
    b, s, n = mla_cols.shape
    tm = ATTN_BLOCK
    tok = lambda w: pl.BlockSpec((1, tm, w), lambda bi, i: (bi, i, 0))
    head = lambda w: pl.BlockSpec((1, MLA_HEADS, tm, w), lambda bi, i: (bi, 0, i, 0))
    consts = (p["qan"], p["kvan"], p["wq"], p["wqr"], p["wk"], p["wv"], p["gq"], p["gqr"], p["gk"], p["gkr"])
    return pl.pallas_call(
        _mla_prep_body,
        grid=(b, s // tm),
        in_specs=[tok(n), tok(LANES), tok(LANES)] + [_const_spec(a.shape) for a in consts],
        out_specs=[head(HEAD_PAD), head(HEAD_PAD),
                   pl.BlockSpec((1, MLA_HEADS, 1, VT_ROWS, tm), lambda bi, i: (bi, 0, i, 0, 0))],
        out_shape=[jax.ShapeDtypeStruct((b, MLA_HEADS, s, HEAD_PAD), BF16),
                   jax.ShapeDtypeStruct((b, MLA_HEADS, s, HEAD_PAD), BF16),
                   jax.ShapeDtypeStruct((b, MLA_HEADS, s // tm, VT_ROWS, tm), BF16)],
        compiler_params=_cparams(("parallel", "parallel")),
        name="mla_prep",
    )(mla_cols, cosf, sinf, *consts)


MASK_VALUE = -1e30


ATTN_BLOCK = 256
ATTN_LOOKAHEAD = 4


def _attn_body(q_ref, k_ref, vt_ref, o_ref, m_ref, acc_ref, pend_ref, *, tq):
    i = pl.program_id(1)
    m_ref[...] = jnp.full(m_ref.shape, MASK_VALUE, F32)
    acc_ref[...] = jnp.zeros(acc_ref.shape, F32)
    key_idx = lax.broadcasted_iota(jnp.int32, (tq, tq), 0)
    qry_idx = lax.broadcasted_iota(jnp.int32, (tq, tq), 1)
    causal = key_idx <= qry_idx

    def scores_t(j, h):
        kj = k_ref[0, h, pl.ds(pl.multiple_of(j * tq, tq), tq), :]
        return lax.dot_general(kj, q_ref[0, h], (((1,), (1,)), ((), ())), preferred_element_type=F32)

    for h in range(ATTN_LOOKAHEAD):
        pend_ref[h] = scores_t(0, h)

    def block(j, carry):
        j_next = jnp.minimum(j + 1, i)
        visible = jnp.logical_or(causal, j < i)
        pending = [pend_ref[h] for h in range(ATTN_LOOKAHEAD)]
        for h in range(MLA_HEADS):
            st = pending.pop(0)
            ahead = h + ATTN_LOOKAHEAD
            if ahead < MLA_HEADS:
                pending.append(scores_t(j, ahead))
            else:
                pend_ref[ahead - MLA_HEADS] = scores_t(j_next, ahead - MLA_HEADS)
            st = jnp.where(visible, st, MASK_VALUE)
            m_prev = m_ref[h]
            m_new = jnp.maximum(m_prev, jnp.max(st, axis=0, keepdims=True))
            alpha = jnp.exp2(m_prev - m_new)
            pt = jnp.exp2(st - m_new)
            acc_ref[h] = alpha * acc_ref[h] + jnp.dot(vt_ref[0, h, j], pt.astype(BF16), preferred_element_type=F32)
            m_ref[h] = m_new
        return carry

    lax.fori_loop(0, i + 1, block, 0)
    for h in range(MLA_HEADS):
        out_t = acc_ref[h, 0:V_HEAD_DIM, :] / acc_ref[h, V_HEAD_DIM:V_HEAD_DIM + 1, :]
        o_ref[0, :, h * V_HEAD_DIM:(h + 1) * V_HEAD_DIM] = out_t.T.astype(o_ref.dtype)


def _attention(q, k, vt):
    b, nh, s, dp = q.shape
    tq = ATTN_BLOCK
    return pl.pallas_call(
        functools.partial(_attn_body, tq=tq),
        grid=(b, s // tq),
        in_specs=[pl.BlockSpec((1, nh, tq, dp), lambda bi, i: (bi, 0, i, 0)),
                  pl.BlockSpec((1, nh, s, dp), lambda bi, i: (bi, 0, 0, 0)),
                  pl.BlockSpec((1, nh, s // tq, VT_ROWS, tq), lambda bi, i: (bi, 0, 0, 0, 0))],
        out_specs=pl.BlockSpec((1, tq, nh * V_HEAD_DIM), lambda bi, i: (bi, i, 0)),
        out_shape=jax.ShapeDtypeStruct((b, s, nh * V_HEAD_DIM), BF16),
        scratch_shapes=[pltpu.VMEM((nh, 1, tq), F32), pltpu.VMEM((nh, VT_ROWS, tq), F32),
                        pltpu.VMEM((ATTN_LOOKAHEAD, tq, tq), F32)],
        compiler_params=_cparams(("parallel", "arbitrary")),
        name="mla_attention",
    )(q, k, vt)


SCAN_VR = 16
SCAN_VQ = RWKV_HEAD_DIM // SCAN_VR
SCAN_TENSORS = 4
SCAN_TC = 64
PREP_TS = LANES


def _rwkv_prep_body(*refs, has_vres, ts, nb):
    if has_vres:
        (x_ref, xp_ref, vf_ref, mu_ref, w0_ref, w2_ref, a0_ref, a2_ref, g2_ref, kk_ref, ka_ref, rk_ref, e_ref, tri_ref,
         v0_ref, v2_ref, xt_ref, val_o, dend_o, g_o, bonus_o, vt_ref) = refs
    else:
        (x_ref, xp_ref, mu_ref, w0_ref, w2_ref, a0_ref, a2_ref, g2_ref, kk_ref, ka_ref, rk_ref, e_ref, tri_ref,
         xt_ref, val_o, dend_o, g_o, bonus_o, vfirst_o, vt_ref) = refs
    i = pl.program_id(0)
    bh = nb * RWKV_HEADS

    def per_batch(b, carry):
        x = x_ref[b]
        prev = jnp.where(i > 0, xp_ref[b][7:8, :], 0.0)
        row = lax.broadcasted_iota(jnp.int32, (ts, 1), 0)
        shifted = jnp.where(row == 0, prev, pltpu.roll(x, 1, axis=0))
        xs = x + (shifted - x) * mu_ref[...]
        wd = RWKV_WIDTH
        r = xs[:, 0:wd]
        k = xs[:, wd:2 * wd]
        v = xs[:, 2 * wd:3 * wd]
        lora_in = xs[:, 3 * wd:3 * wd + LANES]
        xg = xs[:, 3 * wd + LANES:3 * wd + 2 * LANES]
        e = e_ref[...]
        zw = w0_ref[...] + _bdot(jnp.tanh(lora_in), w2_ref[...])
        nz = -zw
        softplus = jnp.maximum(nz, 0.0) + jnp.log(1.0 + jnp.exp(-jnp.abs(nz)))
        log_decay = -jnp.exp(-softplus - 0.5)
        tri = tri_ref[...]
        log_d = sum(jnp.dot(tri, part, preferred_element_type=F32) for part in _split_bf16(log_decay, 3))
        d_incl = jnp.exp(log_d)
        d_prev = jnp.exp(log_d - log_decay)
        d_inv = jnp.exp(-log_d)
        ends = [d_incl[(c + 1) * SCAN_TC - 1:(c + 1) * SCAN_TC, :] for c in range(ts // SCAN_TC)]
        dend_o[0, pl.ds(b, 1), :] = jnp.concatenate(ends, axis=1)
        a_lr = jax.nn.sigmoid(a0_ref[...] + _bdot(lora_in, a2_ref[...]))
        g_o[b] = _bdot(jax.nn.sigmoid(xg), g2_ref[...])
        if has_vres:
            xvs = xs[:, RWKV_COLS:RWKV_COLS + LANES]
            v = v + (vf_ref[b] - v) * jax.nn.sigmoid(v0_ref[...] + _bdot(xvs, v2_ref[...]))
        else:
            vfirst_o[b] = v
        kk = k * kk_ref[...]
        norm = jnp.sqrt(_seg_sum(kk * kk, e))
        kk = kk / jnp.maximum(norm, 1e-12)
        k = k * (1.0 + (a_lr - 1.0) * ka_ref[...])
        bonus_o[b] = _seg_sum(r * k * rk_ref[...], e) * v
        rows = pl.ds(pl.multiple_of(b * wd, wd), wd)
        for idx, val in enumerate((-kk * d_prev, kk * a_lr * d_inv, k * d_inv, r * d_incl)):
            xt_ref[idx, rows, :] = val.T
        vt_ref[rows, :] = v.T
        return carry

    lax.fori_loop(0, nb, per_batch, 0)
    for vr in range(SCAN_VR):
        slab = jnp.concatenate(
            [vt_ref[pl.ds(vq * SCAN_VR + vr, bh, stride=RWKV_HEAD_DIM), :] for vq in range(SCAN_VQ)], axis=0)
        val_o[pl.ds(vr, ts, stride=SCAN_VR), :] = slab.T


def _rwkv_prep(rwkv_cols, v_first, p):
    b, s, n = rwkv_cols.shape
    ts = PREP_TS
    has_vres = v_first is not None
    tok = lambda w: pl.BlockSpec((b, ts, w), lambda i: (0, i, 0))
    halo = pl.BlockSpec((b, 8, n), lambda i: (0, jnp.maximum(i * (ts // 8) - 1, 0), 0))
    step = np.arange(ts)
    same_chunk = (step[:, None] // SCAN_TC) == (step[None, :] // SCAN_TC)
    chunk_tri = jnp.asarray(same_chunk & (step[None, :] <= step[:, None]), BF16)
    consts = [p["mu"], p["w0"], p["w2"], p["a0"], p["a2"], p["g2"], p["k_k"], p["k_a"], p["r_k"], p["seg"], chunk_tri]
    args = [rwkv_cols, rwkv_cols]
    in_specs = [tok(n), halo]
    if has_vres:
        args.append(v_first)
        in_specs.append(tok(RWKV_WIDTH))
        consts += [p["v0"], p["v2"]]
    in_specs += [_const_spec(a.shape) for a in consts]
    n_tok_out = 2 if has_vres else 3
    return pl.pallas_call(
        functools.partial(_rwkv_prep_body, has_vres=has_vres, ts=ts, nb=b),
        grid=(s // ts,),
        in_specs=in_specs,
        out_specs=[pl.BlockSpec((SCAN_TENSORS, b * RWKV_WIDTH, ts), lambda i: (0, 0, i)),
                   pl.BlockSpec((ts * SCAN_VR, LANES), lambda i: (i, 0)),
                   pl.BlockSpec((1, b, SCAN_SUB * RWKV_WIDTH), lambda i: (i, 0, 0))] + [tok(RWKV_WIDTH)] * n_tok_out,
        out_shape=[jax.ShapeDtypeStruct((SCAN_TENSORS, b * RWKV_WIDTH, s), F32),
                   jax.ShapeDtypeStruct((s * SCAN_VR, LANES), F32),
                   jax.ShapeDtypeStruct((s // ts, b, SCAN_SUB * RWKV_WIDTH), F32)]
        + [jax.ShapeDtypeStruct((b, s, RWKV_WIDTH), F32)] * n_tok_out,
        scratch_shapes=[pltpu.VMEM((b * RWKV_WIDTH, ts), F32)],
        compiler_params=_cparams(("parallel",)),
        name="rwkv_prep",
    )(*args, *consts)


N_ACC = 4
SCAN_SUB = PREP_TS // SCAN_TC


SCAN_UNROLL = 8
SLABS_PER_TILE = SCAN_TENSORS * RWKV_HEAD_DIM
SLABS_PER_BODY = SLABS_PER_TILE * SCAN_UNROLL // PREP_TS
BODIES_PER_TENSOR = RWKV_HEAD_DIM // SLABS_PER_BODY


def _scan_body(xt_ref, v_ref, dprev_ref, dcur_ref, y_ref, s_ref, stage_ref, ych_ref, yt_ref, *, nb):
    g = pl.program_id(0)
    bh = nb * RWKV_HEADS
    wslot = g % 2
    rslot = 1 - wslot

    def stage_slabs(body_idx):
        tensor = body_idx // BODIES_PER_TENSOR
        key0 = (body_idx % BODIES_PER_TENSOR) * SLABS_PER_BODY
        for j in range(SLABS_PER_BODY):
            rows = xt_ref[tensor, pl.ds(key0 + j, bh, stride=RWKV_HEAD_DIM), :]
            stage_ref[wslot, tensor, key0 + j] = jnp.concatenate([rows] * SCAN_VQ, axis=0).T

    @pl.when(g == 0)
    def _():
        s_ref[...] = jnp.zeros_like(s_ref)

        def warm(body_idx, carry):
            stage_slabs(body_idx)
            return carry

        lax.fori_loop(0, SLABS_PER_TILE // SLABS_PER_BODY, warm, 0)

    def tree(acc):
        return (acc[0] + acc[1]) + (acc[2] + acc[3])

    def accumulate(acc, kk, term):
        acc[kk % N_ACC] = term if acc[kk % N_ACC] is None else acc[kk % N_ACC] + term

    def scan_chunk(chunk, dend_ref):
        t0 = chunk * SCAN_TC

        def operand(idx, kk, t):
            return stage_ref[rslot, idx, kk, pl.ds(t, 1), :]

        acc = [None] * N_ACC
        for kk in range(RWKV_HEAD_DIM):
            s_start = s_ref[kk] * dend_ref[0, pl.ds(kk, 1), :]
            s_ref[kk] = s_start
            accumulate(acc, kk, s_start * operand(0, kk, t0))

        def step(t, u):
            op = lambda idx, kk: operand(idx, kk, t)
            t_next = jnp.minimum(t + 1, t0 + SCAN_TC - 1)
            v = v_ref[pl.ds(pl.multiple_of(t * SCAN_VR, SCAN_VR), SCAN_VR), :]
            yacc = [None] * N_ACC
            uacc = [None] * N_ACC
            for kk in range(RWKV_HEAD_DIM):
                s_new = s_ref[kk] + (u * op(1, kk) + v * op(2, kk))
                s_ref[kk] = s_new
                accumulate(yacc, kk, s_new * op(3, kk))
                accumulate(uacc, kk, s_new * operand(0, kk, t_next))
            ych_ref[pl.ds(pl.multiple_of(t * SCAN_VR, SCAN_VR), SCAN_VR), :] = tree(yacc)
            return tree(uacc)

        def body(body_idx, u):
            stage_slabs(chunk * (SCAN_TC // SCAN_UNROLL) + body_idx)
            for j in range(SCAN_UNROLL):
                u = step(t0 + body_idx * SCAN_UNROLL + j, u)
            return u

        lax.fori_loop(0, SCAN_TC // SCAN_UNROLL, body, tree(acc))

    @pl.when(g > 0)
    def _():
        scan_chunk(0, dprev_ref)
        scan_chunk(1, dcur_ref)
        for vr in range(SCAN_VR):
            lanes_by_t = ych_ref[pl.ds(vr, PREP_TS, stride=SCAN_VR), :].T
            for vq in range(SCAN_VQ):
                yt_ref[pl.ds(vq * SCAN_VR + vr, bh, stride=RWKV_HEAD_DIM), :] = lanes_by_t[vq * bh:(vq + 1) * bh, :]
        for b in range(nb):
            y_ref[b] = yt_ref[b * RWKV_WIDTH:(b + 1) * RWKV_WIDTH, :].T


def _wkv_scan(xt, val, dend, nb):
    assert SCAN_SUB == 2
    s = xt.shape[-1]
    n_tiles = s // PREP_TS
    prev_tile = lambda g: jnp.maximum(g - 1, 0)
    return pl.pallas_call(
        functools.partial(_scan_body, nb=nb),
        grid=(n_tiles + 1,),
        in_specs=[
            pl.BlockSpec((SCAN_TENSORS, nb * RWKV_WIDTH, PREP_TS), lambda g: (0, 0, jnp.minimum(g, n_tiles - 1))),
            pl.BlockSpec((PREP_TS * SCAN_VR, LANES), lambda g: (prev_tile(g), 0)),
            pl.BlockSpec((1, RWKV_HEAD_DIM, LANES), lambda g: (jnp.maximum(prev_tile(g) * SCAN_SUB - 1, 0), 0, 0)),
            pl.BlockSpec((1, RWKV_HEAD_DIM, LANES), lambda g: (prev_tile(g) * SCAN_SUB, 0, 0))],
        out_specs=pl.BlockSpec((nb, PREP_TS, RWKV_WIDTH), lambda g: (0, prev_tile(g), 0)),
        out_shape=jax.ShapeDtypeStruct((nb, s, RWKV_WIDTH), F32),
        scratch_shapes=[pltpu.VMEM((RWKV_HEAD_DIM, SCAN_VR, LANES), F32),
                        pltpu.VMEM((2, SCAN_TENSORS, RWKV_HEAD_DIM, PREP_TS, LANES), F32),
                        pltpu.VMEM((PREP_TS * SCAN_VR, LANES), F32),
                        pltpu.VMEM((nb * RWKV_WIDTH, PREP_TS), F32)],
        compiler_params=_cparams(("arbitrary",)),
        name="wkv_scan",
    )(xt, val, dend, dend)


def _merge_body(x_ref, gate_ref, att_ref, y_ref, g_ref, bonus_ref, conv_ref, convp_ref,
                lnw_ref, lnb_ref, e_ref, cw_ref, wa_ref, wb_ref, wc_ref, wo_ref, out_ref, *, ts):
    i = pl.program_id(1)
    e = e_ref[...]
    y = y_ref[0]
    inv_n = 1.0 / RWKV_HEAD_DIM
    mean = _seg_sum(y, e) * inv_n
    d = y - mean
    var = _seg_sum(d * d, e) * inv_n
    yn = d * lax.rsqrt(var + GN_EPS) * lnw_ref[...] + lnb_ref[...] + bonus_ref[0]
    ob = _bdot(yn * g_ref[0], wb_ref[...])

    cw = CONV_WIDTH
    c = conv_ref[0]
    u = c[:, cw:2 * cw] * c[:, 2 * cw:3 * cw]
    cp = convp_ref[0]
    up = jnp.where(i > 0, cp[:, cw:2 * cw] * cp[:, 2 * cw:3 * cw], 0.0)
    p6 = up[6:7, :]
    p7 = up[7:8, :]
    row = lax.broadcasted_iota(jnp.int32, (ts, 1), 0)
    u1 = jnp.where(row == 0, p7, pltpu.roll(u, 1, axis=0))
    u2 = jnp.where(row == 0, p6, jnp.where(row == 1, p7, pltpu.roll(u, 2, axis=0)))
    taps = cw_ref[...]
    yc = taps[0:1, :] * u2 + taps[1:2, :] * u1 + taps[2:3, :] * u
    oc = _bdot(c[:, 0:cw] * yc, wc_ref[...])

    oa = jnp.dot(att_ref[0], wa_ref[...], preferred_element_type=F32)
    gates = gate_ref[0].astype(F32)
    dm = D_MODEL
    merged = gates[:, 0:dm] * oa + gates[:, dm:2 * dm] * ob + gates[:, 2 * dm:3 * dm] * oc
    out_ref[0] = x_ref[0] + _bdot(merged, wo_ref[...])


def _merge(x, gates, att, y, g, bonus, conv_cols, p, ts=512):
    b, s, d = x.shape
    tok = lambda w: pl.BlockSpec((1, ts, w), lambda bi, i: (bi, i, 0))
    nconv = conv_cols.shape[-1]
    halo = pl.BlockSpec((1, 8, nconv), lambda bi, i: (bi, jnp.maximum(i * (ts // 8) - 1, 0), 0))
    consts = (p["ln_w"], p["ln_b"], p["seg"], p["conv_w"], p["mla_w_o"], p["rwkv_w_o"], p["conv_w_o"], p["w_out"])
    return pl.pallas_call(
        functools.partial(_merge_body, ts=ts),
        grid=(b, s // ts),
        in_specs=[tok(d), tok(GATE_COLS), tok(att.shape[-1]), tok(RWKV_WIDTH), tok(RWKV_WIDTH), tok(RWKV_WIDTH),
                  tok(nconv), halo] + [_const_spec(a.shape) for a in consts],
        out_specs=tok(d),
        out_shape=jax.ShapeDtypeStruct((b, s, d), F32),
        compiler_params=_cparams(("parallel", "parallel")),
        name="branch_merge",
    )(x, gates, att, y, g, bonus, conv_cols, conv_cols, *consts)


def _mlp_body(x_ref, g_ref, wu_ref, wd_ref, o_ref):
    x = x_ref[...]
    ms = jnp.mean(x * x, axis=-1, keepdims=True)
    h = (x * lax.rsqrt(ms + NORM_EPS) * g_ref[...]).astype(BF16)
    up = jnp.dot(h, wu_ref[...], preferred_element_type=F32)
    act = jnp.square(jnp.maximum(up, 0.0)).astype(BF16)
    o_ref[...] = x + jnp.dot(act, wd_ref[...], preferred_element_type=F32)


def _mlp(x2d, gain, w_up, w_down, tm=256):
    t, d = x2d.shape
    row = pl.BlockSpec((tm, d), lambda i: (i, 0))
    return pl.pallas_call(
        _mlp_body,
        grid=(t // tm,),
        in_specs=[row, _const_spec((1, d)), _const_spec(w_up.shape), _const_spec(w_down.shape)],
        out_specs=row,
        out_shape=jax.ShapeDtypeStruct((t, d), F32),
        compiler_params=_cparams(("parallel",)),
        name="mlp",
    )(x2d, gain, w_up, w_down)


def _rope_partner_cols(w):
    half = QK_ROPE_DIM // 2
    return jnp.concatenate([-w[..., half:], w[..., :half]], axis=-1)


def _pad_lanes(w, lo, total=HEAD_PAD):
    n = w.shape[-1]
    pad = [(0, 0)] * (w.ndim - 1) + [(lo, total - lo - n)]
    return jnp.pad(w, pad)


def _layer_params(l, attn_norm, w_in, mla_q_a_norm, mla_wq_b, mla_kv_a_norm, mla_wkv_b, mla_q_norm, mla_k_norm,
                  mla_w_o, rwkv_mu, rwkv_w0, rwkv_w2, rwkv_a0, rwkv_a2, rwkv_g2, rwkv_k_k, rwkv_k_a, rwkv_r_k,
                  rwkv_ln_w, rwkv_ln_b, rwkv_w_o, rwkv_v1, rwkv_v_mu, rwkv_v0, rwkv_v2, conv_w, conv_w_o, w_out,
                  mlp_norm, w_up, w_down):
    p = {}
    row = lambda a: a.reshape(1, -1).astype(F32)
    w = w_in[l]
    o_mla = GATE_COLS
    o_rwkv = o_mla + MLA_COLS
    o_conv = o_rwkv + RWKV_COLS
    p["attn_norm"] = row(attn_norm[l])
    p["w_gate"] = w[:, :GATE_COLS].astype(BF16)
    w_kpe = w[:, o_mla + Q_LORA_RANK + KV_LORA_RANK:o_rwkv]
    p["w_mla"] = jnp.concatenate(
        [w[:, o_mla:o_mla + Q_LORA_RANK + KV_LORA_RANK], _pad_lanes(w_kpe, QK_NOPE_DIM),
         _pad_lanes(_rope_partner_cols(w_kpe), QK_NOPE_DIM)], axis=1).astype(BF16)
    w_rwkv = w[:, o_rwkv:o_conv]
    mu = rwkv_mu[l]
    if l > 0:
        w_rwkv = jnp.concatenate([w_rwkv, _pad_lanes(rwkv_v1[l - 1], 0)], axis=1)
        mu = jnp.concatenate([mu, _pad_lanes(rwkv_v_mu[l - 1], 0)])
        p["v0"] = row(rwkv_v0[l - 1])
        p["v2"] = jnp.pad(rwkv_v2[l - 1], ((0, LANES - MV_LORA), (0, 0)))
    p["w_rwkv"] = w_rwkv.astype(BF16)
    p["mu"] = row(mu)
    p["w_conv"] = w[:, o_conv:].astype(BF16)

    scale = QK_HEAD_DIM ** -0.5 * float(np.log2(np.e))
    wq = mla_wq_b[l].reshape(Q_LORA_RANK, MLA_HEADS, QK_HEAD_DIM)
    p["wq"] = _pad_lanes(wq, 0).reshape(Q_LORA_RANK, -1).astype(BF16)
    p["wqr"] = _pad_lanes(_rope_partner_cols(wq[..., QK_NOPE_DIM:]), QK_NOPE_DIM).reshape(Q_LORA_RANK, -1).astype(BF16)
    wkv = mla_wkv_b[l].reshape(KV_LORA_RANK, MLA_HEADS, QK_NOPE_DIM + V_HEAD_DIM)
    p["wk"] = _pad_lanes(wkv[..., :QK_NOPE_DIM], 0).reshape(KV_LORA_RANK, -1).astype(BF16)
    p["wv"] = wkv[..., QK_NOPE_DIM:].reshape(KV_LORA_RANK, -1).astype(BF16)
    p["qan"] = row(mla_q_a_norm[l])
    p["kvan"] = row(mla_kv_a_norm[l])
    swap = lambda g: jnp.concatenate([g[QK_ROPE_DIM // 2:], g[:QK_ROPE_DIM // 2]])
    gq, gk = mla_q_norm[l] * scale, mla_k_norm[l]
    p["gq"] = row(_pad_lanes(gq, 0))
    p["gqr"] = row(_pad_lanes(swap(gq[QK_NOPE_DIM:]), QK_NOPE_DIM))
    p["gk"] = row(_pad_lanes(gk, 0))
    p["gkr"] = row(_pad_lanes(swap(gk[QK_NOPE_DIM:]), QK_NOPE_DIM))
    p["mla_w_o"] = mla_w_o[l].astype(BF16)

    p["w0"] = row(rwkv_w0[l])
    p["w2"] = jnp.pad(rwkv_w2[l], ((0, AAA_LORA), (0, 0)))
    p["a0"] = row(rwkv_a0[l])
    p["a2"] = jnp.pad(rwkv_a2[l], ((DECAY_LORA, 0), (0, 0)))
    p["g2"] = rwkv_g2[l]
    p["k_k"] = row(rwkv_k_k[l])
    p["k_a"] = row(rwkv_k_a[l])
    p["r_k"] = row(rwkv_r_k[l])
    p["ln_w"] = row(rwkv_ln_w[l])
    p["ln_b"] = row(rwkv_ln_b[l])
    head_of = np.arange(RWKV_WIDTH) // RWKV_HEAD_DIM
    p["seg"] = jnp.asarray(head_of[:, None] == head_of[None, :], BF16)
    p["rwkv_w_o"] = rwkv_w_o[l].astype(BF16)
    p["conv_w"] = conv_w[l].astype(F32)
    p["conv_w_o"] = conv_w_o[l].astype(BF16)
    p["w_out"] = w_out[l].astype(BF16)
    p["mlp_norm"] = row(mlp_norm[l])
    p["w_up"] = w_up[l].astype(BF16)
    p["w_down"] = w_down[l].astype(BF16)
    return p


def _rope_tables(positions):
    half = QK_ROPE_DIM // 2
    freqs = ROPE_THETA ** (-(jnp.arange(half, dtype=F32) * 2.0 / QK_ROPE_DIM))
    ang = positions.astype(F32)[..., None] * freqs
    cos, sin = jnp.cos(ang), jnp.sin(ang)
    ones = jnp.ones(positions.shape + (QK_NOPE_DIM,), F32)
    tail = jnp.ones(positions.shape + (HEAD_PAD - QK_HEAD_DIM,), F32)
    cosf = jnp.concatenate([ones, cos, cos, tail], axis=-1)
    sinf = jnp.concatenate([0 * ones, sin, sin, 0 * tail], axis=-1)
    return cosf, sinf


def kernel(x, positions, attn_norm, w_in, mla_q_a_norm, mla_wq_b, mla_kv_a_norm, mla_wkv_b, mla_q_norm, mla_k_norm, mla_w_o, rwkv_mu, rwkv_w0, rwkv_w2, rwkv_a0, rwkv_a2, rwkv_g2, rwkv_k_k, rwkv_k_a, rwkv_r_k, rwkv_ln_w, rwkv_ln_b, rwkv_w_o, rwkv_v1, rwkv_v_mu, rwkv_v0, rwkv_v2, conv_w, conv_w_o, w_out, mlp_norm, w_up, w_down):
    weights = (attn_norm, w_in, mla_q_a_norm, mla_wq_b, mla_kv_a_norm, mla_wkv_b, mla_q_norm, mla_k_norm, mla_w_o,
               rwkv_mu, rwkv_w0, rwkv_w2, rwkv_a0, rwkv_a2, rwkv_g2, rwkv_k_k, rwkv_k_a, rwkv_r_k, rwkv_ln_w,
               rwkv_ln_b, rwkv_w_o, rwkv_v1, rwkv_v_mu, rwkv_v0, rwkv_v2, conv_w, conv_w_o, w_out, mlp_norm,
               w_up, w_down)
    b, s, d = x.shape
    cosf, sinf = _rope_tables(positions)
    v_first = None
    for l in range(DEPTH):
        p = _layer_params(l, *weights)
        gates, mla_cols, rwkv_cols, conv_cols = _in_proj(
            x.reshape(b * s, d), p["attn_norm"], p["w_gate"], p["w_mla"], p["w_rwkv"], p["w_conv"])
        q, k, v = _mla_prep(mla_cols.reshape(b, s, -1), cosf, sinf, p)
        att = _attention(q, k, v)
        prep = _rwkv_prep(rwkv_cols.reshape(b, s, -1), v_first, p)
        scan_xt, scan_val, chunk_decay, g_, bonus = prep[:5]
        if l == 0:
            v_first = prep[5]
        dend = chunk_decay.reshape(-1, b, SCAN_SUB, RWKV_HEADS, RWKV_HEAD_DIM).transpose(0, 2, 4, 1, 3)
        dend = jnp.tile(dend.reshape(-1, RWKV_HEAD_DIM, b * RWKV_HEADS), (1, 1, SCAN_VQ))
        y = _wkv_scan(scan_xt, scan_val, dend, b)
        x = _merge(x, gates.reshape(b, s, -1), att, y, g_, bonus, conv_cols.reshape(b, s, -1), p)
        x = _mlp(x.reshape(b * s, d), p["mlp_norm"], p["w_up"], p["w_down"]).reshape(b, s, d)
    return x
```

```python
import functools

import jax
import jax.numpy as jnp
import numpy as np
from jax import lax
from jax.experimental import pallas as pl
from jax.experimental.pallas import tpu as pltpu

D_MODEL = 1024
DEPTH = 2
MLA_HEADS = 8
QK_NOPE_DIM = 64
QK_ROPE_DIM = 32
QK_HEAD_DIM = QK_NOPE_DIM + QK_ROPE_DIM
V_HEAD_DIM = 64
Q_LORA_RANK = 384
KV_LORA_RANK = 256
ROPE_THETA = 10000.0
RWKV_HEAD_DIM = 64
RWKV_HEADS = 4
RWKV_WIDTH = RWKV_HEADS * RWKV_HEAD_DIM
DECAY_LORA = 64
AAA_LORA = 64
GATE_LORA = 128
MV_LORA = 32
GN_EPS = 64e-5
CONV_WIDTH = 256
CONV_K = 3
D_FF = 4 * D_MODEL
N_BRANCH = 3
NORM_EPS = 1e-6
GATE_COLS = N_BRANCH * D_MODEL
MLA_COLS = Q_LORA_RANK + KV_LORA_RANK + QK_ROPE_DIM
RWKV_COLS = 3 * RWKV_WIDTH + DECAY_LORA + AAA_LORA + GATE_LORA

LANES = 128
HEAD_PAD = LANES
VT_ROWS = V_HEAD_DIM + 16
MLA_OUT_COLS = Q_LORA_RANK + KV_LORA_RANK + 2 * LANES
VMEM_LIMIT = 56 * 1024 * 1024

F32 = jnp.float32
BF16 = jnp.bfloat16


def _cparams(sem):
    return pltpu.CompilerParams(dimension_semantics=sem, vmem_limit_bytes=VMEM_LIMIT)


def _const_spec(shape):
    nd = len(shape)
    return pl.BlockSpec(shape, lambda *_: (0,) * nd, pipeline_mode=pl.Buffered(1))


def _bdot(a, b):
    return jnp.dot(a.astype(BF16), b.astype(BF16), preferred_element_type=F32)


def _seg_sum(x, e):
    return sum(jnp.dot(part, e, preferred_element_type=F32) for part in _split_bf16(x, 2))


def _split_bf16(x, parts):
    terms = []
    for _ in range(parts - 1):
        term = x.astype(BF16)
        terms.append(term)
        x = x - term.astype(F32)
    return terms + [x.astype(BF16)]


def _in_proj_body(x_ref, g_ref, wg_ref, wm_ref, wr_ref, wc_ref, cos_ref, sin_ref, qan_ref, kvan_ref,
                  wq_ref, wqr_ref, wk_ref, wv_ref, gq_ref, gqr_ref, gk_ref, gkr_ref,
                  gate_ref, rwkv_ref, conv_ref, q_ref, k_ref, v_ref):
    x = x_ref[...]
    ms = jnp.mean(x * x, axis=-1, keepdims=True)
    h = (x * lax.rsqrt(ms + NORM_EPS) * g_ref[...]).astype(BF16)
    mla_cols = jnp.dot(h, wm_ref[...], preferred_element_type=F32)
    _mla_heads(mla_cols, cos_ref, sin_ref, qan_ref, kvan_ref, wq_ref, wqr_ref, wk_ref, wv_ref,
               gq_ref, gqr_ref, gk_ref, gkr_ref, q_ref, k_ref, v_ref)
    gate_ref[...] = jax.nn.sigmoid(jnp.dot(h, wg_ref[...], preferred_element_type=F32)).astype(gate_ref.dtype)
    rwkv_ref[...] = jnp.dot(h, wr_ref[...], preferred_element_type=F32)
    conv_ref[...] = jnp.dot(h, wc_ref[...], preferred_element_type=F32)


def _in_proj(x, gain, wg, wm, wr, wc, cosf, sinf, p):
    b, s, d = x.shape
    tm = ATTN_BLOCK
    per_seq = s // tm
    t = b * s
    row = lambda n: pl.BlockSpec((tm, n), lambda i: (i, 0))
    tok = pl.BlockSpec((1, tm, LANES), lambda i: (i // per_seq, i % per_seq, 0))
    head = lambda w: pl.BlockSpec((1, MLA_HEADS, tm, w), lambda i: (i // per_seq, 0, i % per_seq, 0))
    mla_consts = (p["qan"], p["kvan"], p["wq"], p["wqr"], p["wk"], p["wv"], p["gq"], p["gqr"], p["gk"], p["gkr"])
    widths = (wg.shape[1], wr.shape[1], wc.shape[1])
    return pl.pallas_call(
        _in_proj_body,
        grid=(t // tm,),
        in_specs=[row(d), _const_spec((1, d))] + [_const_spec(w.shape) for w in (wg, wm, wr, wc)]
        + [tok, tok] + [_const_spec(a.shape) for a in mla_consts],
        out_specs=[row(n) for n in widths]
        + [head(HEAD_PAD), head(HEAD_PAD),
           pl.BlockSpec((1, MLA_HEADS, 1, VT_ROWS, tm), lambda i: (i // per_seq, 0, i % per_seq, 0, 0))],
        out_shape=[jax.ShapeDtypeStruct((t, widths[0]), BF16)]
        + [jax.ShapeDtypeStruct((t, n), F32) for n in widths[1:]]
        + [jax.ShapeDtypeStruct((b, MLA_HEADS, s, HEAD_PAD), BF16),
           jax.ShapeDtypeStruct((b, MLA_HEADS, s, HEAD_PAD), BF16),
           jax.ShapeDtypeStruct((b, MLA_HEADS, per_seq, VT_ROWS, tm), BF16)],
        compiler_params=_cparams(("parallel",)),
        name="in_proj",
    )(x.reshape(t, d), gain, wg, wm, wr, wc, cosf, sinf, *mla_consts)


def _mla_heads(c, cos_ref, sin_ref, qan_ref, kvan_ref, wq_ref, wqr_ref, wk_ref, wv_ref,
               gq_ref, gqr_ref, gk_ref, gkr_ref, q_ref, k_ref, v_ref):
    cq = c[:, :Q_LORA_RANK]
    ckv = c[:, Q_LORA_RANK:Q_LORA_RANK + KV_LORA_RANK]
    kpe = c[:, Q_LORA_RANK + KV_LORA_RANK:Q_LORA_RANK + KV_LORA_RANK + LANES]
    kper = c[:, Q_LORA_RANK + KV_LORA_RANK + LANES:]

    def rms(z, g):
        return (z * lax.rsqrt(jnp.mean(z * z, axis=-1, keepdims=True) + NORM_EPS) * g).astype(BF16)

    cqn = rms(cq, qan_ref[...])
    ckvn = rms(ckv, kvan_ref[...])
    q = jnp.dot(cqn, wq_ref[...], preferred_element_type=F32)
    qr = jnp.dot(cqn, wqr_ref[...], preferred_element_type=F32)
    kn = jnp.dot(ckvn, wk_ref[...], preferred_element_type=F32)
    v = jnp.dot(ckvn, wv_ref[...], preferred_element_type=F32)
    cosf = cos_ref[0]
    sinf = sin_ref[0]
    gqc = gq_ref[...] * cosf
    gqs = gqr_ref[...] * sinf
    gkc = gk_ref[...] * cosf
    kper_s = kper * gkr_ref[...] * sinf
    inv_dim = 1.0 / QK_HEAD_DIM
    vt = v.T
    pad_row = lax.broadcasted_iota(jnp.int32, (VT_ROWS - V_HEAD_DIM, vt.shape[1]), 0)
    ones_rows = jnp.where(pad_row == 0, 1.0, 0.0)
    for h in range(MLA_HEADS):
        sl = slice(h * HEAD_PAD, (h + 1) * HEAD_PAD)
        qh = q[:, sl]
        rq = lax.rsqrt(jnp.sum(qh * qh, axis=-1, keepdims=True) * inv_dim + NORM_EPS)
        q_ref[0, h] = (rq * (qh * gqc + qr[:, sl] * gqs)).astype(q_ref.dtype)
        kh = kn[:, sl] + kpe
        rk = lax.rsqrt(jnp.sum(kh * kh, axis=-1, keepdims=True) * inv_dim + NORM_EPS)
        k_ref[0, h] = (rk * (kh * gkc + kper_s)).astype(k_ref.dtype)
        v_ref[0, h, 0] = jnp.concatenate([vt[h * V_HEAD_DIM:(h + 1) * V_HEAD_DIM, :], ones_rows],
                                         axis=0).astype(v_ref.dtype)


MASK_VALUE = -1e30


ATTN_BLOCK = 256
ATTN_LOOKAHEAD = 4


def _attn_body(q_ref, k_ref, vt_ref, o_ref, m_ref, acc_ref, pend_ref, *, tq):
    i = pl.program_id(1)
    m_ref[...] = jnp.full(m_ref.shape, MASK_VALUE, F32)
    acc_ref[...] = jnp.zeros(acc_ref.shape, F32)
    key_idx = lax.broadcasted_iota(jnp.int32, (tq, tq), 0)
    qry_idx = lax.broadcasted_iota(jnp.int32, (tq, tq), 1)
    causal = key_idx <= qry_idx

    def scores_t(j, h):
        kj = k_ref[0, h, pl.ds(pl.multiple_of(j * tq, tq), tq), :]
        return lax.dot_general(kj, q_ref[0, h], (((1,), (1,)), ((), ())), preferred_element_type=F32)

    for h in range(ATTN_LOOKAHEAD):
        pend_ref[h] = scores_t(0, h)

    def block(j, carry):
        j_next = jnp.minimum(j + 1, i)
        visible = jnp.logical_or(causal, j < i)
        pending = [pend_ref[h] for h in range(ATTN_LOOKAHEAD)]
        for h in range(MLA_HEADS):
            st = pending.pop(0)
            ahead = h + ATTN_LOOKAHEAD
            if ahead < MLA_HEADS:
                pending.append(scores_t(j, ahead))
            else:
                pend_ref[ahead - MLA_HEADS] = scores_t(j_next, ahead - MLA_HEADS)
            st = jnp.where(visible, st, MASK_VALUE)
            m_prev = m_ref[h]
            m_new = jnp.maximum(m_prev, jnp.max(st, axis=0, keepdims=True))
            alpha = jnp.exp2(m_prev - m_new)
            pt = jnp.exp2(st - m_new)
            acc_ref[h] = alpha * acc_ref[h] + jnp.dot(vt_ref[0, h, j], pt.astype(BF16), preferred_element_type=F32)
            m_ref[h] = m_new
        return carry

    lax.fori_loop(0, i + 1, block, 0)
    for h in range(MLA_HEADS):
        out_t = acc_ref[h, 0:V_HEAD_DIM, :] / acc_ref[h, V_HEAD_DIM:V_HEAD_DIM + 1, :]
        o_ref[0, :, h * V_HEAD_DIM:(h + 1) * V_HEAD_DIM] = out_t.T.astype(o_ref.dtype)


def _attention(q, k, vt):
    b, nh, s, dp = q.shape
    tq = ATTN_BLOCK
    return pl.pallas_call(
        functools.partial(_attn_body, tq=tq),
        grid=(b, s // tq),
        in_specs=[pl.BlockSpec((1, nh, tq, dp), lambda bi, i: (bi, 0, i, 0)),
                  pl.BlockSpec((1, nh, s, dp), lambda bi, i: (bi, 0, 0, 0)),
                  pl.BlockSpec((1, nh, s // tq, VT_ROWS, tq), lambda bi, i: (bi, 0, 0, 0, 0))],
        out_specs=pl.BlockSpec((1, tq, nh * V_HEAD_DIM), lambda bi, i: (bi, i, 0)),
        out_shape=jax.ShapeDtypeStruct((b, s, nh * V_HEAD_DIM), BF16),
        scratch_shapes=[pltpu.VMEM((nh, 1, tq), F32), pltpu.VMEM((nh, VT_ROWS, tq), F32),
                        pltpu.VMEM((ATTN_LOOKAHEAD, tq, tq), F32)],
        compiler_params=_cparams(("parallel", "arbitrary")),
        name="mla_attention",
    )(q, k, vt)


SCAN_VR = 16
SCAN_VQ = RWKV_HEAD_DIM // SCAN_VR
SCAN_TENSORS = 4
SCAN_TC = 64
PREP_TS = LANES


def _rwkv_prep_body(*refs, has_vres, ts, nb):
    if has_vres:
        (x_ref, xp_ref, vf_ref, mu_ref, w0_ref, w2_ref, a0_ref, a2_ref, g2_ref, kk_ref, ka_ref, rk_ref, e_ref, tri_ref,
         v0_ref, v2_ref, xt_ref, val_o, dend_o, g_o, bonus_o, vt_ref) = refs
    else:
        (x_ref, xp_ref, mu_ref, w0_ref, w2_ref, a0_ref, a2_ref, g2_ref, kk_ref, ka_ref, rk_ref, e_ref, tri_ref,
         xt_ref, val_o, dend_o, g_o, bonus_o, vfirst_o, vt_ref) = refs
    i = pl.program_id(0)
    bh = nb * RWKV_HEADS

    def per_batch(b, carry):
        x = x_ref[b]
        prev = jnp.where(i > 0, xp_ref[b][7:8, :], 0.0)
        row = lax.broadcasted_iota(jnp.int32, (ts, 1), 0)
        shifted = jnp.where(row == 0, prev, pltpu.roll(x, 1, axis=0))
        xs = x + (shifted - x) * mu_ref[...]
        wd = RWKV_WIDTH
        r = xs[:, 0:wd]
        k = xs[:, wd:2 * wd]
        v = xs[:, 2 * wd:3 * wd]
        lora_in = xs[:, 3 * wd:3 * wd + LANES]
        xg = xs[:, 3 * wd + LANES:3 * wd + 2 * LANES]
        e = e_ref[...]
        zw = w0_ref[...] + _bdot(jnp.tanh(lora_in), w2_ref[...])
        nz = -zw
        softplus = jnp.maximum(nz, 0.0) + jnp.log(1.0 + jnp.exp(-jnp.abs(nz)))
        log_decay = -jnp.exp(-softplus - 0.5)
        tri = tri_ref[...]
        log_d = sum(jnp.dot(tri, part, preferred_element_type=F32) for part in _split_bf16(log_decay, 3))
        d_incl = jnp.exp(log_d)
        d_prev = jnp.exp(log_d - log_decay)
        d_inv = jnp.exp(-log_d)
        ends = [d_incl[(c + 1) * SCAN_TC - 1:(c + 1) * SCAN_TC, :] for c in range(ts // SCAN_TC)]
        dend_o[0, pl.ds(b, 1), :] = jnp.concatenate(ends, axis=1)
        a_lr = jax.nn.sigmoid(a0_ref[...] + _bdot(lora_in, a2_ref[...]))
        g_o[b] = _bdot(jax.nn.sigmoid(xg), g2_ref[...])
        if has_vres:
            xvs = xs[:, RWKV_COLS:RWKV_COLS + LANES]
            v = v + (vf_ref[b] - v) * jax.nn.sigmoid(v0_ref[...] + _bdot(xvs, v2_ref[...]))
        else:
            vfirst_o[b] = v
        kk = k * kk_ref[...]
        norm = jnp.sqrt(_seg_sum(kk * kk, e))
        kk = kk / jnp.maximum(norm, 1e-12)
        k = k * (1.0 + (a_lr - 1.0) * ka_ref[...])
        bonus_o[b] = _seg_sum(r * k * rk_ref[...], e) * v
        rows = pl.ds(pl.multiple_of(b * wd, wd), wd)
        for idx, val in enumerate((-kk * d_prev, kk * a_lr * d_inv, k * d_inv, r * d_incl)):
            xt_ref[idx, rows, :] = val.T
        vt_ref[rows, :] = v.T
        return carry

    lax.fori_loop(0, nb, per_batch, 0)
    for vr in range(SCAN_VR):
        slab = jnp.concatenate(
            [vt_ref[pl.ds(vq * SCAN_VR + vr, bh, stride=RWKV_HEAD_DIM), :] for vq in range(SCAN_VQ)], axis=0)
        val_o[pl.ds(vr, ts, stride=SCAN_VR), :] = slab.T


def _rwkv_prep(rwkv_cols, v_first, p):
    b, s, n = rwkv_cols.shape
    ts = PREP_TS
    has_vres = v_first is not None
    tok = lambda w: pl.BlockSpec((b, ts, w), lambda i: (0, i, 0))
    halo = pl.BlockSpec((b, 8, n), lambda i: (0, jnp.maximum(i * (ts // 8) - 1, 0), 0))
    step = np.arange(ts)
    same_chunk = (step[:, None] // SCAN_TC) == (step[None, :] // SCAN_TC)
    chunk_tri = jnp.asarray(same_chunk & (step[None, :] <= step[:, None]), BF16)
    consts = [p["mu"], p["w0"], p["w2"], p["a0"], p["a2"], p["g2"], p["k_k"], p["k_a"], p["r_k"], p["seg"], chunk_tri]
    args = [rwkv_cols, rwkv_cols]
    in_specs = [tok(n), halo]
    if has_vres:
        args.append(v_first)
        in_specs.append(tok(RWKV_WIDTH))
        consts += [p["v0"], p["v2"]]
    in_specs += [_const_spec(a.shape) for a in consts]
    n_tok_out = 2 if has_vres else 3
    return pl.pallas_call(
        functools.partial(_rwkv_prep_body, has_vres=has_vres, ts=ts, nb=b),
        grid=(s // ts,),
        in_specs=in_specs,
        out_specs=[pl.BlockSpec((SCAN_TENSORS, b * RWKV_WIDTH, ts), lambda i: (0, 0, i)),
                   pl.BlockSpec((ts * SCAN_VR, LANES), lambda i: (i, 0)),
                   pl.BlockSpec((1, b, SCAN_SUB * RWKV_WIDTH), lambda i: (i, 0, 0))] + [tok(RWKV_WIDTH)] * n_tok_out,
        out_shape=[jax.ShapeDtypeStruct((SCAN_TENSORS, b * RWKV_WIDTH, s), F32),
                   jax.ShapeDtypeStruct((s * SCAN_VR, LANES), F32),
                   jax.ShapeDtypeStruct((s // ts, b, SCAN_SUB * RWKV_WIDTH), F32)]
        + [jax.ShapeDtypeStruct((b, s, RWKV_WIDTH), F32)] * n_tok_out,
        scratch_shapes=[pltpu.VMEM((b * RWKV_WIDTH, ts), F32)],
        compiler_params=_cparams(("parallel",)),
        name="rwkv_prep",
    )(*args, *consts)


N_ACC = 4
SCAN_SUB = PREP_TS // SCAN_TC


SCAN_UNROLL = 8
SLABS_PER_TILE = SCAN_TENSORS * RWKV_HEAD_DIM
SLABS_PER_BODY = SLABS_PER_TILE * SCAN_UNROLL // PREP_TS
BODIES_PER_TENSOR = RWKV_HEAD_DIM // SLABS_PER_BODY


def _scan_body(xt_ref, v_ref, dprev_ref, dcur_ref, y_ref, s_ref, stage_ref, ych_ref, yt_ref, *, nb):
    g = pl.program_id(0)
    bh = nb * RWKV_HEADS
    wslot = g % 2
    rslot = 1 - wslot

    def stage_slabs(body_idx):
        tensor = body_idx // BODIES_PER_TENSOR
        key0 = (body_idx % BODIES_PER_TENSOR) * SLABS_PER_BODY
        for j in range(SLABS_PER_BODY):
            rows = xt_ref[tensor, pl.ds(key0 + j, bh, stride=RWKV_HEAD_DIM), :]
            stage_ref[wslot, tensor, key0 + j] = jnp.concatenate([rows] * SCAN_VQ, axis=0).T

    @pl.when(g == 0)
    def _():
        s_ref[...] = jnp.zeros_like(s_ref)

        def warm(body_idx, carry):
            stage_slabs(body_idx)
            return carry

        lax.fori_loop(0, SLABS_PER_TILE // SLABS_PER_BODY, warm, 0)

    def tree(acc):
        return (acc[0] + acc[1]) + (acc[2] + acc[3])

    def accumulate(acc, kk, term):
        acc[kk % N_ACC] = term if acc[kk % N_ACC] is None else acc[kk % N_ACC] + term

    def scan_chunk(chunk, dend_ref):
        t0 = chunk * SCAN_TC

        def operand(idx, kk, t):
            return stage_ref[rslot, idx, kk, pl.ds(t, 1), :]

        acc = [None] * N_ACC
        for kk in range(RWKV_HEAD_DIM):
            s_start = s_ref[kk] * dend_ref[0, pl.ds(kk, 1), :]
            s_ref[kk] = s_start
            accumulate(acc, kk, s_start * operand(0, kk, t0))

        def step(t, u):
            op = lambda idx, kk: operand(idx, kk, t)
            t_next = jnp.minimum(t + 1, t0 + SCAN_TC - 1)
            v = v_ref[pl.ds(pl.multiple_of(t * SCAN_VR, SCAN_VR), SCAN_VR), :]
            yacc = [None] * N_ACC
            uacc = [None] * N_ACC
            for kk in range(RWKV_HEAD_DIM):
                s_new = s_ref[kk] + (u * op(1, kk) + v * op(2, kk))
                s_ref[kk] = s_new
                accumulate(yacc, kk, s_new * op(3, kk))
                accumulate(uacc, kk, s_new * operand(0, kk, t_next))
            ych_ref[pl.ds(pl.multiple_of(t * SCAN_VR, SCAN_VR), SCAN_VR), :] = tree(yacc)
            return tree(uacc)

        def body(body_idx, u):
            stage_slabs(chunk * (SCAN_TC // SCAN_UNROLL) + body_idx)
            for j in range(SCAN_UNROLL):
                u = step(t0 + body_idx * SCAN_UNROLL + j, u)
            return u

        lax.fori_loop(0, SCAN_TC // SCAN_UNROLL, body, tree(acc))

    @pl.when(g > 0)
    def _():
        scan_chunk(0, dprev_ref)
        scan_chunk(1, dcur_ref)
        for vr in range(SCAN_VR):
            lanes_by_t = ych_ref[pl.ds(vr, PREP_TS, stride=SCAN_VR), :].T
            for vq in range(SCAN_VQ):
                yt_ref[pl.ds(vq * SCAN_VR + vr, bh, stride=RWKV_HEAD_DIM), :] = lanes_by_t[vq * bh:(vq + 1) * bh, :]
        for b in range(nb):
            y_ref[b] = yt_ref[b * RWKV_WIDTH:(b + 1) * RWKV_WIDTH, :].T


def _wkv_scan(xt, val, dend, nb):
    assert SCAN_SUB == 2
    s = xt.shape[-1]
    n_tiles = s // PREP_TS
    prev_tile = lambda g: jnp.maximum(g - 1, 0)
    return pl.pallas_call(
        functools.partial(_scan_body, nb=nb),
        grid=(n_tiles + 1,),
        in_specs=[
            pl.BlockSpec((SCAN_TENSORS, nb * RWKV_WIDTH, PREP_TS), lambda g: (0, 0, jnp.minimum(g, n_tiles - 1))),
            pl.BlockSpec((PREP_TS * SCAN_VR, LANES), lambda g: (prev_tile(g), 0)),
            pl.BlockSpec((1, RWKV_HEAD_DIM, LANES), lambda g: (jnp.maximum(prev_tile(g) * SCAN_SUB - 1, 0), 0, 0)),
            pl.BlockSpec((1, RWKV_HEAD_DIM, LANES), lambda g: (prev_tile(g) * SCAN_SUB, 0, 0))],
        out_specs=pl.BlockSpec((nb, PREP_TS, RWKV_WIDTH), lambda g: (0, prev_tile(g), 0)),
        out_shape=jax.ShapeDtypeStruct((nb, s, RWKV_WIDTH), F32),
        scratch_shapes=[pltpu.VMEM((RWKV_HEAD_DIM, SCAN_VR, LANES), F32),
                        pltpu.VMEM((2, SCAN_TENSORS, RWKV_HEAD_DIM, PREP_TS, LANES), F32),
                        pltpu.VMEM((PREP_TS * SCAN_VR, LANES), F32),
                        pltpu.VMEM((nb * RWKV_WIDTH, PREP_TS), F32)],
        compiler_params=_cparams(("arbitrary",)),
        name="wkv_scan",
    )(xt, val, dend, dend)


def _merge_body(x_ref, gate_ref, att_ref, y_ref, g_ref, bonus_ref, conv_ref, convp_ref,
                lnw_ref, lnb_ref, e_ref, cw_ref, wa_ref, wb_ref, wc_ref, wo_ref, out_ref, *, ts):
    i = pl.program_id(1)
    e = e_ref[...]
    y = y_ref[0]
    inv_n = 1.0 / RWKV_HEAD_DIM
    mean = _seg_sum(y, e) * inv_n
    d = y - mean
    var = _seg_sum(d * d, e) * inv_n
    yn = d * lax.rsqrt(var + GN_EPS) * lnw_ref[...] + lnb_ref[...] + bonus_ref[0]
    ob = _bdot(yn * g_ref[0], wb_ref[...])

    cw = CONV_WIDTH
    c = conv_ref[0]
    u = c[:, cw:2 * cw] * c[:, 2 * cw:3 * cw]
    cp = convp_ref[0]
    up = jnp.where(i > 0, cp[:, cw:2 * cw] * cp[:, 2 * cw:3 * cw], 0.0)
    p6 = up[6:7, :]
    p7 = up[7:8, :]
    row = lax.broadcasted_iota(jnp.int32, (ts, 1), 0)
    u1 = jnp.where(row == 0, p7, pltpu.roll(u, 1, axis=0))
    u2 = jnp.where(row == 0, p6, jnp.where(row == 1, p7, pltpu.roll(u, 2, axis=0)))
    taps = cw_ref[...]
    yc = taps[0:1, :] * u2 + taps[1:2, :] * u1 + taps[2:3, :] * u
    oc = _bdot(c[:, 0:cw] * yc, wc_ref[...])

    oa = jnp.dot(att_ref[0], wa_ref[...], preferred_element_type=F32)
    gates = gate_ref[0].astype(F32)
    dm = D_MODEL
    merged = gates[:, 0:dm] * oa + gates[:, dm:2 * dm] * ob + gates[:, 2 * dm:3 * dm] * oc
    out_ref[0] = x_ref[0] + _bdot(merged, wo_ref[...])


def _merge(x, gates, att, y, g, bonus, conv_cols, p, ts=512):
    b, s, d = x.shape
    tok = lambda w: pl.BlockSpec((1, ts, w), lambda bi, i: (bi, i, 0))
    nconv = conv_cols.shape[-1]
    halo = pl.BlockSpec((1, 8, nconv), lambda bi, i: (bi, jnp.maximum(i * (ts // 8) - 1, 0), 0))
    consts = (p["ln_w"], p["ln_b"], p["seg"], p["conv_w"], p["mla_w_o"], p["rwkv_w_o"], p["conv_w_o"], p["w_out"])
    return pl.pallas_call(
        functools.partial(_merge_body, ts=ts),
        grid=(b, s // ts),
        in_specs=[tok(d), tok(GATE_COLS), tok(att.shape[-1]), tok(RWKV_WIDTH), tok(RWKV_WIDTH), tok(RWKV_WIDTH),
                  tok(nconv), halo] + [_const_spec(a.shape) for a in consts],
        out_specs=tok(d),
        out_shape=jax.ShapeDtypeStruct((b, s, d), F32),
        compiler_params=_cparams(("parallel", "parallel")),
        name="branch_merge",
    )(x, gates, att, y, g, bonus, conv_cols, conv_cols, *consts)


def _mlp_body(x_ref, g_ref, wu_ref, wd_ref, o_ref):
    x = x_ref[...]
    ms = jnp.mean(x * x, axis=-1, keepdims=True)
    h = (x * lax.rsqrt(ms + NORM_EPS) * g_ref[...]).astype(BF16)
    up = jnp.dot(h, wu_ref[...], preferred_element_type=F32)
    act = jnp.square(jnp.maximum(up, 0.0)).astype(BF16)
    o_ref[...] = x + jnp.dot(act, wd_ref[...], preferred_element_type=F32)


def _mlp(x2d, gain, w_up, w_down, tm=256):
    t, d = x2d.shape
    row = pl.BlockSpec((tm, d), lambda i: (i, 0))
    return pl.pallas_call(
        _mlp_body,
        grid=(t // tm,),
        in_specs=[row, _const_spec((1, d)), _const_spec(w_up.shape), _const_spec(w_down.shape)],
        out_specs=row,
        out_shape=jax.ShapeDtypeStruct((t, d), F32),
        compiler_params=_cparams(("parallel",)),
        name="mlp",
    )(x2d, gain, w_up, w_down)


def _rope_partner_cols(w):
    half = QK_ROPE_DIM // 2
    return jnp.concatenate([-w[..., half:], w[..., :half]], axis=-1)


def _pad_lanes(w, lo, total=HEAD_PAD):
    n = w.shape[-1]
    pad = [(0, 0)] * (w.ndim - 1) + [(lo, total - lo - n)]
    return jnp.pad(w, pad)


def _layer_params(l, attn_norm, w_in, mla_q_a_norm, mla_wq_b, mla_kv_a_norm, mla_wkv_b, mla_q_norm, mla_k_norm,
                  mla_w_o, rwkv_mu, rwkv_w0, rwkv_w2, rwkv_a0, rwkv_a2, rwkv_g2, rwkv_k_k, rwkv_k_a, rwkv_r_k,
                  rwkv_ln_w, rwkv_ln_b, rwkv_w_o, rwkv_v1, rwkv_v_mu, rwkv_v0, rwkv_v2, conv_w, conv_w_o, w_out,
                  mlp_norm, w_up, w_down):
    p = {}
    row = lambda a: a.reshape(1, -1).astype(F32)
    w = w_in[l]
    o_mla = GATE_COLS
    o_rwkv = o_mla + MLA_COLS
    o_conv = o_rwkv + RWKV_COLS
    p["attn_norm"] = row(attn_norm[l])
    p["w_gate"] = w[:, :GATE_COLS].astype(BF16)
    w_kpe = w[:, o_mla + Q_LORA_RANK + KV_LORA_RANK:o_rwkv]
    p["w_mla"] = jnp.concatenate(
        [w[:, o_mla:o_mla + Q_LORA_RANK + KV_LORA_RANK], _pad_lanes(w_kpe, QK_NOPE_DIM),
         _pad_lanes(_rope_partner_cols(w_kpe), QK_NOPE_DIM)], axis=1).astype(BF16)
    w_rwkv = w[:, o_rwkv:o_conv]
    mu = rwkv_mu[l]
    if l > 0:
        w_rwkv = jnp.concatenate([w_rwkv, _pad_lanes(rwkv_v1[l - 1], 0)], axis=1)
        mu = jnp.concatenate([mu, _pad_lanes(rwkv_v_mu[l - 1], 0)])
        p["v0"] = row(rwkv_v0[l - 1])
        p["v2"] = jnp.pad(rwkv_v2[l - 1], ((0, LANES - MV_LORA), (0, 0)))
    p["w_rwkv"] = w_rwkv.astype(BF16)
    p["mu"] = row(mu)
    p["w_conv"] = w[:, o_conv:].astype(BF16)

    scale = QK_HEAD_DIM ** -0.5 * float(np.log2(np.e))
    wq = mla_wq_b[l].reshape(Q_LORA_RANK, MLA_HEADS, QK_HEAD_DIM)
    p["wq"] = _pad_lanes(wq, 0).reshape(Q_LORA_RANK, -1).astype(BF16)
    p["wqr"] = _pad_lanes(_rope_partner_cols(wq[..., QK_NOPE_DIM:]), QK_NOPE_DIM).reshape(Q_LORA_RANK, -1).astype(BF16)
    wkv = mla_wkv_b[l].reshape(KV_LORA_RANK, MLA_HEADS, QK_NOPE_DIM + V_HEAD_DIM)
    p["wk"] = _pad_lanes(wkv[..., :QK_NOPE_DIM], 0).reshape(KV_LORA_RANK, -1).astype(BF16)
    p["wv"] = wkv[..., QK_NOPE_DIM:].reshape(KV_LORA_RANK, -1).astype(BF16)
    p["qan"] = row(mla_q_a_norm[l])
    p["kvan"] = row(mla_kv_a_norm[l])
    swap = lambda g: jnp.concatenate([g[QK_ROPE_DIM // 2:], g[:QK_ROPE_DIM // 2]])
    gq, gk = mla_q_norm[l] * scale, mla_k_norm[l]
    p["gq"] = row(_pad_lanes(gq, 0))
    p["gqr"] = row(_pad_lanes(swap(gq[QK_NOPE_DIM:]), QK_NOPE_DIM))
    p["gk"] = row(_pad_lanes(gk, 0))
    p["gkr"] = row(_pad_lanes(swap(gk[QK_NOPE_DIM:]), QK_NOPE_DIM))
    p["mla_w_o"] = mla_w_o[l].astype(BF16)

    p["w0"] = row(rwkv_w0[l])
    p["w2"] = jnp.pad(rwkv_w2[l], ((0, AAA_LORA), (0, 0)))
    p["a0"] = row(rwkv_a0[l])
    p["a2"] = jnp.pad(rwkv_a2[l], ((DECAY_LORA, 0), (0, 0)))
    p["g2"] = rwkv_g2[l]
    p["k_k"] = row(rwkv_k_k[l])
    p["k_a"] = row(rwkv_k_a[l])
    p["r_k"] = row(rwkv_r_k[l])
    p["ln_w"] = row(rwkv_ln_w[l])
    p["ln_b"] = row(rwkv_ln_b[l])
    head_of = np.arange(RWKV_WIDTH) // RWKV_HEAD_DIM
    p["seg"] = jnp.asarray(head_of[:, None] == head_of[None, :], BF16)
    p["rwkv_w_o"] = rwkv_w_o[l].astype(BF16)
    p["conv_w"] = conv_w[l].astype(F32)
    p["conv_w_o"] = conv_w_o[l].astype(BF16)
    p["w_out"] = w_out[l].astype(BF16)
    p["mlp_norm"] = row(mlp_norm[l])
    p["w_up"] = w_up[l].astype(BF16)
    p["w_down"] = w_down[l].astype(BF16)
    return p


def _rope_tables(positions):
    half = QK_ROPE_DIM // 2
    freqs = ROPE_THETA ** (-(jnp.arange(half, dtype=F32) * 2.0 / QK_ROPE_DIM))
    ang = positions.astype(F32)[..., None] * freqs
    cos, sin = lax.optimization_barrier((jnp.cos(ang), jnp.sin(ang)))
    ones =jnp.ones(positions.shape + (QK_NOPE_DIM,), F32)
    tail = jnp.ones(positions.shape + (HEAD_PAD - QK_HEAD_DIM,), F32)
    cosf = jnp.concatenate([ones, cos, cos, tail], axis=-1)
    sinf = jnp.concatenate([0 * ones, sin, sin, 0 * tail], axis=-1)
    return cosf, sinf


def kernel(x, positions, attn_norm, w_in, mla_q_a_norm, mla_wq_b, mla_kv_a_norm, mla_wkv_b, mla_q_norm, mla_k_norm, mla_w_o, rwkv_mu, rwkv_w0, rwkv_w2, rwkv_a0, rwkv_a2, rwkv_g2, rwkv_k_k, rwkv_k_a, rwkv_r_k, rwkv_ln_w, rwkv_ln_b, rwkv_w_o, rwkv_v1, rwkv_v_mu, rwkv_v0, rwkv_v2, conv_w, conv_w_o, w_out, mlp_norm, w_up, w_down):
    weights = (attn_norm, w_in, mla_q_a_norm, mla_wq_b, mla_kv_a_norm, mla_wkv_b, mla_q_norm, mla_k_norm, mla_w_o,
               rwkv_mu, rwkv_w0, rwkv_w2, rwkv_a0, rwkv_a2, rwkv_g2, rwkv_k_k, rwkv_k_a, rwkv_r_k, rwkv_ln_w,
               rwkv_ln_b, rwkv_w_o, rwkv_v1, rwkv_v_mu, rwkv_v0, rwkv_v2, conv_w, conv_w_o, w_out, mlp_norm,
               w_up, w_down)
    b, s, d = x.shape
    cosf, sinf = _rope_tables(positions)
    v_first = None
    for l in range(DEPTH):
        p = _layer_params(l, *weights)
        gates, rwkv_cols, conv_cols, q, k, v = _in_proj(
            x, p["attn_norm"], p["w_gate"], p["w_mla"], p["w_rwkv"], p["w_conv"], cosf, sinf, p)
        att = _attention(q, k, v)
        prep = _rwkv_prep(rwkv_cols.reshape(b, s, -1), v_first, p)
        scan_xt, scan_val, chunk_decay, g_, bonus = prep[:5]
        if l == 0:
            v_first = prep[5]
        dend = chunk_decay.reshape(-1, b, SCAN_SUB, RWKV_HEADS, RWKV_HEAD_DIM).transpose(0, 2, 4, 1, 3)
        dend = jnp.tile(dend.reshape(-1, RWKV_HEAD_DIM, b * RWKV_HEADS), (1, 1, SCAN_VQ))
        y = _wkv_scan(scan_xt, scan_val, dend, b)
        x = _merge(x, gates.reshape(b, s, -1), att, y, g_, bonus, conv_cols.reshape(b, s, -1), p)
        x = _mlp(x.reshape(b * s, d), p["mlp_norm"], p["w_up"], p["w_down"]).reshape(b, s, d)
    return x
```

```python
import functools

import jax
import jax.numpy as jnp
import numpy as np
from jax import lax
from jax.experimental import pallas as pl
from jax.experimental.pallas import tpu as pltpu

D_MODEL = 1024
DEPTH = 2
MLA_HEADS = 8
QK_NOPE_DIM = 64
QK_ROPE_DIM = 32
QK_HEAD_DIM = QK_NOPE_DIM + QK_ROPE_DIM
V_HEAD_DIM = 64
Q_LORA_RANK = 384
KV_LORA_RANK = 256
ROPE_THETA = 10000.0
RWKV_HEAD_DIM = 64
RWKV_HEADS = 4
RWKV_WIDTH = RWKV_HEADS * RWKV_HEAD_DIM
DECAY_LORA = 64
AAA_LORA = 64
GATE_LORA = 128
MV_LORA = 32
GN_EPS = 64e-5
CONV_WIDTH = 256
CONV_K = 3
D_FF = 4 * D_MODEL
N_BRANCH = 3
NORM_EPS = 1e-6
GATE_COLS = N_BRANCH * D_MODEL
MLA_COLS = Q_LORA_RANK + KV_LORA_RANK + QK_ROPE_DIM
RWKV_COLS = 3 * RWKV_WIDTH + DECAY_LORA + AAA_LORA + GATE_LORA

LANES = 128
HEAD_PAD = LANES
VT_ROWS = V_HEAD_DIM + 16
MLA_OUT_COLS = Q_LORA_RANK + KV_LORA_RANK + 2 * LANES
VMEM_LIMIT = 56 * 1024 * 1024

F32 = jnp.float32
BF16 = jnp.bfloat16


def _cparams(sem):
    return pltpu.CompilerParams(dimension_semantics=sem, vmem_limit_bytes=VMEM_LIMIT)


def _const_spec(shape):
    nd = len(shape)
    return pl.BlockSpec(shape, lambda *_: (0,) * nd, pipeline_mode=pl.Buffered(1))


def _layer_spec(arr, layer):
    nd = arr.ndim
    return pl.BlockSpec((None,) + arr.shape[1:], lambda *_: (layer,) + (0,) * (nd - 1),
                        pipeline_mode=pl.Buffered(1))


def _bdot(a, b):
    return jnp.dot(a.astype(BF16), b.astype(BF16), preferred_element_type=F32)


def _seg_sum(x, e):
    return sum(jnp.dot(part, e, preferred_element_type=F32) for part in _split_bf16(x, 2))


def _split_bf16(x, parts):
    terms = []
    for _ in range(parts - 1):
        term = x.astype(BF16)
        terms.append(term)
        x = x - term.astype(F32)
    return terms + [x.astype(BF16)]


def _in_proj_body(x_ref, g_ref, wg_ref, wm_ref, wr_ref, wc_ref, cos_ref, sin_ref, qan_ref, kvan_ref,
                  wq_ref, wqr_ref, wk_ref, wv_ref, gq_ref, gqr_ref, gk_ref, gkr_ref,
                  gate_ref, rwkv_ref, conv_ref, q_ref, k_ref, v_ref):
    x = x_ref[...]
    ms = jnp.mean(x * x, axis=-1, keepdims=True)
    h = (x * lax.rsqrt(ms + NORM_EPS) * g_ref[...]).astype(BF16)
    mla_cols = jnp.dot(h, wm_ref[...], preferred_element_type=F32)
    _mla_heads(mla_cols, cos_ref, sin_ref, qan_ref, kvan_ref, wq_ref, wqr_ref, wk_ref, wv_ref,
               gq_ref, gqr_ref, gk_ref, gkr_ref, q_ref, k_ref, v_ref)
    gate_ref[...] = jax.nn.sigmoid(jnp.dot(h, wg_ref[...], preferred_element_type=F32)).astype(gate_ref.dtype)
    rwkv_ref[...] = jnp.dot(h, wr_ref[...], preferred_element_type=F32)
    conv_ref[...] = jnp.dot(h, wc_ref[...], preferred_element_type=F32)


def _in_proj(x, cosf, sinf, p, layer):
    b, s, d = x.shape
    tm = ATTN_BLOCK
    per_seq = s // tm
    t = b * s
    row = lambda n: pl.BlockSpec((tm, n), lambda i: (i, 0))
    tok = pl.BlockSpec((1, tm, LANES), lambda i: (i // per_seq, i % per_seq, 0))
    head = lambda w: pl.BlockSpec((1, MLA_HEADS, tm, w), lambda i: (i // per_seq, 0, i % per_seq, 0))
    gain, wg, wm, wr, wc = p["attn_norm"], p["w_gate"], p["w_mla"], p["w_rwkv"], p["w_conv"]
    mla_consts = (p["qan"], p["kvan"], p["wq"], p["wqr"], p["wk"], p["wv"], p["gq"], p["gqr"], p["gk"], p["gkr"])
    widths = (wg.shape[-1], wr.shape[-1], wc.shape[-1])
    return pl.pallas_call(
        _in_proj_body,
        grid=(t // tm,),
        in_specs=[row(d)] + [_layer_spec(a, layer) for a in (gain, wg, wm, wr, wc)]
        + [tok, tok] + [_layer_spec(a, layer) for a in mla_consts],
        out_specs=[row(n) for n in widths]
        + [head(HEAD_PAD), head(HEAD_PAD),
           pl.BlockSpec((1, MLA_HEADS, 1, VT_ROWS, tm), lambda i: (i // per_seq, 0, i % per_seq, 0, 0))],
        out_shape=[jax.ShapeDtypeStruct((t, widths[0]), BF16)]
        + [jax.ShapeDtypeStruct((t, n), F32) for n in widths[1:]]
        + [jax.ShapeDtypeStruct((b, MLA_HEADS, s, HEAD_PAD), BF16),
           jax.ShapeDtypeStruct((b, MLA_HEADS, s, HEAD_PAD), BF16),
           jax.ShapeDtypeStruct((b, MLA_HEADS, per_seq, VT_ROWS, tm), BF16)],
        compiler_params=_cparams(("parallel",)),
        name="in_proj",
    )(x.reshape(t, d), gain, wg, wm, wr, wc, cosf, sinf, *mla_consts)


def _mla_heads(c, cos_ref, sin_ref, qan_ref, kvan_ref, wq_ref, wqr_ref, wk_ref, wv_ref,
               gq_ref, gqr_ref, gk_ref, gkr_ref, q_ref, k_ref, v_ref):
    cq = c[:, :Q_LORA_RANK]
    ckv = c[:, Q_LORA_RANK:Q_LORA_RANK + KV_LORA_RANK]
    kpe = c[:, Q_LORA_RANK + KV_LORA_RANK:Q_LORA_RANK + KV_LORA_RANK + LANES]
    kper = c[:, Q_LORA_RANK + KV_LORA_RANK + LANES:]

    def rms(z, g):
        return (z * lax.rsqrt(jnp.mean(z * z, axis=-1, keepdims=True) + NORM_EPS) * g).astype(BF16)

    cqn = rms(cq, qan_ref[...])
    ckvn = rms(ckv, kvan_ref[...])
    q = jnp.dot(cqn, wq_ref[...], preferred_element_type=F32)
    qr = jnp.dot(cqn, wqr_ref[...], preferred_element_type=F32)
    kn = jnp.dot(ckvn, wk_ref[...], preferred_element_type=F32)
    v = jnp.dot(ckvn, wv_ref[...], preferred_element_type=F32)
    cosf = cos_ref[0]
    sinf = sin_ref[0]
    gqc = gq_ref[...] * cosf
    gqs = gqr_ref[...] * sinf
    gkc = gk_ref[...] * cosf
    kper_s = kper * gkr_ref[...] * sinf
    inv_dim = 1.0 / QK_HEAD_DIM
    vt = v.T
    pad_row = lax.broadcasted_iota(jnp.int32, (VT_ROWS - V_HEAD_DIM, vt.shape[1]), 0)
    ones_rows = jnp.where(pad_row == 0, 1.0, 0.0)
    for h in range(MLA_HEADS):
        sl = slice(h * HEAD_PAD, (h + 1) * HEAD_PAD)
        qh = q[:, sl]
        rq = lax.rsqrt(jnp.sum(qh * qh, axis=-1, keepdims=True) * inv_dim + NORM_EPS)
        q_ref[0, h] = (rq * (qh * gqc + qr[:, sl] * gqs)).astype(q_ref.dtype)
        kh = kn[:, sl] + kpe
        rk = lax.rsqrt(jnp.sum(kh * kh, axis=-1, keepdims=True) * inv_dim + NORM_EPS)
        k_ref[0, h] = (rk * (kh * gkc + kper_s)).astype(k_ref.dtype)
        v_ref[0, h, 0] = jnp.concatenate([vt[h * V_HEAD_DIM:(h + 1) * V_HEAD_DIM, :], ones_rows],
                                         axis=0).astype(v_ref.dtype)


MASK_VALUE = -1e30


ATTN_BLOCK = 256
ATTN_LOOKAHEAD = 4


def _attn_body(q_ref, k_ref, vt_ref, o_ref, m_ref, acc_ref, pend_ref, *, tq):
    i = pl.program_id(1)
    m_ref[...] = jnp.full(m_ref.shape, MASK_VALUE, F32)
    acc_ref[...] = jnp.zeros(acc_ref.shape, F32)
    key_idx = lax.broadcasted_iota(jnp.int32, (tq, tq), 0)
    qry_idx = lax.broadcasted_iota(jnp.int32, (tq, tq), 1)
    causal = key_idx <= qry_idx

    def scores_t(j, h):
        kj = k_ref[0, h, pl.ds(pl.multiple_of(j * tq, tq), tq), :]
        return lax.dot_general(kj, q_ref[0, h], (((1,), (1,)), ((), ())), preferred_element_type=F32)

    for h in range(ATTN_LOOKAHEAD):
        pend_ref[h] = scores_t(0, h)

    def block(j, carry):
        j_next = jnp.minimum(j + 1, i)
        visible = jnp.logical_or(causal, j < i)
        pending = [pend_ref[h] for h in range(ATTN_LOOKAHEAD)]
        for h in range(MLA_HEADS):
            st = pending.pop(0)
            ahead = h + ATTN_LOOKAHEAD
            if ahead < MLA_HEADS:
                pending.append(scores_t(j, ahead))
            else:
                pend_ref[ahead - MLA_HEADS] = scores_t(j_next, ahead - MLA_HEADS)
            st = jnp.where(visible, st, MASK_VALUE)
            m_prev = m_ref[h]
            m_new = jnp.maximum(m_prev, jnp.max(st, axis=0, keepdims=True))
            alpha = jnp.exp2(m_prev - m_new)
            pt = jnp.exp2(st - m_new)
            acc_ref[h] = alpha * acc_ref[h] + jnp.dot(vt_ref[0, h, j], pt.astype(BF16), preferred_element_type=F32)
            m_ref[h] = m_new
        return carry

    lax.fori_loop(0, i + 1, block, 0)
    for h in range(MLA_HEADS):
        out_t = acc_ref[h, 0:V_HEAD_DIM, :] / acc_ref[h, V_HEAD_DIM:V_HEAD_DIM + 1, :]
        o_ref[0, :, h * V_HEAD_DIM:(h + 1) * V_HEAD_DIM] = out_t.T.astype(o_ref.dtype)


def _attention(q, k, vt):
    b, nh, s, dp = q.shape
    tq = ATTN_BLOCK
    return pl.pallas_call(
        functools.partial(_attn_body, tq=tq),
        grid=(b, s // tq),
        in_specs=[pl.BlockSpec((1, nh, tq, dp), lambda bi, i: (bi, 0, i, 0)),
                  pl.BlockSpec((1, nh, s, dp), lambda bi, i: (bi, 0, 0, 0)),
                  pl.BlockSpec((1, nh, s // tq, VT_ROWS, tq), lambda bi, i: (bi, 0, 0, 0, 0))],
        out_specs=pl.BlockSpec((1, tq, nh * V_HEAD_DIM), lambda bi, i: (bi, i, 0)),
        out_shape=jax.ShapeDtypeStruct((b, s, nh * V_HEAD_DIM), BF16),
        scratch_shapes=[pltpu.VMEM((nh, 1, tq), F32), pltpu.VMEM((nh, VT_ROWS, tq), F32),
                        pltpu.VMEM((ATTN_LOOKAHEAD, tq, tq), F32)],
        compiler_params=_cparams(("parallel", "arbitrary")),
        name="mla_attention",
    )(q, k, vt)


SCAN_VR = 16
SCAN_VQ = RWKV_HEAD_DIM // SCAN_VR
SCAN_TENSORS = 4
SCAN_TC = 64
PREP_TS = LANES


def _rwkv_prep_body(*refs, has_vres, ts, nb):
    if has_vres:
        (x_ref, xp_ref, vf_ref, mu_ref, w0_ref, w2_ref, a0_ref, a2_ref, g2_ref, kk_ref, ka_ref, rk_ref, e_ref, tri_ref,
         v0_ref, v2_ref, xt_ref, val_o, dend_o, g_o, bonus_o, vt_ref) = refs
    else:
        (x_ref, xp_ref, mu_ref, w0_ref, w2_ref, a0_ref, a2_ref, g2_ref, kk_ref, ka_ref, rk_ref, e_ref, tri_ref,
         xt_ref, val_o, dend_o, g_o, bonus_o, vfirst_o, vt_ref) = refs
    i = pl.program_id(0)
    bh = nb * RWKV_HEADS

    def per_batch(b, carry):
        x = x_ref[b]
        prev = jnp.where(i > 0, xp_ref[b][7:8, :], 0.0)
        row = lax.broadcasted_iota(jnp.int32, (ts, 1), 0)
        shifted = jnp.where(row == 0, prev, pltpu.roll(x, 1, axis=0))
        xs = x + (shifted - x) * mu_ref[...]
        wd = RWKV_WIDTH
        r = xs[:, 0:wd]
        k = xs[:, wd:2 * wd]
        v = xs[:, 2 * wd:3 * wd]
        lora_in = xs[:, 3 * wd:3 * wd + LANES]
        xg = xs[:, 3 * wd + LANES:3 * wd + 2 * LANES]
        e = e_ref[...]
        zw = w0_ref[...] + _bdot(jnp.tanh(lora_in), w2_ref[...])
        nz = -zw
        softplus = jnp.maximum(nz, 0.0) + jnp.log(1.0 + jnp.exp(-jnp.abs(nz)))
        log_decay = -jnp.exp(-softplus - 0.5)
        tri = tri_ref[...]
        log_d = sum(jnp.dot(tri, part, preferred_element_type=F32) for part in _split_bf16(log_decay, 3))
        d_incl = jnp.exp(log_d)
        d_prev = jnp.exp(log_d - log_decay)
        d_inv = jnp.exp(-log_d)
        ends = [d_incl[(c + 1) * SCAN_TC - 1:(c + 1) * SCAN_TC, :] for c in range(ts // SCAN_TC)]
        dend_o[0, pl.ds(b, 1), :] = jnp.concatenate(ends, axis=1)
        a_lr = jax.nn.sigmoid(a0_ref[...] + _bdot(lora_in, a2_ref[...]))
        g_o[b] = _bdot(jax.nn.sigmoid(xg), g2_ref[...])
        if has_vres:
            xvs = xs[:, RWKV_COLS:RWKV_COLS + LANES]
            v = v + (vf_ref[b] - v) * jax.nn.sigmoid(v0_ref[...] + _bdot(xvs, v2_ref[...]))
        else:
            vfirst_o[b] = v
        kk = k * kk_ref[...]
        norm = jnp.sqrt(_seg_sum(kk * kk, e))
        kk = kk / jnp.maximum(norm, 1e-12)
        k = k * (1.0 + (a_lr - 1.0) * ka_ref[...])
        bonus_o[b] = _seg_sum(r * k * rk_ref[...], e) * v
        rows = pl.ds(pl.multiple_of(b * wd, wd), wd)
        for idx, val in enumerate((-kk * d_prev, kk * a_lr * d_inv, k * d_inv, r * d_incl)):
            xt_ref[idx, rows, :] = val.T
        vt_ref[rows, :] = v.T
        return carry

    lax.fori_loop(0, nb, per_batch, 0)
    for vr in range(SCAN_VR):
        slab = jnp.concatenate(
            [vt_ref[pl.ds(vq * SCAN_VR + vr, bh, stride=RWKV_HEAD_DIM), :] for vq in range(SCAN_VQ)], axis=0)
        val_o[pl.ds(vr, ts, stride=SCAN_VR), :] = slab.T


def _rwkv_prep(rwkv_cols, v_first, p, seg, layer):
    b, s, n = rwkv_cols.shape
    ts = PREP_TS
    has_vres = v_first is not None
    tok = lambda w: pl.BlockSpec((b, ts, w), lambda i: (0, i, 0))
    halo = pl.BlockSpec((b, 8, n), lambda i: (0, jnp.maximum(i * (ts // 8) - 1, 0), 0))
    step = np.arange(ts)
    same_chunk = (step[:, None] // SCAN_TC) == (step[None, :] // SCAN_TC)
    chunk_tri = jnp.asarray(same_chunk & (step[None, :] <= step[:, None]), BF16)
    layer_consts = [p["mu"], p["w0"], p["w2"], p["a0"], p["a2"], p["g2"], p["k_k"], p["k_a"], p["r_k"]]
    args = [rwkv_cols, rwkv_cols]
    in_specs = [tok(n), halo]
    if has_vres:
        args.append(v_first)
        in_specs.append(tok(RWKV_WIDTH))
    in_specs += [_layer_spec(a, layer) for a in layer_consts] + [_const_spec(seg.shape), _const_spec(chunk_tri.shape)]
    consts = layer_consts + [seg, chunk_tri]
    if has_vres:
        in_specs += [_layer_spec(p["v0"], layer), _layer_spec(p["v2"], layer)]
        consts += [p["v0"], p["v2"]]
    n_tok_out = 2 if has_vres else 3
    return pl.pallas_call(
        functools.partial(_rwkv_prep_body, has_vres=has_vres, ts=ts, nb=b),
        grid=(s // ts,),
        in_specs=in_specs,
        out_specs=[pl.BlockSpec((SCAN_TENSORS, b * RWKV_WIDTH, ts), lambda i: (0, 0, i)),
                   pl.BlockSpec((ts * SCAN_VR, LANES), lambda i: (i, 0)),
                   pl.BlockSpec((1, b, SCAN_SUB * RWKV_WIDTH), lambda i: (i, 0, 0))] + [tok(RWKV_WIDTH)] * n_tok_out,
        out_shape=[jax.ShapeDtypeStruct((SCAN_TENSORS, b * RWKV_WIDTH, s), F32),
                   jax.ShapeDtypeStruct((s * SCAN_VR, LANES), F32),
                   jax.ShapeDtypeStruct((s // ts, b, SCAN_SUB * RWKV_WIDTH), F32)]
        + [jax.ShapeDtypeStruct((b, s, RWKV_WIDTH), F32)] * n_tok_out,
        scratch_shapes=[pltpu.VMEM((b * RWKV_WIDTH, ts), F32)],
        compiler_params=_cparams(("parallel",)),
        name="rwkv_prep",
    )(*args, *consts)


N_ACC = 4
SCAN_SUB = PREP_TS // SCAN_TC


SCAN_UNROLL = 8
SLABS_PER_TILE = SCAN_TENSORS * RWKV_HEAD_DIM
SLABS_PER_BODY = SLABS_PER_TILE * SCAN_UNROLL // PREP_TS
BODIES_PER_TENSOR = RWKV_HEAD_DIM // SLABS_PER_BODY


def _scan_body(xt_ref, v_ref, dprev_ref, dcur_ref, y_ref, s_ref, stage_ref, ych_ref, yt_ref, *, nb):
    g = pl.program_id(0)
    bh = nb * RWKV_HEADS
    wslot = g % 2
    rslot = 1 - wslot

    def stage_slabs(body_idx):
        tensor = body_idx // BODIES_PER_TENSOR
        key0 = (body_idx % BODIES_PER_TENSOR) * SLABS_PER_BODY
        for j in range(SLABS_PER_BODY):
            rows = xt_ref[tensor, pl.ds(key0 + j, bh, stride=RWKV_HEAD_DIM), :]
            stage_ref[wslot, tensor, key0 + j] = jnp.concatenate([rows] * SCAN_VQ, axis=0).T

    @pl.when(g == 0)
    def _():
        s_ref[...] = jnp.zeros_like(s_ref)

        def warm(body_idx, carry):
            stage_slabs(body_idx)
            return carry

        lax.fori_loop(0, SLABS_PER_TILE // SLABS_PER_BODY, warm, 0)

    def tree(acc):
        return (acc[0] + acc[1]) + (acc[2] + acc[3])

    def accumulate(acc, kk, term):
        acc[kk % N_ACC] = term if acc[kk % N_ACC] is None else acc[kk % N_ACC] + term

    def scan_chunk(chunk, dend_ref):
        t0 = chunk * SCAN_TC

        def operand(idx, kk, t):
            return stage_ref[rslot, idx, kk, pl.ds(t, 1), :]

        acc = [None] * N_ACC
        for kk in range(RWKV_HEAD_DIM):
            s_start = s_ref[kk] * dend_ref[0, pl.ds(kk, 1), :]
            s_ref[kk] = s_start
            accumulate(acc, kk, s_start * operand(0, kk, t0))

        def step(t, u):
            op = lambda idx, kk: operand(idx, kk, t)
            t_next = jnp.minimum(t + 1, t0 + SCAN_TC - 1)
            v = v_ref[pl.ds(pl.multiple_of(t * SCAN_VR, SCAN_VR), SCAN_VR), :]
            yacc = [None] * N_ACC
            uacc = [None] * N_ACC
            for kk in range(RWKV_HEAD_DIM):
                s_new = s_ref[kk] + (u * op(1, kk) + v * op(2, kk))
                s_ref[kk] = s_new
                accumulate(yacc, kk, s_new * op(3, kk))
                accumulate(uacc, kk, s_new * operand(0, kk, t_next))
            ych_ref[pl.ds(pl.multiple_of(t * SCAN_VR, SCAN_VR), SCAN_VR), :] = tree(yacc)
            return tree(uacc)

        def body(body_idx, u):
            stage_slabs(chunk * (SCAN_TC // SCAN_UNROLL) + body_idx)
            for j in range(SCAN_UNROLL):
                u = step(t0 + body_idx * SCAN_UNROLL + j, u)
            return u

        lax.fori_loop(0, SCAN_TC // SCAN_UNROLL, body, tree(acc))

    @pl.when(g > 0)
    def _():
        scan_chunk(0, dprev_ref)
        scan_chunk(1, dcur_ref)
        for vr in range(SCAN_VR):
            lanes_by_t = ych_ref[pl.ds(vr, PREP_TS, stride=SCAN_VR), :].T
            for vq in range(SCAN_VQ):
                yt_ref[pl.ds(vq * SCAN_VR + vr, bh, stride=RWKV_HEAD_DIM), :] = lanes_by_t[vq * bh:(vq + 1) * bh, :]
        for b in range(nb):
            y_ref[b] = yt_ref[b * RWKV_WIDTH:(b + 1) * RWKV_WIDTH, :].T


def _wkv_scan(xt, val, dend, nb):
    assert SCAN_SUB == 2
    s = xt.shape[-1]
    n_tiles = s // PREP_TS
    prev_tile = lambda g: jnp.maximum(g - 1, 0)
    return pl.pallas_call(
        functools.partial(_scan_body, nb=nb),
        grid=(n_tiles + 1,),
        in_specs=[
            pl.BlockSpec((SCAN_TENSORS, nb * RWKV_WIDTH, PREP_TS), lambda g: (0, 0, jnp.minimum(g, n_tiles - 1))),
            pl.BlockSpec((PREP_TS * SCAN_VR, LANES), lambda g: (prev_tile(g), 0)),
            pl.BlockSpec((1, RWKV_HEAD_DIM, LANES), lambda g: (jnp.maximum(prev_tile(g) * SCAN_SUB - 1, 0), 0, 0)),
            pl.BlockSpec((1, RWKV_HEAD_DIM, LANES), lambda g: (prev_tile(g) * SCAN_SUB, 0, 0))],
        out_specs=pl.BlockSpec((nb, PREP_TS, RWKV_WIDTH), lambda g: (0, prev_tile(g), 0)),
        out_shape=jax.ShapeDtypeStruct((nb, s, RWKV_WIDTH), F32),
        scratch_shapes=[pltpu.VMEM((RWKV_HEAD_DIM, SCAN_VR, LANES), F32),
                        pltpu.VMEM((2, SCAN_TENSORS, RWKV_HEAD_DIM, PREP_TS, LANES), F32),
                        pltpu.VMEM((PREP_TS * SCAN_VR, LANES), F32),
                        pltpu.VMEM((nb * RWKV_WIDTH, PREP_TS), F32)],
        compiler_params=_cparams(("arbitrary",)),
        name="wkv_scan",
    )(xt, val, dend, dend)


def _merge_body(x_ref, gate_ref, att_ref, y_ref, g_ref, bonus_ref, conv_ref, convp_ref,
                lnw_ref, lnb_ref, e_ref, cw_ref, wa_ref, wb_ref, wc_ref, wo_ref, out_ref, *, ts):
    i = pl.program_id(1)
    e = e_ref[...]
    y = y_ref[0]
    inv_n = 1.0 / RWKV_HEAD_DIM
    mean = _seg_sum(y, e) * inv_n
    d = y - mean
    var = _seg_sum(d * d, e) * inv_n
    yn = d * lax.rsqrt(var + GN_EPS) * lnw_ref[...] + lnb_ref[...] + bonus_ref[0]
    ob = _bdot(yn * g_ref[0], wb_ref[...])

    cw = CONV_WIDTH
    c = conv_ref[0]
    u = c[:, cw:2 * cw] * c[:, 2 * cw:3 * cw]
    cp = convp_ref[0]
    up = jnp.where(i > 0, cp[:, cw:2 * cw] * cp[:, 2 * cw:3 * cw], 0.0)
    p6 = up[6:7, :]
    p7 = up[7:8, :]
    row = lax.broadcasted_iota(jnp.int32, (ts, 1), 0)
    u1 = jnp.where(row == 0, p7, pltpu.roll(u, 1, axis=0))
    u2 = jnp.where(row == 0, p6, jnp.where(row == 1, p7, pltpu.roll(u, 2, axis=0)))
    taps = cw_ref[...]
    yc = taps[0:1, :] * u2 + taps[1:2, :] * u1 + taps[2:3, :] * u
    oc = _bdot(c[:, 0:cw] * yc, wc_ref[...])

    oa = jnp.dot(att_ref[0], wa_ref[...], preferred_element_type=F32)
    gates = gate_ref[0].astype(F32)
    dm = D_MODEL
    merged = gates[:, 0:dm] * oa + gates[:, dm:2 * dm] * ob + gates[:, 2 * dm:3 * dm] * oc
    out_ref[0] = x_ref[0] + _bdot(merged, wo_ref[...])


def _merge(x, gates, att, y, g, bonus, conv_cols, p, seg, layer, ts=512):
    b, s, d = x.shape
    tok = lambda w: pl.BlockSpec((1, ts, w), lambda bi, i: (bi, i, 0))
    nconv = conv_cols.shape[-1]
    halo = pl.BlockSpec((1, 8, nconv), lambda bi, i: (bi, jnp.maximum(i * (ts // 8) - 1, 0), 0))
    consts = (p["ln_w"], p["ln_b"], seg, p["conv_w"], p["mla_w_o"], p["rwkv_w_o"], p["conv_w_o"], p["w_out"])
    const_specs = [_const_spec(a.shape) if a is seg else _layer_spec(a, layer) for a in consts]
    return pl.pallas_call(
        functools.partial(_merge_body, ts=ts),
        grid=(b, s // ts),
        in_specs=[tok(d), tok(GATE_COLS), tok(att.shape[-1]), tok(RWKV_WIDTH), tok(RWKV_WIDTH), tok(RWKV_WIDTH),
                  tok(nconv), halo] + const_specs,
        out_specs=tok(d),
        out_shape=jax.ShapeDtypeStruct((b, s, d), F32),
        compiler_params=_cparams(("parallel", "parallel")),
        name="branch_merge",
    )(x, gates, att, y, g, bonus, conv_cols, conv_cols, *consts)


def _mlp_body(x_ref, g_ref, wu_ref, wd_ref, o_ref):
    x = x_ref[...]
    ms = jnp.mean(x * x, axis=-1, keepdims=True)
    h = (x * lax.rsqrt(ms + NORM_EPS) * g_ref[...]).astype(BF16)
    up = jnp.dot(h, wu_ref[...], preferred_element_type=F32)
    act = jnp.square(jnp.maximum(up, 0.0)).astype(BF16)
    o_ref[...] = x + jnp.dot(act, wd_ref[...], preferred_element_type=F32)


def _mlp(x2d, p, layer, tm=256):
    t, d = x2d.shape
    row = pl.BlockSpec((tm, d), lambda i: (i, 0))
    consts = (p["mlp_norm"], p["w_up"], p["w_down"])
    return pl.pallas_call(
        _mlp_body,
        grid=(t // tm,),
        in_specs=[row] + [_layer_spec(a, layer) for a in consts],
        out_specs=row,
        out_shape=jax.ShapeDtypeStruct((t, d), F32),
        compiler_params=_cparams(("parallel",)),
        name="mlp",
    )(x2d, *consts)


def _rope_partner_cols(w):
    half = QK_ROPE_DIM // 2
    return jnp.concatenate([-w[..., half:], w[..., :half]], axis=-1)


def _pad_lanes(w, lo, total=HEAD_PAD):
    n = w.shape[-1]
    pad = [(0, 0)] * (w.ndim - 1) + [(lo, total - lo - n)]
    return jnp.pad(w, pad)


def _stacked_params(attn_norm, w_in, mla_q_a_norm, mla_wq_b, mla_kv_a_norm, mla_wkv_b, mla_q_norm, mla_k_norm,
                    mla_w_o, rwkv_mu, rwkv_w0, rwkv_w2, rwkv_a0, rwkv_a2, rwkv_g2, rwkv_k_k, rwkv_k_a, rwkv_r_k,
                    rwkv_ln_w, rwkv_ln_b, rwkv_w_o, rwkv_v1, rwkv_v_mu, rwkv_v0, rwkv_v2, conv_w, conv_w_o, w_out,
                    mlp_norm, w_up, w_down):
    p = {}
    n_layers = w_in.shape[0]
    row = lambda a: a.reshape(n_layers, 1, -1).astype(F32)
    first_layer_zeros = lambda a: jnp.pad(a, [(1, 0)] + [(0, 0)] * (a.ndim - 1))
    w = w_in
    o_mla = GATE_COLS
    o_rwkv = o_mla + MLA_COLS
    o_conv = o_rwkv + RWKV_COLS
    p["attn_norm"] = row(attn_norm)
    p["w_gate"] = w[..., :GATE_COLS].astype(BF16)
    w_kpe = w[..., o_mla + Q_LORA_RANK + KV_LORA_RANK:o_rwkv]
    p["w_mla"] = jnp.concatenate(
        [w[..., o_mla:o_mla + Q_LORA_RANK + KV_LORA_RANK], _pad_lanes(w_kpe, QK_NOPE_DIM),
         _pad_lanes(_rope_partner_cols(w_kpe), QK_NOPE_DIM)], axis=-1).astype(BF16)
    p["w_rwkv"] = jnp.concatenate(
        [w[..., o_rwkv:o_conv], _pad_lanes(first_layer_zeros(rwkv_v1), 0)], axis=-1).astype(BF16)
    p["mu"] = row(jnp.concatenate([rwkv_mu, _pad_lanes(first_layer_zeros(rwkv_v_mu), 0)], axis=-1))
    p["v0"] = row(first_layer_zeros(rwkv_v0))
    p["v2"] = jnp.pad(first_layer_zeros(rwkv_v2), ((0, 0), (0, LANES - MV_LORA), (0, 0)))
    p["w_conv"] = w[..., o_conv:].astype(BF16)

    scale = QK_HEAD_DIM ** -0.5 * float(np.log2(np.e))
    wq = mla_wq_b.reshape(n_layers, Q_LORA_RANK, MLA_HEADS, QK_HEAD_DIM)
    p["wq"] = _pad_lanes(wq, 0).reshape(n_layers, Q_LORA_RANK, -1).astype(BF16)
    p["wqr"] = _pad_lanes(_rope_partner_cols(wq[..., QK_NOPE_DIM:]), QK_NOPE_DIM).reshape(
        n_layers, Q_LORA_RANK, -1).astype(BF16)
    wkv = mla_wkv_b.reshape(n_layers, KV_LORA_RANK, MLA_HEADS, QK_NOPE_DIM + V_HEAD_DIM)
    p["wk"] = _pad_lanes(wkv[..., :QK_NOPE_DIM], 0).reshape(n_layers, KV_LORA_RANK, -1).astype(BF16)
    p["wv"] = wkv[..., QK_NOPE_DIM:].reshape(n_layers, KV_LORA_RANK, -1).astype(BF16)
    p["qan"] = row(mla_q_a_norm)
    p["kvan"] = row(mla_kv_a_norm)
    swap = lambda g: jnp.concatenate([g[..., QK_ROPE_DIM // 2:], g[..., :QK_ROPE_DIM // 2]], axis=-1)
    gq, gk = mla_q_norm * scale, mla_k_norm
    p["gq"] = row(_pad_lanes(gq, 0))
    p["gqr"] = row(_pad_lanes(swap(gq[..., QK_NOPE_DIM:]), QK_NOPE_DIM))
    p["gk"] = row(_pad_lanes(gk, 0))
    p["gkr"] = row(_pad_lanes(swap(gk[..., QK_NOPE_DIM:]), QK_NOPE_DIM))
    p["mla_w_o"] = mla_w_o.astype(BF16)

    p["w0"] = row(rwkv_w0)
    p["w2"] = jnp.pad(rwkv_w2, ((0, 0), (0, AAA_LORA), (0, 0)))
    p["a0"] = row(rwkv_a0)
    p["a2"] = jnp.pad(rwkv_a2, ((0, 0), (DECAY_LORA, 0), (0, 0)))
    p["g2"] = rwkv_g2
    p["k_k"] = row(rwkv_k_k)
    p["k_a"] = row(rwkv_k_a)
    p["r_k"] = row(rwkv_r_k)
    p["ln_w"] = row(rwkv_ln_w)
    p["ln_b"] = row(rwkv_ln_b)
    p["rwkv_w_o"] = rwkv_w_o.astype(BF16)
    p["conv_w"] = conv_w.astype(F32)
    p["conv_w_o"] = conv_w_o.astype(BF16)
    p["w_out"] = w_out.astype(BF16)
    p["mlp_norm"] = row(mlp_norm)
    p["w_up"] = w_up.astype(BF16)
    p["w_down"] = w_down.astype(BF16)
    return p


def _segment_matrix():
    head_of = np.arange(RWKV_WIDTH) // RWKV_HEAD_DIM
    return jnp.asarray(head_of[:, None] == head_of[None, :], BF16)


def _rope_tables(positions):
    half = QK_ROPE_DIM // 2
    freqs = ROPE_THETA ** (-(jnp.arange(half, dtype=F32) * 2.0 / QK_ROPE_DIM))
    ang = (positions.astype(F32)[..., None] * freqs).reshape(-1, LANES)
    cos, sin = lax.optimization_barrier((jnp.cos(ang), jnp.sin(ang)))
    cos = cos.reshape(positions.shape + (half,))
    sin = sin.reshape(positions.shape + (half,))
    ones = jnp.ones(positions.shape + (QK_NOPE_DIM,), F32)
    tail = jnp.ones(positions.shape + (HEAD_PAD - QK_HEAD_DIM,), F32)
    cosf = jnp.concatenate([ones, cos, cos, tail], axis=-1)
    sinf = jnp.concatenate([0 * ones, sin, sin, 0 * tail], axis=-1)
    return cosf, sinf


def kernel(x, positions, attn_norm, w_in, mla_q_a_norm, mla_wq_b, mla_kv_a_norm, mla_wkv_b, mla_q_norm, mla_k_norm, mla_w_o, rwkv_mu, rwkv_w0, rwkv_w2, rwkv_a0, rwkv_a2, rwkv_g2, rwkv_k_k, rwkv_k_a, rwkv_r_k, rwkv_ln_w, rwkv_ln_b, rwkv_w_o, rwkv_v1, rwkv_v_mu, rwkv_v0, rwkv_v2, conv_w, conv_w_o, w_out, mlp_norm, w_up, w_down):
    weights = (attn_norm, w_in, mla_q_a_norm, mla_wq_b, mla_kv_a_norm, mla_wkv_b, mla_q_norm, mla_k_norm, mla_w_o,
               rwkv_mu, rwkv_w0, rwkv_w2, rwkv_a0, rwkv_a2, rwkv_g2, rwkv_k_k, rwkv_k_a, rwkv_r_k, rwkv_ln_w,
               rwkv_ln_b, rwkv_w_o, rwkv_v1, rwkv_v_mu, rwkv_v0, rwkv_v2, conv_w, conv_w_o, w_out, mlp_norm,
               w_up, w_down)
    b, s, d = x.shape
    cosf, sinf = _rope_tables(positions)
    v_first = None
    p = _stacked_params(*weights)
    seg = _segment_matrix()
    for l in range(DEPTH):
        gates, rwkv_cols, conv_cols, q, k, v = _in_proj(x, cosf, sinf, p, l)
        att = _attention(q, k, v)
        prep = _rwkv_prep(rwkv_cols.reshape(b, s, -1), v_first, p, seg, l)
        scan_xt, scan_val, chunk_decay, g_, bonus = prep[:5]
        if l == 0:
            v_first = prep[5]
        dend = chunk_decay.reshape(-1, b, SCAN_SUB, RWKV_HEADS, RWKV_HEAD_DIM).transpose(0, 2, 4, 1, 3)
        dend = jnp.tile(dend.reshape(-1, RWKV_HEAD_DIM, b * RWKV_HEADS), (1, 1, SCAN_VQ))
        y = _wkv_scan(scan_xt, scan_val, dend, b)
        x = _merge(x, gates.reshape(b, s, -1), att, y, g_, bonus, conv_cols.reshape(b, s, -1), p, seg, l)
        x = _mlp(x.reshape(b * s, d), p, l).reshape(b, s, d)
    return x
```

```python
import functools

import jax
import jax.numpy as jnp
import numpy as np
from jax import lax
from jax.experimental import pallas as pl
from jax.experimental.pallas import tpu as pltpu

D_MODEL = 1024
DEPTH = 2
MLA_HEADS = 8
QK_NOPE_DIM = 64
QK_ROPE_DIM = 32
QK_HEAD_DIM = QK_NOPE_DIM + QK_ROPE_DIM
V_HEAD_DIM = 64
Q_LORA_RANK = 384
KV_LORA_RANK = 256
ROPE_THETA = 10000.0
RWKV_HEAD_DIM = 64
RWKV_HEADS = 4
RWKV_WIDTH = RWKV_HEADS * RWKV_HEAD_DIM
DECAY_LORA = 64
AAA_LORA = 64
GATE_LORA = 128
MV_LORA = 32
GN_EPS = 64e-5
CONV_WIDTH = 256
CONV_K = 3
D_FF = 4 * D_MODEL
N_BRANCH = 3
NORM_EPS = 1e-6
GATE_COLS = N_BRANCH * D_MODEL
MLA_COLS = Q_LORA_RANK + KV_LORA_RANK + QK_ROPE_DIM
RWKV_COLS = 3 * RWKV_WIDTH + DECAY_LORA + AAA_LORA + GATE_LORA

LANES = 128
HEAD_PAD = LANES
VT_ROWS = V_HEAD_DIM + 16
MLA_OUT_COLS = Q_LORA_RANK + KV_LORA_RANK + 2 * LANES
VMEM_LIMIT = 56 * 1024 * 1024

F32 = jnp.float32
BF16 = jnp.bfloat16


def _cparams(sem):
    return pltpu.CompilerParams(dimension_semantics=sem, vmem_limit_bytes=VMEM_LIMIT)


def _const_spec(shape):
    nd = len(shape)
    return pl.BlockSpec(shape, lambda *_: (0,) * nd, pipeline_mode=pl.Buffered(1))


def _layer_spec(arr, layer):
    nd = arr.ndim
    return pl.BlockSpec((None,) + arr.shape[1:], lambda *_: (layer,) + (0,) * (nd - 1),
                        pipeline_mode=pl.Buffered(1))


def _bdot(a, b):
    return jnp.dot(a.astype(BF16), b.astype(BF16), preferred_element_type=F32)


def _seg_sum(x, e):
    return sum(jnp.dot(part, e, preferred_element_type=F32) for part in _split_bf16(x, 2))


def _split_bf16(x, parts):
    terms = []
    for _ in range(parts - 1):
        term = x.astype(BF16)
        terms.append(term)
        x = x - term.astype(F32)
    return terms + [x.astype(BF16)]


def _in_proj_body(x_ref, g_ref, wg_ref, wm_ref, wr_ref, wc_ref, cos_ref, sin_ref, qan_ref, kvan_ref,
                  wq_ref, wqr_ref, wk_ref, wv_ref, gq_ref, gqr_ref, gk_ref, gkr_ref,
                  gate_ref, rwkv_ref, conv_ref, q_ref, k_ref, v_ref):
    x = x_ref[...]
    ms = jnp.mean(x * x, axis=-1, keepdims=True)
    h = (x * lax.rsqrt(ms + NORM_EPS) * g_ref[...]).astype(BF16)
    mla_cols = jnp.dot(h, wm_ref[...], preferred_element_type=F32)
    _mla_heads(mla_cols, cos_ref, sin_ref, qan_ref, kvan_ref, wq_ref, wqr_ref, wk_ref, wv_ref,
               gq_ref, gqr_ref, gk_ref, gkr_ref, q_ref, k_ref, v_ref)
    gate_ref[...] = jax.nn.sigmoid(jnp.dot(h, wg_ref[...], preferred_element_type=F32)).astype(gate_ref.dtype)
    rwkv_ref[...] = jnp.dot(h, wr_ref[...], preferred_element_type=F32)
    conv_ref[...] = jnp.dot(h, wc_ref[...], preferred_element_type=F32)


def _in_proj(x, cosf, sinf, p, layer):
    b, s, d = x.shape
    tm = ATTN_BLOCK
    per_seq = s // tm
    t = b * s
    row = lambda n: pl.BlockSpec((tm, n), lambda i: (i, 0))
    tok = pl.BlockSpec((1, tm, LANES), lambda i: (i // per_seq, i % per_seq, 0))
    head = lambda w: pl.BlockSpec((1, MLA_HEADS, tm, w), lambda i: (i // per_seq, 0, i % per_seq, 0))
    gain, wg, wm, wr, wc = p["attn_norm"], p["w_gate"], p["w_mla"], p["w_rwkv"], p["w_conv"]
    mla_consts = (p["qan"], p["kvan"], p["wq"], p["wqr"], p["wk"], p["wv"], p["gq"], p["gqr"], p["gk"], p["gkr"])
    widths = (wg.shape[-1], wr.shape[-1], wc.shape[-1])
    return pl.pallas_call(
        _in_proj_body,
        grid=(t // tm,),
        in_specs=[row(d)] + [_layer_spec(a, layer) for a in (gain, wg, wm, wr, wc)]
        + [tok, tok] + [_layer_spec(a, layer) for a in mla_consts],
        out_specs=[row(n) for n in widths]
        + [head(HEAD_PAD), head(HEAD_PAD),
           pl.BlockSpec((1, MLA_HEADS, 1, VT_ROWS, tm), lambda i: (i // per_seq, 0, i % per_seq, 0, 0))],
        out_shape=[jax.ShapeDtypeStruct((t, widths[0]), BF16)]
        + [jax.ShapeDtypeStruct((t, n), F32) for n in widths[1:]]
        + [jax.ShapeDtypeStruct((b, MLA_HEADS, s, HEAD_PAD), BF16),
           jax.ShapeDtypeStruct((b, MLA_HEADS, s, HEAD_PAD), BF16),
           jax.ShapeDtypeStruct((b, MLA_HEADS, per_seq, VT_ROWS, tm), BF16)],
        compiler_params=_cparams(("parallel",)),
        name="in_proj",
    )(x.reshape(t, d), gain, wg, wm, wr, wc, cosf, sinf, *mla_consts)


def _mla_heads(c, cos_ref, sin_ref, qan_ref, kvan_ref, wq_ref, wqr_ref, wk_ref, wv_ref,
               gq_ref, gqr_ref, gk_ref, gkr_ref, q_ref, k_ref, v_ref):
    cq = c[:, :Q_LORA_RANK]
    ckv = c[:, Q_LORA_RANK:Q_LORA_RANK + KV_LORA_RANK]
    kpe = c[:, Q_LORA_RANK + KV_LORA_RANK:Q_LORA_RANK + KV_LORA_RANK + LANES]
    kper = c[:, Q_LORA_RANK + KV_LORA_RANK + LANES:]

    def rms(z, g):
        return (z * lax.rsqrt(jnp.mean(z * z, axis=-1, keepdims=True) + NORM_EPS) * g).astype(BF16)

    cqn = rms(cq, qan_ref[...])
    ckvn = rms(ckv, kvan_ref[...])
    q = jnp.dot(cqn, wq_ref[...], preferred_element_type=F32)
    qr = jnp.dot(cqn, wqr_ref[...], preferred_element_type=F32)
    kn = jnp.dot(ckvn, wk_ref[...], preferred_element_type=F32)
    v = jnp.dot(ckvn, wv_ref[...], preferred_element_type=F32)
    cosf = cos_ref[0]
    sinf = sin_ref[0]
    gqc = gq_ref[...] * cosf
    gqs = gqr_ref[...] * sinf
    gkc = gk_ref[...] * cosf
    kper_s = kper * gkr_ref[...] * sinf
    inv_dim = 1.0 / QK_HEAD_DIM
    vt = v.T
    pad_row = lax.broadcasted_iota(jnp.int32, (VT_ROWS - V_HEAD_DIM, vt.shape[1]), 0)
    ones_rows = jnp.where(pad_row == 0, 1.0, 0.0)
    for h in range(MLA_HEADS):
        sl = slice(h * HEAD_PAD, (h + 1) * HEAD_PAD)
        qh = q[:, sl]
        rq = lax.rsqrt(jnp.sum(qh * qh, axis=-1, keepdims=True) * inv_dim + NORM_EPS)
        q_ref[0, h] = (rq * (qh * gqc + qr[:, sl] * gqs)).astype(q_ref.dtype)
        kh = kn[:, sl] + kpe
        rk = lax.rsqrt(jnp.sum(kh * kh, axis=-1, keepdims=True) * inv_dim + NORM_EPS)
        k_ref[0, h] = (rk * (kh * gkc + kper_s)).astype(k_ref.dtype)
        v_ref[0, h, 0] = jnp.concatenate([vt[h * V_HEAD_DIM:(h + 1) * V_HEAD_DIM, :], ones_rows],
                                         axis=0).astype(v_ref.dtype)


MASK_VALUE = -1e30


ATTN_BLOCK = 256
ATTN_LOOKAHEAD = 4


def _attn_body(q_ref, k_ref, vt_ref, o_ref, m_ref, acc_ref, pend_ref, *, tq):
    i = pl.program_id(1)
    m_ref[...] = jnp.full(m_ref.shape, MASK_VALUE, F32)
    acc_ref[...] = jnp.zeros(acc_ref.shape, F32)
    key_idx = lax.broadcasted_iota(jnp.int32, (tq, tq), 0)
    qry_idx = lax.broadcasted_iota(jnp.int32, (tq, tq), 1)
    causal = key_idx <= qry_idx

    def scores_t(j, h):
        kj = k_ref[0, h, pl.ds(pl.multiple_of(j * tq, tq), tq), :]
        return lax.dot_general(kj, q_ref[0, h], (((1,), (1,)), ((), ())), preferred_element_type=F32)

    for h in range(ATTN_LOOKAHEAD):
        pend_ref[h] = scores_t(0, h)

    def block(j, carry):
        j_next = jnp.minimum(j + 1, i)
        visible = jnp.logical_or(causal, j < i)
        pending = [pend_ref[h] for h in range(ATTN_LOOKAHEAD)]
        for h in range(MLA_HEADS):
            st = pending.pop(0)
            ahead = h + ATTN_LOOKAHEAD
            if ahead < MLA_HEADS:
                pending.append(scores_t(j, ahead))
            else:
                pend_ref[ahead - MLA_HEADS] = scores_t(j_next, ahead - MLA_HEADS)
            st = jnp.where(visible, st, MASK_VALUE)
            m_prev = m_ref[h]
            m_new = jnp.maximum(m_prev, jnp.max(st, axis=0, keepdims=True))
            alpha = jnp.exp2(m_prev - m_new)
            pt = jnp.exp2(st - m_new)
            acc_ref[h] = alpha * acc_ref[h] + jnp.dot(vt_ref[0, h, j], pt.astype(BF16), preferred_element_type=F32)
            m_ref[h] = m_new
        return carry

    lax.fori_loop(0, i + 1, block, 0)
    for h in range(MLA_HEADS):
        out_t = acc_ref[h, 0:V_HEAD_DIM, :] / acc_ref[h, V_HEAD_DIM:V_HEAD_DIM + 1, :]
        o_ref[0, :, h * V_HEAD_DIM:(h + 1) * V_HEAD_DIM] = out_t.T.astype(o_ref.dtype)


def _attention(q, k, vt):
    b, nh, s, dp = q.shape
    tq = ATTN_BLOCK
    return pl.pallas_call(
        functools.partial(_attn_body, tq=tq),
        grid=(b, s // tq),
        in_specs=[pl.BlockSpec((1, nh, tq, dp), lambda bi, i: (bi, 0, i, 0)),
                  pl.BlockSpec((1, nh, s, dp), lambda bi, i: (bi, 0, 0, 0)),
                  pl.BlockSpec((1, nh, s // tq, VT_ROWS, tq), lambda bi, i: (bi, 0, 0, 0, 0))],
        out_specs=pl.BlockSpec((1, tq, nh * V_HEAD_DIM), lambda bi, i: (bi, i, 0)),
        out_shape=jax.ShapeDtypeStruct((b, s, nh * V_HEAD_DIM), BF16),
        scratch_shapes=[pltpu.VMEM((nh, 1, tq), F32), pltpu.VMEM((nh, VT_ROWS, tq), F32),
                        pltpu.VMEM((ATTN_LOOKAHEAD, tq, tq), F32)],
        compiler_params=_cparams(("parallel", "arbitrary")),
        name="mla_attention",
    )(q, k, vt)


SCAN_VR = 16
SCAN_VQ = RWKV_HEAD_DIM // SCAN_VR
SCAN_TENSORS = 4
SCAN_TC = 64
PREP_TS = LANES


def _rwkv_prep_body(*refs, has_vres, ts, nb):
    if has_vres:
        (x_ref, xp_ref, vf_ref, mu_ref, w0_ref, w2_ref, a0_ref, a2_ref, g2_ref, kk_ref, ka_ref, rk_ref, e_ref, tri_ref,
         v0_ref, v2_ref, xt_ref, val_o, dend_o, g_o, bonus_o, vt_ref) = refs
    else:
        (x_ref, xp_ref, mu_ref, w0_ref, w2_ref, a0_ref, a2_ref, g2_ref, kk_ref, ka_ref, rk_ref, e_ref, tri_ref,
         xt_ref, val_o, dend_o, g_o, bonus_o, vfirst_o, vt_ref) = refs
    i = pl.program_id(0)
    bh = nb * RWKV_HEADS

    def per_batch(b, carry):
        x = x_ref[b]
        prev = jnp.where(i > 0, xp_ref[b][7:8, :], 0.0)
        row = lax.broadcasted_iota(jnp.int32, (ts, 1), 0)
        shifted = jnp.where(row == 0, prev, pltpu.roll(x, 1, axis=0))
        xs = x + (shifted - x) * mu_ref[...]
        wd = RWKV_WIDTH
        r = xs[:, 0:wd]
        k = xs[:, wd:2 * wd]
        v = xs[:, 2 * wd:3 * wd]
        lora_in = xs[:, 3 * wd:3 * wd + LANES]
        xg = xs[:, 3 * wd + LANES:3 * wd + 2 * LANES]
        e = e_ref[...]
        zw = w0_ref[...] + _bdot(jnp.tanh(lora_in), w2_ref[...])
        nz = -zw
        softplus = jnp.maximum(nz, 0.0) + jnp.log(1.0 + jnp.exp(-jnp.abs(nz)))
        log_decay = -jnp.exp(-softplus - 0.5)
        tri = tri_ref[...]
        log_d = sum(jnp.dot(tri, part, preferred_element_type=F32) for part in _split_bf16(log_decay, 3))
        d_incl = jnp.exp(log_d)
        d_prev = jnp.exp(log_d - log_decay)
        d_inv = jnp.exp(-log_d)
        ends = [d_incl[(c + 1) * SCAN_TC - 1:(c + 1) * SCAN_TC, :] for c in range(ts // SCAN_TC)]
        dend_o[0, pl.ds(b, 1), :] = jnp.concatenate(ends, axis=1)
        a_lr = jax.nn.sigmoid(a0_ref[...] + _bdot(lora_in, a2_ref[...]))
        g_o[b] = _bdot(jax.nn.sigmoid(xg), g2_ref[...])
        if has_vres:
            xvs = xs[:, RWKV_COLS:RWKV_COLS + LANES]
            v = v + (vf_ref[b] - v) * jax.nn.sigmoid(v0_ref[...] + _bdot(xvs, v2_ref[...]))
        else:
            vfirst_o[b] = v
        kk = k * kk_ref[...]
        norm = jnp.sqrt(_seg_sum(kk * kk, e))
        kk = kk / jnp.maximum(norm, 1e-12)
        k = k * (1.0 + (a_lr - 1.0) * ka_ref[...])
        bonus_o[b] = _seg_sum(r * k * rk_ref[...], e) * v
        rows = pl.ds(pl.multiple_of(b * wd, wd), wd)
        for idx, val in enumerate((-kk * d_prev, kk * a_lr * d_inv, k * d_inv, r * d_incl)):
            xt_ref[idx, rows, :] = val.T
        vt_ref[rows, :] = v.T
        return carry

    lax.fori_loop(0, nb, per_batch, 0)
    for vr in range(SCAN_VR):
        slab = jnp.concatenate(
            [vt_ref[pl.ds(vq * SCAN_VR + vr, bh, stride=RWKV_HEAD_DIM), :] for vq in range(SCAN_VQ)], axis=0)
        val_o[pl.ds(vr, ts, stride=SCAN_VR), :] = slab.T


def _rwkv_prep(rwkv_cols, v_first, p, seg, layer):
    b, s, n = rwkv_cols.shape
    ts = PREP_TS
    has_vres = v_first is not None
    tok = lambda w: pl.BlockSpec((b, ts, w), lambda i: (0, i, 0))
    halo = pl.BlockSpec((b, 8, n), lambda i: (0, jnp.maximum(i * (ts // 8) - 1, 0), 0))
    step = np.arange(ts)
    same_chunk = (step[:, None] // SCAN_TC) == (step[None, :] // SCAN_TC)
    chunk_tri = jnp.asarray(same_chunk & (step[None, :] <= step[:, None]), BF16)
    layer_consts = [p["mu"], p["w0"], p["w2"], p["a0"], p["a2"], p["g2"], p["k_k"], p["k_a"], p["r_k"]]
    args = [rwkv_cols, rwkv_cols]
    in_specs = [tok(n), halo]
    if has_vres:
        args.append(v_first)
        in_specs.append(tok(RWKV_WIDTH))
    in_specs += [_layer_spec(a, layer) for a in layer_consts] + [_const_spec(seg.shape), _const_spec(chunk_tri.shape)]
    consts = layer_consts + [seg, chunk_tri]
    if has_vres:
        in_specs += [_layer_spec(p["v0"], layer), _layer_spec(p["v2"], layer)]
        consts += [p["v0"], p["v2"]]
    n_tok_out = 2 if has_vres else 3
    return pl.pallas_call(
        functools.partial(_rwkv_prep_body, has_vres=has_vres, ts=ts, nb=b),
        grid=(s // ts,),
        in_specs=in_specs,
        out_specs=[pl.BlockSpec((SCAN_TENSORS, b * RWKV_WIDTH, ts), lambda i: (0, 0, i)),
                   pl.BlockSpec((ts * SCAN_VR, LANES), lambda i: (i, 0)),
                   pl.BlockSpec((1, b, SCAN_SUB * RWKV_WIDTH), lambda i: (i, 0, 0))] + [tok(RWKV_WIDTH)] * n_tok_out,
        out_shape=[jax.ShapeDtypeStruct((SCAN_TENSORS, b * RWKV_WIDTH, s), F32),
                   jax.ShapeDtypeStruct((s * SCAN_VR, LANES), F32),
                   jax.ShapeDtypeStruct((s // ts, b, SCAN_SUB * RWKV_WIDTH), F32)]
        + [jax.ShapeDtypeStruct((b, s, RWKV_WIDTH), F32)] * n_tok_out,
        scratch_shapes=[pltpu.VMEM((b * RWKV_WIDTH, ts), F32)],
        compiler_params=_cparams(("parallel",)),
        name="rwkv_prep",
    )(*args, *consts)


N_ACC = 4
SCAN_SUB = PREP_TS // SCAN_TC


SCAN_UNROLL = 8
SLABS_PER_TILE = SCAN_TENSORS * RWKV_HEAD_DIM
SLABS_PER_BODY = SLABS_PER_TILE * SCAN_UNROLL // PREP_TS
BODIES_PER_TENSOR = RWKV_HEAD_DIM // SLABS_PER_BODY


def _scan_body(xt_ref, v_ref, dprev_ref, dcur_ref, y_ref, s_ref, stage_ref, ych_ref, yt_ref, *, nb):
    g = pl.program_id(0)
    bh = nb * RWKV_HEADS
    wslot = g % 2
    rslot = 1 - wslot

    def stage_slabs(body_idx):
        tensor = body_idx // BODIES_PER_TENSOR
        key0 = (body_idx % BODIES_PER_TENSOR) * SLABS_PER_BODY
        for j in range(SLABS_PER_BODY):
            rows = xt_ref[tensor, pl.ds(key0 + j, bh, stride=RWKV_HEAD_DIM), :]
            stage_ref[wslot, tensor, key0 + j] = jnp.concatenate([rows] * SCAN_VQ, axis=0).T

    @pl.when(g == 0)
    def _():
        s_ref[...] = jnp.zeros_like(s_ref)

        def warm(body_idx, carry):
            stage_slabs(body_idx)
            return carry

        lax.fori_loop(0, SLABS_PER_TILE // SLABS_PER_BODY, warm, 0)

    def tree(acc):
        return (acc[0] + acc[1]) + (acc[2] + acc[3])

    def accumulate(acc, kk, term):
        acc[kk % N_ACC] = term if acc[kk % N_ACC] is None else acc[kk % N_ACC] + term

    def scan_chunk(chunk, dend_ref):
        t0 = chunk * SCAN_TC

        def operand(idx, kk, t):
            return stage_ref[rslot, idx, kk, pl.ds(t, 1), :]

        acc = [None] * N_ACC
        for kk in range(RWKV_HEAD_DIM):
            s_start = s_ref[kk] * dend_ref[0, pl.ds(kk, 1), :]
            s_ref[kk] = s_start
            accumulate(acc, kk, s_start * operand(0, kk, t0))

        def step(t, u):
            op = lambda idx, kk: operand(idx, kk, t)
            t_next = jnp.minimum(t + 1, t0 + SCAN_TC - 1)
            v = v_ref[pl.ds(pl.multiple_of(t * SCAN_VR, SCAN_VR), SCAN_VR), :]
            yacc = [None] * N_ACC
            uacc = [None] * N_ACC
            for kk in range(RWKV_HEAD_DIM):
                s_new = s_ref[kk] + (u * op(1, kk) + v * op(2, kk))
                s_ref[kk] = s_new
                accumulate(yacc, kk, s_new * op(3, kk))
                accumulate(uacc, kk, s_new * operand(0, kk, t_next))
            ych_ref[pl.ds(pl.multiple_of(t * SCAN_VR, SCAN_VR), SCAN_VR), :] = tree(yacc)
            return tree(uacc)

        def body(body_idx, u):
            stage_slabs(chunk * (SCAN_TC // SCAN_UNROLL) + body_idx)
            for j in range(SCAN_UNROLL):
                u = step(t0 + body_idx * SCAN_UNROLL + j, u)
            return u

        lax.fori_loop(0, SCAN_TC // SCAN_UNROLL, body, tree(acc))

    @pl.when(g > 0)
    def _():
        scan_chunk(0, dprev_ref)
        scan_chunk(1, dcur_ref)
        for vr in range(SCAN_VR):
            lanes_by_t = ych_ref[pl.ds(vr, PREP_TS, stride=SCAN_VR), :].T
            for vq in range(SCAN_VQ):
                yt_ref[pl.ds(vq * SCAN_VR + vr, bh, stride=RWKV_HEAD_DIM), :] = lanes_by_t[vq * bh:(vq + 1) * bh, :]
        for b in range(nb):
            y_ref[b] = yt_ref[b * RWKV_WIDTH:(b + 1) * RWKV_WIDTH, :].T


def _wkv_scan(xt, val, dend, nb):
    assert SCAN_SUB == 2
    s = xt.shape[-1]
    n_tiles = s // PREP_TS
    prev_tile = lambda g: jnp.maximum(g - 1, 0)
    return pl.pallas_call(
        functools.partial(_scan_body, nb=nb),
        grid=(n_tiles + 1,),
        in_specs=[
            pl.BlockSpec((SCAN_TENSORS, nb * RWKV_WIDTH, PREP_TS), lambda g: (0, 0, jnp.minimum(g, n_tiles - 1))),
            pl.BlockSpec((PREP_TS * SCAN_VR, LANES), lambda g: (prev_tile(g), 0)),
            pl.BlockSpec((1, RWKV_HEAD_DIM, LANES), lambda g: (jnp.maximum(prev_tile(g) * SCAN_SUB - 1, 0), 0, 0)),
            pl.BlockSpec((1, RWKV_HEAD_DIM, LANES), lambda g: (prev_tile(g) * SCAN_SUB, 0, 0))],
        out_specs=pl.BlockSpec((nb, PREP_TS, RWKV_WIDTH), lambda g: (0, prev_tile(g), 0)),
        out_shape=jax.ShapeDtypeStruct((nb, s, RWKV_WIDTH), F32),
        scratch_shapes=[pltpu.VMEM((RWKV_HEAD_DIM, SCAN_VR, LANES), F32),
                        pltpu.VMEM((2, SCAN_TENSORS, RWKV_HEAD_DIM, PREP_TS, LANES), F32),
                        pltpu.VMEM((PREP_TS * SCAN_VR, LANES), F32),
                        pltpu.VMEM((nb * RWKV_WIDTH, PREP_TS), F32)],
        compiler_params=_cparams(("arbitrary",)),
        name="wkv_scan",
    )(xt, val, dend, dend)


def _merge_body(x_ref, gate_ref, att_ref, y_ref, g_ref, bonus_ref, conv_ref, convp_ref,
                lnw_ref, lnb_ref, e_ref, cw_ref, wa_ref, wb_ref, wc_ref, wo_ref, out_ref, *, ts):
    i = pl.program_id(1)
    e = e_ref[...]
    y = y_ref[0]
    inv_n = 1.0 / RWKV_HEAD_DIM
    mean = _seg_sum(y, e) * inv_n
    d = y - mean
    var = _seg_sum(d * d, e) * inv_n
    yn = d * lax.rsqrt(var + GN_EPS) * lnw_ref[...] + lnb_ref[...] + bonus_ref[0]
    ob = _bdot(yn * g_ref[0], wb_ref[...])

    cw = CONV_WIDTH
    c = conv_ref[0]
    u = c[:, cw:2 * cw] * c[:, 2 * cw:3 * cw]
    cp = convp_ref[0]
    up = jnp.where(i > 0, cp[:, cw:2 * cw] * cp[:, 2 * cw:3 * cw], 0.0)
    p6 = up[6:7, :]
    p7 = up[7:8, :]
    row = lax.broadcasted_iota(jnp.int32, (ts, 1), 0)
    u1 = jnp.where(row == 0, p7, pltpu.roll(u, 1, axis=0))
    u2 = jnp.where(row == 0, p6, jnp.where(row == 1, p7, pltpu.roll(u, 2, axis=0)))
    taps = cw_ref[...]
    yc = taps[0:1, :] * u2 + taps[1:2, :] * u1 + taps[2:3, :] * u
    oc = _bdot(c[:, 0:cw] * yc, wc_ref[...])

    oa = jnp.dot(att_ref[0], wa_ref[...], preferred_element_type=F32)
    gates = gate_ref[0].astype(F32)
    dm = D_MODEL
    merged = gates[:, 0:dm] * oa + gates[:, dm:2 * dm] * ob + gates[:, 2 * dm:3 * dm] * oc
    out_ref[0] = x_ref[0] + _bdot(merged, wo_ref[...])


def _merge(x, gates, att, y, g, bonus, conv_cols, p, seg, layer, ts=512):
    b, s, d = x.shape
    tok = lambda w: pl.BlockSpec((1, ts, w), lambda bi, i: (bi, i, 0))
    nconv = conv_cols.shape[-1]
    halo = pl.BlockSpec((1, 8, nconv), lambda bi, i: (bi, jnp.maximum(i * (ts // 8) - 1, 0), 0))
    consts = (p["ln_w"], p["ln_b"], seg, p["conv_w"], p["mla_w_o"], p["rwkv_w_o"], p["conv_w_o"], p["w_out"])
    const_specs = [_const_spec(a.shape) if a is seg else _layer_spec(a, layer) for a in consts]
    return pl.pallas_call(
        functools.partial(_merge_body, ts=ts),
        grid=(b, s // ts),
        in_specs=[tok(d), tok(GATE_COLS), tok(att.shape[-1]), tok(RWKV_WIDTH), tok(RWKV_WIDTH), tok(RWKV_WIDTH),
                  tok(nconv), halo] + const_specs,
        out_specs=tok(d),
        out_shape=jax.ShapeDtypeStruct((b, s, d), F32),
        compiler_params=_cparams(("parallel", "parallel")),
        name="branch_merge",
    )(x, gates, att, y, g, bonus, conv_cols, conv_cols, *consts)


def _mlp_body(x_ref, g_ref, wu_ref, wd_ref, o_ref):
    x = x_ref[...]
    ms = jnp.mean(x * x, axis=-1, keepdims=True)
    h = (x * lax.rsqrt(ms + NORM_EPS) * g_ref[...]).astype(BF16)
    up = jnp.dot(h, wu_ref[...], preferred_element_type=F32)
    act = jnp.square(jnp.maximum(up, 0.0)).astype(BF16)
    o_ref[...] = x + jnp.dot(act, wd_ref[...], preferred_element_type=F32)


def _mlp(x2d, p, layer, tm=256):
    t, d = x2d.shape
    row = pl.BlockSpec((tm, d), lambda i: (i, 0))
    consts = (p["mlp_norm"], p["w_up"], p["w_down"])
    return pl.pallas_call(
        _mlp_body,
        grid=(t // tm,),
        in_specs=[row] + [_layer_spec(a, layer) for a in consts],
        out_specs=row,
        out_shape=jax.ShapeDtypeStruct((t, d), F32),
        compiler_params=_cparams(("parallel",)),
        name="mlp",
    )(x2d, *consts)


def _rope_partner_cols(w):
    half = QK_ROPE_DIM // 2
    return jnp.concatenate([-w[..., half:], w[..., :half]], axis=-1)


def _pad_lanes(w, lo, total=HEAD_PAD):
    n = w.shape[-1]
    pad = [(0, 0)] * (w.ndim - 1) + [(lo, total - lo - n)]
    return jnp.pad(w, pad)


def _stacked_params(attn_norm, w_in, mla_q_a_norm, mla_wq_b, mla_kv_a_norm, mla_wkv_b, mla_q_norm, mla_k_norm,
                    mla_w_o, rwkv_mu, rwkv_w0, rwkv_w2, rwkv_a0, rwkv_a2, rwkv_g2, rwkv_k_k, rwkv_k_a, rwkv_r_k,
                    rwkv_ln_w, rwkv_ln_b, rwkv_w_o, rwkv_v1, rwkv_v_mu, rwkv_v0, rwkv_v2, conv_w, conv_w_o, w_out,
                    mlp_norm, w_up, w_down):
    p = {}
    n_layers = w_in.shape[0]
    row = lambda a: a.reshape(n_layers, 1, -1).astype(F32)
    first_layer_zeros = lambda a: jnp.pad(a, [(1, 0)] + [(0, 0)] * (a.ndim - 1))
    w = w_in
    o_mla = GATE_COLS
    o_rwkv = o_mla + MLA_COLS
    o_conv = o_rwkv + RWKV_COLS
    p["attn_norm"] = row(attn_norm)
    p["w_gate"] = w[..., :GATE_COLS].astype(BF16)
    w_kpe = w[..., o_mla + Q_LORA_RANK + KV_LORA_RANK:o_rwkv]
    p["w_mla"] = jnp.concatenate(
        [w[..., o_mla:o_mla + Q_LORA_RANK + KV_LORA_RANK], _pad_lanes(w_kpe, QK_NOPE_DIM),
         _pad_lanes(_rope_partner_cols(w_kpe), QK_NOPE_DIM)], axis=-1).astype(BF16)
    p["w_rwkv"] = jnp.concatenate(
        [w[..., o_rwkv:o_conv], _pad_lanes(first_layer_zeros(rwkv_v1), 0)], axis=-1).astype(BF16)
    p["mu"] = row(jnp.concatenate([rwkv_mu, _pad_lanes(first_layer_zeros(rwkv_v_mu), 0)], axis=-1))
    p["v0"] = row(first_layer_zeros(rwkv_v0))
    p["v2"] = jnp.pad(first_layer_zeros(rwkv_v2), ((0, 0), (0, LANES - MV_LORA), (0, 0)))
    p["w_conv"] = w[..., o_conv:].astype(BF16)

    scale = QK_HEAD_DIM ** -0.5 * float(np.log2(np.e))
    wq = mla_wq_b.reshape(n_layers, Q_LORA_RANK, MLA_HEADS, QK_HEAD_DIM)
    p["wq"] = _pad_lanes(wq, 0).reshape(n_layers, Q_LORA_RANK, -1).astype(BF16)
    p["wqr"] = _pad_lanes(_rope_partner_cols(wq[..., QK_NOPE_DIM:]), QK_NOPE_DIM).reshape(
        n_layers, Q_LORA_RANK, -1).astype(BF16)
    wkv = mla_wkv_b.reshape(n_layers, KV_LORA_RANK, MLA_HEADS, QK_NOPE_DIM + V_HEAD_DIM)
    p["wk"] = _pad_lanes(wkv[..., :QK_NOPE_DIM], 0).reshape(n_layers, KV_LORA_RANK, -1).astype(BF16)
    p["wv"] = wkv[..., QK_NOPE_DIM:].reshape(n_layers, KV_LORA_RANK, -1).astype(BF16)
    p["qan"] = row(mla_q_a_norm)
    p["kvan"] = row(mla_kv_a_norm)
    swap = lambda g: jnp.concatenate([g[..., QK_ROPE_DIM // 2:], g[..., :QK_ROPE_DIM // 2]], axis=-1)
    gq, gk = mla_q_norm * scale, mla_k_norm
    p["gq"] = row(_pad_lanes(gq, 0))
    p["gqr"] = row(_pad_lanes(swap(gq[..., QK_NOPE_DIM:]), QK_NOPE_DIM))
    p["gk"] = row(_pad_lanes(gk, 0))
    p["gkr"] = row(_pad_lanes(swap(gk[..., QK_NOPE_DIM:]), QK_NOPE_DIM))
    p["mla_w_o"] = mla_w_o.astype(BF16)

    p["w0"] = row(rwkv_w0)
    p["w2"] = jnp.pad(rwkv_w2, ((0, 0), (0, AAA_LORA), (0, 0)))
    p["a0"] = row(rwkv_a0)
    p["a2"] = jnp.pad(rwkv_a2, ((0, 0), (DECAY_LORA, 0), (0, 0)))
    p["g2"] = rwkv_g2
    p["k_k"] = row(rwkv_k_k)
    p["k_a"] = row(rwkv_k_a)
    p["r_k"] = row(rwkv_r_k)
    p["ln_w"] = row(rwkv_ln_w)
    p["ln_b"] = row(rwkv_ln_b)
    p["rwkv_w_o"] = rwkv_w_o.astype(BF16)
    p["conv_w"] = conv_w.astype(F32)
    p["conv_w_o"] = conv_w_o.astype(BF16)
    p["w_out"] = w_out.astype(BF16)
    p["mlp_norm"] = row(mlp_norm)
    p["w_up"] = w_up.astype(BF16)
    p["w_down"] = w_down.astype(BF16)
    return p


def _segment_matrix():
    head_of = np.arange(RWKV_WIDTH) // RWKV_HEAD_DIM
    return jnp.asarray(head_of[:, None] == head_of[None, :], BF16)


def _cos_sin_body(ang_ref, cos_ref, sin_ref):
    ang = ang_ref[...]
    cos_ref[...] = jnp.cos(ang)
    sin_ref[...] = jnp.sin(ang)


def _rope_tables(positions):
    half = QK_ROPE_DIM // 2
    freqs = ROPE_THETA ** (-(jnp.arange(half, dtype=F32) * 2.0 / QK_ROPE_DIM))
    ang = (positions.astype(F32)[..., None] * freqs).reshape(-1, LANES)
    cos, sin = pl.pallas_call(
        _cos_sin_body,
        out_shape=[jax.ShapeDtypeStruct(ang.shape, F32)] * 2,
        name="rope_cos_sin",
    )(ang)
    cos = cos.reshape(positions.shape + (half,))
    sin = sin.reshape(positions.shape + (half,))
    ones = jnp.ones(positions.shape + (QK_NOPE_DIM,), F32)
    tail = jnp.ones(positions.shape + (HEAD_PAD - QK_HEAD_DIM,), F32)
    cosf = jnp.concatenate([ones, cos, cos, tail], axis=-1)
    sinf = jnp.concatenate([0 * ones, sin, sin, 0 * tail], axis=-1)
    return cosf, sinf


def kernel(x, positions, attn_norm, w_in, mla_q_a_norm, mla_wq_b, mla_kv_a_norm, mla_wkv_b, mla_q_norm, mla_k_norm, mla_w_o, rwkv_mu, rwkv_w0, rwkv_w2, rwkv_a0, rwkv_a2, rwkv_g2, rwkv_k_k, rwkv_k_a, rwkv_r_k, rwkv_ln_w, rwkv_ln_b, rwkv_w_o, rwkv_v1, rwkv_v_mu, rwkv_v0, rwkv_v2, conv_w, conv_w_o, w_out, mlp_norm, w_up, w_down):
    weights = (attn_norm, w_in, mla_q_a_norm, mla_wq_b, mla_kv_a_norm, mla_wkv_b, mla_q_norm, mla_k_norm, mla_w_o,
               rwkv_mu, rwkv_w0, rwkv_w2, rwkv_a0, rwkv_a2, rwkv_g2, rwkv_k_k, rwkv_k_a, rwkv_r_k, rwkv_ln_w,
               rwkv_ln_b, rwkv_w_o, rwkv_v1, rwkv_v_mu, rwkv_v0, rwkv_v2, conv_w, conv_w_o, w_out, mlp_norm,
               w_up, w_down)
    b, s, d = x.shape
    cosf, sinf = _rope_tables(positions)
    v_first = None
    p = _stacked_params(*weights)
    seg = _segment_matrix()
    for l in range(DEPTH):
        gates, rwkv_cols, conv_cols, q, k, v = _in_proj(x, cosf, sinf, p, l)
        att = _attention(q, k, v)
        prep = _rwkv_prep(rwkv_cols.reshape(b, s, -1), v_first, p, seg, l)
        scan_xt, scan_val, chunk_decay, g_, bonus = prep[:5]
        if l == 0:
            v_first = prep[5]
        dend = chunk_decay.reshape(-1, b, SCAN_SUB, RWKV_HEADS, RWKV_HEAD_DIM).transpose(0, 2, 4, 1, 3)
        dend = jnp.tile(dend.reshape(-1, RWKV_HEAD_DIM, b * RWKV_HEADS), (1, 1, SCAN_VQ))
        y = _wkv_scan(scan_xt, scan_val, dend, b)
        x = _merge(x, gates.reshape(b, s, -1), att, y, g_, bonus, conv_cols.reshape(b, s, -1), p, seg, l)
        x = _mlp(x.reshape(b * s, d), p, l).reshape(b, s, d)
    return x
```

```python
import functools

import jax
import jax.numpy as jnp
import numpy as np
from jax import lax
from jax.experimental import pallas as pl
from jax.experimental.pallas import tpu as pltpu

D_MODEL = 1024
DEPTH = 2
MLA_HEADS = 8
QK_NOPE_DIM = 64
QK_ROPE_DIM = 32
QK_HEAD_DIM = QK_NOPE_DIM + QK_ROPE_DIM
V_HEAD_DIM = 64
Q_LORA_RANK = 384
KV_LORA_RANK = 256
ROPE_THETA = 10000.0
RWKV_HEAD_DIM = 64
RWKV_HEADS = 4
RWKV_WIDTH = RWKV_HEADS * RWKV_HEAD_DIM
DECAY_LORA = 64
AAA_LORA = 64
GATE_LORA = 128
MV_LORA = 32
GN_EPS = 64e-5
CONV_WIDTH = 256
CONV_K = 3
D_FF = 4 * D_MODEL
N_BRANCH = 3
NORM_EPS = 1e-6
GATE_COLS = N_BRANCH * D_MODEL
MLA_COLS = Q_LORA_RANK + KV_LORA_RANK + QK_ROPE_DIM
RWKV_COLS = 3 * RWKV_WIDTH + DECAY_LORA + AAA_LORA + GATE_LORA

LANES = 128
HEAD_PAD = LANES
VT_ROWS = V_HEAD_DIM + 16
MLA_OUT_COLS = Q_LORA_RANK + KV_LORA_RANK + 2 * LANES
VMEM_LIMIT = 56 * 1024 * 1024

F32 = jnp.float32
BF16 = jnp.bfloat16


def _cparams(sem):
    return pltpu.CompilerParams(dimension_semantics=sem, vmem_limit_bytes=VMEM_LIMIT)


def _const_spec(shape):
    nd = len(shape)
    return pl.BlockSpec(shape, lambda *_: (0,) * nd, pipeline_mode=pl.Buffered(1))


def _layer_spec(arr, layer):
    nd = arr.ndim
    return pl.BlockSpec((None,) + arr.shape[1:], lambda *_: (layer,) + (0,) * (nd - 1),
                        pipeline_mode=pl.Buffered(1))


def _bdot(a, b):
    return jnp.dot(a.astype(BF16), b.astype(BF16), preferred_element_type=F32)


def _seg_sum(x, e):
    return sum(jnp.dot(part, e, preferred_element_type=F32) for part in _split_bf16(x, 2))


def _split_bf16(x, parts):
    terms = []
    for _ in range(parts - 1):
        term = x.astype(BF16)
        terms.append(term)
        x = x - term.astype(F32)
    return terms + [x.astype(BF16)]


def _in_proj_body(x_ref, g_ref, w_ref, cos_ref, sin_ref, qan_ref, kvan_ref,
                  wq_ref, wqr_ref, wk_ref, wv_ref, gq_ref, gqr_ref, gk_ref, gkr_ref,
                  gate_ref, rwkv_ref, conv_ref, q_ref, k_ref, v_ref):
    x = x_ref[...]
    ms = jnp.mean(x * x, axis=-1, keepdims=True)
    h = (x * lax.rsqrt(ms + NORM_EPS) * g_ref[...]).astype(BF16)
    o_rwkv = GATE_COLS + MLA_OUT_COLS
    o_conv = o_rwkv + rwkv_ref.shape[-1]
    proj = lambda lo, hi: jnp.dot(h, w_ref[:, lo:hi], preferred_element_type=F32)
    _mla_heads(proj(GATE_COLS, o_rwkv), cos_ref, sin_ref, qan_ref, kvan_ref, wq_ref, wqr_ref, wk_ref, wv_ref,
               gq_ref, gqr_ref, gk_ref, gkr_ref, q_ref, k_ref, v_ref)
    gate_ref[...] = jax.nn.sigmoid(proj(0, GATE_COLS)).astype(gate_ref.dtype)
    rwkv_ref[...] = proj(o_rwkv, o_conv)
    conv_ref[...] = proj(o_conv, w_ref.shape[-1])


def _in_proj(x, cosf, sinf, p, layer):
    b, s, d = x.shape
    tm = ATTN_BLOCK
    per_seq = s // tm
    t = b * s
    row = lambda n: pl.BlockSpec((tm, n), lambda i: (i, 0))
    tok = pl.BlockSpec((1, tm, LANES), lambda i: (i // per_seq, i % per_seq, 0))
    head = lambda w: pl.BlockSpec((1, MLA_HEADS, tm, w), lambda i: (i // per_seq, 0, i % per_seq, 0))
    gain, w_all = p["attn_norm"], p["w_in"]
    mla_consts = (p["qan"], p["kvan"], p["wq"], p["wqr"], p["wk"], p["wv"], p["gq"], p["gqr"], p["gk"], p["gkr"])
    widths = (GATE_COLS, RWKV_COLS + LANES, 3 * CONV_WIDTH)
    assert w_all.shape[-1] == GATE_COLS + MLA_OUT_COLS + widths[1] + widths[2]
    return pl.pallas_call(
        _in_proj_body,
        grid=(t // tm,),
        in_specs=[row(d)] + [_layer_spec(a, layer) for a in (gain, w_all)]
        + [tok, tok] + [_layer_spec(a, layer) for a in mla_consts],
        out_specs=[row(n) for n in widths]
        + [head(HEAD_PAD), head(HEAD_PAD),
           pl.BlockSpec((1, MLA_HEADS, 1, VT_ROWS, tm), lambda i: (i // per_seq, 0, i % per_seq, 0, 0))],
        out_shape=[jax.ShapeDtypeStruct((t, widths[0]), BF16)]
        + [jax.ShapeDtypeStruct((t, n), F32) for n in widths[1:]]
        + [jax.ShapeDtypeStruct((b, MLA_HEADS, s, HEAD_PAD), BF16),
           jax.ShapeDtypeStruct((b, MLA_HEADS, s, HEAD_PAD), BF16),
           jax.ShapeDtypeStruct((b, MLA_HEADS, per_seq, VT_ROWS, tm), BF16)],
        compiler_params=_cparams(("parallel",)),
        name="in_proj",
    )(x.reshape(t, d), gain, w_all, cosf, sinf, *mla_consts)


def _mla_heads(c, cos_ref, sin_ref, qan_ref, kvan_ref, wq_ref, wqr_ref, wk_ref, wv_ref,
               gq_ref, gqr_ref, gk_ref, gkr_ref, q_ref, k_ref, v_ref):
    cq = c[:, :Q_LORA_RANK]
    ckv = c[:, Q_LORA_RANK:Q_LORA_RANK + KV_LORA_RANK]
    kpe = c[:, Q_LORA_RANK + KV_LORA_RANK:Q_LORA_RANK + KV_LORA_RANK + LANES]
    kper = c[:, Q_LORA_RANK + KV_LORA_RANK + LANES:]

    def rms(z, g):
        return (z * lax.rsqrt(jnp.mean(z * z, axis=-1, keepdims=True) + NORM_EPS) * g).astype(BF16)

    cqn = rms(cq, qan_ref[...])
    ckvn = rms(ckv, kvan_ref[...])
    q = jnp.dot(cqn, wq_ref[...], preferred_element_type=F32)
    qr = jnp.dot(cqn, wqr_ref[...], preferred_element_type=F32)
    kn = jnp.dot(ckvn, wk_ref[...], preferred_element_type=F32)
    v = jnp.dot(ckvn, wv_ref[...], preferred_element_type=F32)
    cosf = cos_ref[0]
    sinf = sin_ref[0]
    gqc = gq_ref[...] * cosf
    gqs = gqr_ref[...] * sinf
    gkc = gk_ref[...] * cosf
    kper_s = kper * gkr_ref[...] * sinf
    inv_dim = 1.0 / QK_HEAD_DIM
    vt = v.T
    pad_row = lax.broadcasted_iota(jnp.int32, (VT_ROWS - V_HEAD_DIM, vt.shape[1]), 0)
    ones_rows = jnp.where(pad_row == 0, 1.0, 0.0)
    for h in range(MLA_HEADS):
        sl = slice(h * HEAD_PAD, (h + 1) * HEAD_PAD)
        qh = q[:, sl]
        rq = lax.rsqrt(jnp.sum(qh * qh, axis=-1, keepdims=True) * inv_dim + NORM_EPS)
        q_ref[0, h] = (rq * (qh * gqc + qr[:, sl] * gqs)).astype(q_ref.dtype)
        kh = kn[:, sl] + kpe
        rk = lax.rsqrt(jnp.sum(kh * kh, axis=-1, keepdims=True) * inv_dim + NORM_EPS)
        k_ref[0, h] = (rk * (kh * gkc + kper_s)).astype(k_ref.dtype)
        v_ref[0, h, 0] = jnp.concatenate([vt[h * V_HEAD_DIM:(h + 1) * V_HEAD_DIM, :], ones_rows],
                                         axis=0).astype(v_ref.dtype)


MASK_VALUE = -1e30


ATTN_BLOCK = 256
ATTN_LOOKAHEAD = 4


def _attn_body(q_ref, k_ref, vt_ref, o_ref, m_ref, acc_ref, pend_ref, *, tq):
    i = pl.program_id(1)
    m_ref[...] = jnp.full(m_ref.shape, MASK_VALUE, F32)
    acc_ref[...] = jnp.zeros(acc_ref.shape, F32)
    key_idx = lax.broadcasted_iota(jnp.int32, (tq, tq), 0)
    qry_idx = lax.broadcasted_iota(jnp.int32, (tq, tq), 1)
    causal = key_idx <= qry_idx

    def scores_t(j, h):
        kj = k_ref[0, h, pl.ds(pl.multiple_of(j * tq, tq), tq), :]
        return lax.dot_general(kj, q_ref[0, h], (((1,), (1,)), ((), ())), preferred_element_type=F32)

    for h in range(ATTN_LOOKAHEAD):
        pend_ref[h] = scores_t(0, h)

    def block(j, carry, diagonal):
        pending = [pend_ref[h] for h in range(ATTN_LOOKAHEAD)]
        for h in range(MLA_HEADS):
            st = pending.pop(0)
            ahead = h + ATTN_LOOKAHEAD
            if ahead < MLA_HEADS:
                pending.append(scores_t(j, ahead))
            elif not diagonal:
                pend_ref[ahead - MLA_HEADS] = scores_t(j + 1, ahead - MLA_HEADS)
            if diagonal:
                st = jnp.where(causal, st, MASK_VALUE)
            m_prev = m_ref[h]
            m_new = jnp.maximum(m_prev, jnp.max(st, axis=0, keepdims=True))
            alpha = jnp.exp2(m_prev - m_new)
            pt = jnp.exp2(st - m_new)
            acc_ref[h] = alpha * acc_ref[h] + jnp.dot(vt_ref[0, h, j], pt.astype(BF16), preferred_element_type=F32)
            m_ref[h] = m_new
        return carry

    lax.fori_loop(0, i, functools.partial(block, diagonal=False), 0)
    block(i, 0, diagonal=True)
    for h in range(MLA_HEADS):
        out_t = acc_ref[h, 0:V_HEAD_DIM, :] / acc_ref[h, V_HEAD_DIM:V_HEAD_DIM + 1, :]
        o_ref[0, :, h * V_HEAD_DIM:(h + 1) * V_HEAD_DIM] = out_t.T.astype(o_ref.dtype)


def _attention(q, k, vt):
    b, nh, s, dp = q.shape
    tq = ATTN_BLOCK
    return pl.pallas_call(
        functools.partial(_attn_body, tq=tq),
        grid=(b, s // tq),
        in_specs=[pl.BlockSpec((1, nh, tq, dp), lambda bi, i: (bi, 0, i, 0)),
                  pl.BlockSpec((1, nh, s, dp), lambda bi, i: (bi, 0, 0, 0)),
                  pl.BlockSpec((1, nh, s // tq, VT_ROWS, tq), lambda bi, i: (bi, 0, 0, 0, 0))],
        out_specs=pl.BlockSpec((1, tq, nh * V_HEAD_DIM), lambda bi, i: (bi, i, 0)),
        out_shape=jax.ShapeDtypeStruct((b, s, nh * V_HEAD_DIM), BF16),
        scratch_shapes=[pltpu.VMEM((nh, 1, tq), F32), pltpu.VMEM((nh, VT_ROWS, tq), F32),
                        pltpu.VMEM((ATTN_LOOKAHEAD, tq, tq), F32)],
        compiler_params=_cparams(("parallel", "arbitrary")),
        name="mla_attention",
    )(q, k, vt)


SCAN_VR = 16
SCAN_VQ = RWKV_HEAD_DIM // SCAN_VR
SCAN_TENSORS = 4
SCAN_TC = 64
PREP_TS = LANES


def _rwkv_prep_body(*refs, has_vres, ts, nb):
    if has_vres:
        (x_ref, xp_ref, vf_ref, mu_ref, w0_ref, w2_ref, a0_ref, a2_ref, g2_ref, kk_ref, ka_ref, rk_ref, e_ref, tri_ref,
         v0_ref, v2_ref, xt_ref, val_o, dend_o, g_o, bonus_o, vt_ref) = refs
    else:
        (x_ref, xp_ref, mu_ref, w0_ref, w2_ref, a0_ref, a2_ref, g2_ref, kk_ref, ka_ref, rk_ref, e_ref, tri_ref,
         xt_ref, val_o, dend_o, g_o, bonus_o, vfirst_o, vt_ref) = refs
    i = pl.program_id(0)
    bh = nb * RWKV_HEADS

    def per_batch(b, carry):
        x = x_ref[b]
        prev = jnp.where(i > 0, xp_ref[b][7:8, :], 0.0)
        row = lax.broadcasted_iota(jnp.int32, (ts, 1), 0)
        shifted = jnp.where(row == 0, prev, pltpu.roll(x, 1, axis=0))
        xs = x + (shifted - x) * mu_ref[...]
        wd = RWKV_WIDTH
        r = xs[:, 0:wd]
        k = xs[:, wd:2 * wd]
        v = xs[:, 2 * wd:3 * wd]
        lora_in = xs[:, 3 * wd:3 * wd + LANES]
        xg = xs[:, 3 * wd + LANES:3 * wd + 2 * LANES]
        e = e_ref[...]
        zw = w0_ref[...] + _bdot(jnp.tanh(lora_in), w2_ref[...])
        nz = -zw
        softplus = jnp.maximum(nz, 0.0) + jnp.log(1.0 + jnp.exp(-jnp.abs(nz)))
        log_decay = -jnp.exp(-softplus - 0.5)
        tri = tri_ref[...]
        log_d = sum(jnp.dot(tri, part, preferred_element_type=F32) for part in _split_bf16(log_decay, 3))
        d_incl = jnp.exp(log_d)
        d_prev = jnp.exp(log_d - log_decay)
        d_inv = jnp.exp(-log_d)
        ends = [d_incl[(c + 1) * SCAN_TC - 1:(c + 1) * SCAN_TC, :] for c in range(ts // SCAN_TC)]
        dend_o[0, pl.ds(b, 1), :] = jnp.concatenate(ends, axis=1)
        a_lr = jax.nn.sigmoid(a0_ref[...] + _bdot(lora_in, a2_ref[...]))
        g_o[b] = _bdot(jax.nn.sigmoid(xg), g2_ref[...])
        if has_vres:
            xvs = xs[:, RWKV_COLS:RWKV_COLS + LANES]
            v = v + (vf_ref[b] - v) * jax.nn.sigmoid(v0_ref[...] + _bdot(xvs, v2_ref[...]))
        else:
            vfirst_o[b] = v
        kk = k * kk_ref[...]
        norm = jnp.sqrt(_seg_sum(kk * kk, e))
        kk = kk / jnp.maximum(norm, 1e-12)
        k = k * (1.0 + (a_lr - 1.0) * ka_ref[...])
        bonus_o[b] = _seg_sum(r * k * rk_ref[...], e) * v
        rows = pl.ds(pl.multiple_of(b * wd, wd), wd)
        for idx, val in enumerate((-kk * d_prev, kk * a_lr * d_inv, k * d_inv, r * d_incl)):
            xt_ref[idx, rows, :] = val.T
        vt_ref[rows, :] = v.T
        return carry

    lax.fori_loop(0, nb, per_batch, 0)
    for vr in range(SCAN_VR):
        slab = jnp.concatenate(
            [vt_ref[pl.ds(vq * SCAN_VR + vr, bh, stride=RWKV_HEAD_DIM), :] for vq in range(SCAN_VQ)], axis=0)
        val_o[pl.ds(vr, ts, stride=SCAN_VR), :] = slab.T


def _rwkv_prep(rwkv_cols, v_first, p, seg, layer):
    b, s, n = rwkv_cols.shape
    ts = PREP_TS
    has_vres = v_first is not None
    tok = lambda w: pl.BlockSpec((b, ts, w), lambda i: (0, i, 0))
    halo = pl.BlockSpec((b, 8, n), lambda i: (0, jnp.maximum(i * (ts // 8) - 1, 0), 0))
    step = np.arange(ts)
    same_chunk = (step[:, None] // SCAN_TC) == (step[None, :] // SCAN_TC)
    chunk_tri = jnp.asarray(same_chunk & (step[None, :] <= step[:, None]), BF16)
    layer_consts = [p["mu"], p["w0"], p["w2"], p["a0"], p["a2"], p["g2"], p["k_k"], p["k_a"], p["r_k"]]
    args = [rwkv_cols, rwkv_cols]
    in_specs = [tok(n), halo]
    if has_vres:
        args.append(v_first)
        in_specs.append(tok(RWKV_WIDTH))
    in_specs += [_layer_spec(a, layer) for a in layer_consts] + [_const_spec(seg.shape), _const_spec(chunk_tri.shape)]
    consts = layer_consts + [seg, chunk_tri]
    if has_vres:
        in_specs += [_layer_spec(p["v0"], layer), _layer_spec(p["v2"], layer)]
        consts += [p["v0"], p["v2"]]
    n_tok_out = 2 if has_vres else 3
    return pl.pallas_call(
        functools.partial(_rwkv_prep_body, has_vres=has_vres, ts=ts, nb=b),
        grid=(s // ts,),
        in_specs=in_specs,
        out_specs=[pl.BlockSpec((SCAN_TENSORS, b * RWKV_WIDTH, ts), lambda i: (0, 0, i)),
                   pl.BlockSpec((ts * SCAN_VR, LANES), lambda i: (i, 0)),
                   pl.BlockSpec((1, b, SCAN_SUB * RWKV_WIDTH), lambda i: (i, 0, 0))] + [tok(RWKV_WIDTH)] * n_tok_out,
        out_shape=[jax.ShapeDtypeStruct((SCAN_TENSORS, b * RWKV_WIDTH, s), F32),
                   jax.ShapeDtypeStruct((s * SCAN_VR, LANES), F32),
                   jax.ShapeDtypeStruct((s // ts, b, SCAN_SUB * RWKV_WIDTH), F32)]
        + [jax.ShapeDtypeStruct((b, s, RWKV_WIDTH), F32)] * n_tok_out,
        scratch_shapes=[pltpu.VMEM((b * RWKV_WIDTH, ts), F32)],
        compiler_params=_cparams(("parallel",)),
        name="rwkv_prep",
    )(*args, *consts)


N_ACC = 4
SCAN_SUB = PREP_TS // SCAN_TC


SCAN_UNROLL = 8
SLABS_PER_TILE = SCAN_TENSORS * RWKV_HEAD_DIM
SLABS_PER_BODY = SLABS_PER_TILE * SCAN_UNROLL // PREP_TS
BODIES_PER_TENSOR = RWKV_HEAD_DIM // SLABS_PER_BODY


def _scan_body(xt_ref, v_ref, dprev_ref, dcur_ref, y_ref, s_ref, stage_ref, ych_ref, yt_ref, *, nb):
    g = pl.program_id(0)
    bh = nb * RWKV_HEADS
    wslot = g % 2
    rslot = 1 - wslot

    def stage_slabs(body_idx):
        tensor = body_idx // BODIES_PER_TENSOR
        key0 = (body_idx % BODIES_PER_TENSOR) * SLABS_PER_BODY
        for j in range(SLABS_PER_BODY):
            rows = xt_ref[tensor, pl.ds(key0 + j, bh, stride=RWKV_HEAD_DIM), :]
            stage_ref[wslot, tensor, key0 + j] = jnp.concatenate([rows] * SCAN_VQ, axis=0).T

    @pl.when(g == 0)
    def _():
        s_ref[...] = jnp.zeros_like(s_ref)

        def warm(body_idx, carry):
            stage_slabs(body_idx)
            return carry

        lax.fori_loop(0, SLABS_PER_TILE // SLABS_PER_BODY, warm, 0)

    def tree(acc):
        return (acc[0] + acc[1]) + (acc[2] + acc[3])

    def accumulate(acc, kk, term):
        acc[kk % N_ACC] = term if acc[kk % N_ACC] is None else acc[kk % N_ACC] + term

    def scan_chunk(chunk, dend_ref):
        t0 = chunk * SCAN_TC

        def operand(idx, kk, t):
            return stage_ref[rslot, idx, kk, pl.ds(t, 1), :]

        acc = [None] * N_ACC
        for kk in range(RWKV_HEAD_DIM):
            s_start = s_ref[kk] * dend_ref[0, pl.ds(kk, 1), :]
            s_ref[kk] = s_start
            accumulate(acc, kk, s_start * operand(0, kk, t0))

        def step(t, u):
            op = lambda idx, kk: operand(idx, kk, t)
            t_next = jnp.minimum(t + 1, t0 + SCAN_TC - 1)
            v = v_ref[pl.ds(pl.multiple_of(t * SCAN_VR, SCAN_VR), SCAN_VR), :]
            yacc = [None] * N_ACC
            uacc = [None] * N_ACC
            for kk in range(RWKV_HEAD_DIM):
                s_new = s_ref[kk] + (u * op(1, kk) + v * op(2, kk))
                s_ref[kk] = s_new
                accumulate(yacc, kk, s_new * op(3, kk))
                accumulate(uacc, kk, s_new * operand(0, kk, t_next))
            ych_ref[pl.ds(pl.multiple_of(t * SCAN_VR, SCAN_VR), SCAN_VR), :] = tree(yacc)
            return tree(uacc)

        def body(body_idx, u):
            stage_slabs(chunk * (SCAN_TC // SCAN_UNROLL) + body_idx)
            for j in range(SCAN_UNROLL):
                u = step(t0 + body_idx * SCAN_UNROLL + j, u)
            return u

        lax.fori_loop(0, SCAN_TC // SCAN_UNROLL, body, tree(acc))

    @pl.when(g > 0)
    def _():
        scan_chunk(0, dprev_ref)
        scan_chunk(1, dcur_ref)
        for vr in range(SCAN_VR):
            lanes_by_t = ych_ref[pl.ds(vr, PREP_TS, stride=SCAN_VR), :].T
            for vq in range(SCAN_VQ):
                yt_ref[pl.ds(vq * SCAN_VR + vr, bh, stride=RWKV_HEAD_DIM), :] = lanes_by_t[vq * bh:(vq + 1) * bh, :]
        for b in range(nb):
            y_ref[b] = yt_ref[b * RWKV_WIDTH:(b + 1) * RWKV_WIDTH, :].T


def _wkv_scan(xt, val, dend, nb):
    assert SCAN_SUB == 2
    s = xt.shape[-1]
    n_tiles = s // PREP_TS
    prev_tile = lambda g: jnp.maximum(g - 1, 0)
    return pl.pallas_call(
        functools.partial(_scan_body, nb=nb),
        grid=(n_tiles + 1,),
        in_specs=[
            pl.BlockSpec((SCAN_TENSORS, nb * RWKV_WIDTH, PREP_TS), lambda g: (0, 0, jnp.minimum(g, n_tiles - 1))),
            pl.BlockSpec((PREP_TS * SCAN_VR, LANES), lambda g: (prev_tile(g), 0)),
            pl.BlockSpec((1, RWKV_HEAD_DIM, LANES), lambda g: (jnp.maximum(prev_tile(g) * SCAN_SUB - 1, 0), 0, 0)),
            pl.BlockSpec((1, RWKV_HEAD_DIM, LANES), lambda g: (prev_tile(g) * SCAN_SUB, 0, 0))],
        out_specs=pl.BlockSpec((nb, PREP_TS, RWKV_WIDTH), lambda g: (0, prev_tile(g), 0)),
        out_shape=jax.ShapeDtypeStruct((nb, s, RWKV_WIDTH), F32),
        scratch_shapes=[pltpu.VMEM((RWKV_HEAD_DIM, SCAN_VR, LANES), F32),
                        pltpu.VMEM((2, SCAN_TENSORS, RWKV_HEAD_DIM, PREP_TS, LANES), F32),
                        pltpu.VMEM((PREP_TS * SCAN_VR, LANES), F32),
                        pltpu.VMEM((nb * RWKV_WIDTH, PREP_TS), F32)],
        compiler_params=_cparams(("arbitrary",)),
        name="wkv_scan",
    )(xt, val, dend, dend)


def _merge_body(x_ref, gate_ref, att_ref, y_ref, g_ref, bonus_ref, conv_ref, convp_ref,
                lnw_ref, lnb_ref, e_ref, cw_ref, wa_ref, wb_ref, wc_ref, wo_ref, out_ref, *, ts):
    i = pl.program_id(1)
    e = e_ref[...]
    y = y_ref[0]
    inv_n = 1.0 / RWKV_HEAD_DIM
    mean = _seg_sum(y, e) * inv_n
    d = y - mean
    var = _seg_sum(d * d, e) * inv_n
    yn = d * lax.rsqrt(var + GN_EPS) * lnw_ref[...] + lnb_ref[...] + bonus_ref[0]
    ob = _bdot(yn * g_ref[0], wb_ref[...])

    cw = CONV_WIDTH
    c = conv_ref[0]
    u = c[:, cw:2 * cw] * c[:, 2 * cw:3 * cw]
    cp = convp_ref[0]
    up = jnp.where(i > 0, cp[:, cw:2 * cw] * cp[:, 2 * cw:3 * cw], 0.0)
    p6 = up[6:7, :]
    p7 = up[7:8, :]
    row = lax.broadcasted_iota(jnp.int32, (ts, 1), 0)
    u1 = jnp.where(row == 0, p7, pltpu.roll(u, 1, axis=0))
    u2 = jnp.where(row == 0, p6, jnp.where(row == 1, p7, pltpu.roll(u, 2, axis=0)))
    taps = cw_ref[...]
    yc = taps[0:1, :] * u2 + taps[1:2, :] * u1 + taps[2:3, :] * u
    oc = _bdot(c[:, 0:cw] * yc, wc_ref[...])

    oa = jnp.dot(att_ref[0], wa_ref[...], preferred_element_type=F32)
    gates = gate_ref[0].astype(F32)
    dm = D_MODEL
    merged = gates[:, 0:dm] * oa + gates[:, dm:2 * dm] * ob + gates[:, 2 * dm:3 * dm] * oc
    out_ref[0] = x_ref[0] + _bdot(merged, wo_ref[...])


def _merge(x, gates, att, y, g, bonus, conv_cols, p, seg, layer, ts=512):
    b, s, d = x.shape
    tok = lambda w: pl.BlockSpec((1, ts, w), lambda bi, i: (bi, i, 0))
    nconv = conv_cols.shape[-1]
    halo = pl.BlockSpec((1, 8, nconv), lambda bi, i: (bi, jnp.maximum(i * (ts // 8) - 1, 0), 0))
    consts = (p["ln_w"], p["ln_b"], seg, p["conv_w"], p["mla_w_o"], p["rwkv_w_o"], p["conv_w_o"], p["w_out"])
    const_specs = [_const_spec(a.shape) if a is seg else _layer_spec(a, layer) for a in consts]
    return pl.pallas_call(
        functools.partial(_merge_body, ts=ts),
        grid=(b, s // ts),
        in_specs=[tok(d), tok(GATE_COLS), tok(att.shape[-1]), tok(RWKV_WIDTH), tok(RWKV_WIDTH), tok(RWKV_WIDTH),
                  tok(nconv), halo] + const_specs,
        out_specs=tok(d),
        out_shape=jax.ShapeDtypeStruct((b, s, d), F32),
        compiler_params=_cparams(("parallel", "parallel")),
        name="branch_merge",
    )(x, gates, att, y, g, bonus, conv_cols, conv_cols, *consts)


def _mlp_body(x_ref, g_ref, wu_ref, wd_ref, o_ref):
    x = x_ref[...]
    ms = jnp.mean(x * x, axis=-1, keepdims=True)
    h = (x * lax.rsqrt(ms + NORM_EPS) * g_ref[...]).astype(BF16)
    up = jnp.dot(h, wu_ref[...], preferred_element_type=F32)
    act = jnp.square(jnp.maximum(up, 0.0)).astype(BF16)
    o_ref[...] = x + jnp.dot(act, wd_ref[...], preferred_element_type=F32)


def _mlp(x2d, p, layer, tm=256):
    t, d = x2d.shape
    row = pl.BlockSpec((tm, d), lambda i: (i, 0))
    consts = (p["mlp_norm"], p["w_up"], p["w_down"])
    return pl.pallas_call(
        _mlp_body,
        grid=(t // tm,),
        in_specs=[row] + [_layer_spec(a, layer) for a in consts],
        out_specs=row,
        out_shape=jax.ShapeDtypeStruct((t, d), F32),
        compiler_params=_cparams(("parallel",)),
        name="mlp",
    )(x2d, *consts)


def _rope_partner_cols(w):
    half = QK_ROPE_DIM // 2
    return jnp.concatenate([-w[..., half:], w[..., :half]], axis=-1)


def _pad_lanes(w, lo, total=HEAD_PAD):
    n = w.shape[-1]
    pad = [(0, 0)] * (w.ndim - 1) + [(lo, total - lo - n)]
    return jnp.pad(w, pad)


def _stacked_params(attn_norm, w_in, mla_q_a_norm, mla_wq_b, mla_kv_a_norm, mla_wkv_b, mla_q_norm, mla_k_norm,
                    mla_w_o, rwkv_mu, rwkv_w0, rwkv_w2, rwkv_a0, rwkv_a2, rwkv_g2, rwkv_k_k, rwkv_k_a, rwkv_r_k,
                    rwkv_ln_w, rwkv_ln_b, rwkv_w_o, rwkv_v1, rwkv_v_mu, rwkv_v0, rwkv_v2, conv_w, conv_w_o, w_out,
                    mlp_norm, w_up, w_down):
    p = {}
    n_layers = w_in.shape[0]
    row = lambda a: a.reshape(n_layers, 1, -1).astype(F32)
    first_layer_zeros = lambda a: jnp.pad(a, [(1, 0)] + [(0, 0)] * (a.ndim - 1))
    w = w_in
    o_mla = GATE_COLS
    o_rwkv = o_mla + MLA_COLS
    o_conv = o_rwkv + RWKV_COLS
    p["attn_norm"] = row(attn_norm)
    w_kpe = w[..., o_mla + Q_LORA_RANK + KV_LORA_RANK:o_rwkv]
    p["w_in"] = jnp.concatenate(
        [w[..., :o_mla + Q_LORA_RANK + KV_LORA_RANK], _pad_lanes(w_kpe, QK_NOPE_DIM),
         _pad_lanes(_rope_partner_cols(w_kpe), QK_NOPE_DIM),
         w[..., o_rwkv:o_conv], _pad_lanes(first_layer_zeros(rwkv_v1), 0), w[..., o_conv:]], axis=-1).astype(BF16)
    p["mu"] = row(jnp.concatenate([rwkv_mu, _pad_lanes(first_layer_zeros(rwkv_v_mu), 0)], axis=-1))
    p["v0"] = row(first_layer_zeros(rwkv_v0))
    p["v2"] = jnp.pad(first_layer_zeros(rwkv_v2), ((0, 0), (0, LANES - MV_LORA), (0, 0)))

    scale = QK_HEAD_DIM ** -0.5 * float(np.log2(np.e))
    wq = mla_wq_b.reshape(n_layers, Q_LORA_RANK, MLA_HEADS, QK_HEAD_DIM)
    p["wq"] = _pad_lanes(wq, 0).reshape(n_layers, Q_LORA_RANK, -1).astype(BF16)
    p["wqr"] = _pad_lanes(_rope_partner_cols(wq[..., QK_NOPE_DIM:]), QK_NOPE_DIM).reshape(
        n_layers, Q_LORA_RANK, -1).astype(BF16)
    wkv = mla_wkv_b.reshape(n_layers, KV_LORA_RANK, MLA_HEADS, QK_NOPE_DIM + V_HEAD_DIM)
    p["wk"] = _pad_lanes(wkv[..., :QK_NOPE_DIM], 0).reshape(n_layers, KV_LORA_RANK, -1).astype(BF16)
    p["wv"] = wkv[..., QK_NOPE_DIM:].reshape(n_layers, KV_LORA_RANK, -1).astype(BF16)
    p["qan"] = row(mla_q_a_norm)
    p["kvan"] = row(mla_kv_a_norm)
    swap = lambda g: jnp.concatenate([g[..., QK_ROPE_DIM // 2:], g[..., :QK_ROPE_DIM // 2]], axis=-1)
    gq, gk = mla_q_norm * scale, mla_k_norm
    p["gq"] = row(_pad_lanes(gq, 0))
    p["gqr"] = row(_pad_lanes(swap(gq[..., QK_NOPE_DIM:]), QK_NOPE_DIM))
    p["gk"] = row(_pad_lanes(gk, 0))
    p["gkr"] = row(_pad_lanes(swap(gk[..., QK_NOPE_DIM:]), QK_NOPE_DIM))
    p["mla_w_o"] = mla_w_o.astype(BF16)

    p["w0"] = row(rwkv_w0)
    p["w2"] = jnp.pad(rwkv_w2, ((0, 0), (0, AAA_LORA), (0, 0)))
    p["a0"] = row(rwkv_a0)
    p["a2"] = jnp.pad(rwkv_a2, ((0, 0), (DECAY_LORA, 0), (0, 0)))
    p["g2"] = rwkv_g2
    p["k_k"] = row(rwkv_k_k)
    p["k_a"] = row(rwkv_k_a)
    p["r_k"] = row(rwkv_r_k)
    p["ln_w"] = row(rwkv_ln_w)
    p["ln_b"] = row(rwkv_ln_b)
    p["rwkv_w_o"] = rwkv_w_o.astype(BF16)
    p["conv_w"] = conv_w.astype(F32)
    p["conv_w_o"] = conv_w_o.astype(BF16)
    p["w_out"] = w_out.astype(BF16)
    p["mlp_norm"] = row(mlp_norm)
    p["w_up"] = w_up.astype(BF16)
    p["w_down"] = w_down.astype(BF16)
    return p


def _segment_matrix():
    head_of = np.arange(RWKV_WIDTH) // RWKV_HEAD_DIM
    return jnp.asarray(head_of[:, None] == head_of[None, :], BF16)


def _cos_sin_body(ang_ref, cos_ref, sin_ref):
    ang = ang_ref[...]
    cos_ref[...] = jnp.cos(ang)
    sin_ref[...] = jnp.sin(ang)


def _rope_tables(positions):
    half = QK_ROPE_DIM // 2
    freqs = ROPE_THETA ** (-(jnp.arange(half, dtype=F32) * 2.0 / QK_ROPE_DIM))
    ang = (positions.astype(F32)[..., None] * freqs).reshape(-1, LANES)
    cos, sin = pl.pallas_call(
        _cos_sin_body,
        out_shape=[jax.ShapeDtypeStruct(ang.shape, F32)] * 2,
        name="rope_cos_sin",
    )(ang)
    cos = cos.reshape(positions.shape + (half,))
    sin = sin.reshape(positions.shape + (half,))
    ones = jnp.ones(positions.shape + (QK_NOPE_DIM,), F32)
    tail = jnp.ones(positions.shape + (HEAD_PAD - QK_HEAD_DIM,), F32)
    cosf = jnp.concatenate([ones, cos, cos, tail], axis=-1)
    sinf = jnp.concatenate([0 * ones, sin, sin, 0 * tail], axis=-1)
    return cosf, sinf


def kernel(x, positions, attn_norm, w_in, mla_q_a_norm, mla_wq_b, mla_kv_a_norm, mla_wkv_b, mla_q_norm, mla_k_norm, mla_w_o, rwkv_mu, rwkv_w0, rwkv_w2, rwkv_a0, rwkv_a2, rwkv_g2, rwkv_k_k, rwkv_k_a, rwkv_r_k, rwkv_ln_w, rwkv_ln_b, rwkv_w_o, rwkv_v1, rwkv_v_mu, rwkv_v0, rwkv_v2, conv_w, conv_w_o, w_out, mlp_norm, w_up, w_down):
    weights = (attn_norm, w_in, mla_q_a_norm, mla_wq_b, mla_kv_a_norm, mla_wkv_b, mla_q_norm, mla_k_norm, mla_w_o,
               rwkv_mu, rwkv_w0, rwkv_w2, rwkv_a0, rwkv_a2, rwkv_g2, rwkv_k_k, rwkv_k_a, rwkv_r_k, rwkv_ln_w,
               rwkv_ln_b, rwkv_w_o, rwkv_v1, rwkv_v_mu, rwkv_v0, rwkv_v2, conv_w, conv_w_o, w_out, mlp_norm,
               w_up, w_down)
    b, s, d = x.shape
    cosf, sinf = _rope_tables(positions)
    v_first = None
    p = _stacked_params(*weights)
    seg = _segment_matrix()
    for l in range(DEPTH):
        gates, rwkv_cols, conv_cols, q, k, v = _in_proj(x, cosf, sinf, p, l)
        att = _attention(q, k, v)
        prep = _rwkv_prep(rwkv_cols.reshape(b, s, -1), v_first, p, seg, l)
        scan_xt, scan_val, chunk_decay, g_, bonus = prep[:5]
        if l == 0:
            v_first = prep[5]
        dend = chunk_decay.reshape(-1, b, SCAN_SUB, RWKV_HEADS, RWKV_HEAD_DIM).transpose(0, 2, 4, 1, 3)
        dend = jnp.tile(dend.reshape(-1, RWKV_HEAD_DIM, b * RWKV_HEADS), (1, 1, SCAN_VQ))
        y = _wkv_scan(scan_xt, scan_val, dend, b)
        x = _merge(x, gates.reshape(b, s, -1), att, y, g_, bonus, conv_cols.reshape(b, s, -1), p, seg, l)
        x = _mlp(x.reshape(b * s, d), p, l).reshape(b, s, d)
    return x
```

```python
import functools

import jax
import jax.numpy as jnp
import numpy as np
from jax import lax
from jax.experimental import pallas as pl
from jax.experimental.pallas import tpu as pltpu

D_MODEL = 1024
DEPTH = 2
MLA_HEADS = 8
QK_NOPE_DIM = 64
QK_ROPE_DIM = 32
QK_HEAD_DIM = QK_NOPE_DIM + QK_ROPE_DIM
V_HEAD_DIM = 64
Q_LORA_RANK = 384
KV_LORA_RANK = 256
ROPE_THETA = 10000.0
RWKV_HEAD_DIM = 64
RWKV_HEADS = 4
RWKV_WIDTH = RWKV_HEADS * RWKV_HEAD_DIM
DECAY_LORA = 64
AAA_LORA = 64
GATE_LORA = 128
MV_LORA = 32
GN_EPS = 64e-5
CONV_WIDTH = 256
CONV_K = 3
D_FF = 4 * D_MODEL
N_BRANCH = 3
NORM_EPS = 1e-6
GATE_COLS = N_BRANCH * D_MODEL
MLA_COLS = Q_LORA_RANK + KV_LORA_RANK + QK_ROPE_DIM
RWKV_COLS = 3 * RWKV_WIDTH + DECAY_LORA + AAA_LORA + GATE_LORA

LANES = 128
HEAD_PAD = LANES
VT_ROWS = V_HEAD_DIM + 16
MLA_OUT_COLS = Q_LORA_RANK + KV_LORA_RANK + 2 * LANES
VMEM_LIMIT = 56 * 1024 * 1024

F32 = jnp.float32
BF16 = jnp.bfloat16


def _cparams(sem):
    return pltpu.CompilerParams(dimension_semantics=sem, vmem_limit_bytes=VMEM_LIMIT)


def _const_spec(shape):
    nd = len(shape)
    return pl.BlockSpec(shape, lambda *_: (0,) * nd, pipeline_mode=pl.Buffered(1))


def _layer_spec(arr, layer):
    nd = arr.ndim
    return pl.BlockSpec((None,) + arr.shape[1:], lambda *_: (layer,) + (0,) * (nd - 1),
                        pipeline_mode=pl.Buffered(1))


def _bdot(a, b):
    return jnp.dot(a.astype(BF16), b.astype(BF16), preferred_element_type=F32)


def _seg_sum(x, e):
    return sum(jnp.dot(part, e, preferred_element_type=F32) for part in _split_bf16(x, 2))


def _split_bf16(x, parts):
    terms = []
    for _ in range(parts - 1):
        term = x.astype(BF16)
        terms.append(term)
        x = x - term.astype(F32)
    return terms + [x.astype(BF16)]


def _in_proj_body(x_ref, g_ref, w_ref, cos_ref, sin_ref, qan_ref, kvan_ref,
                  wq_ref, wqr_ref, wk_ref, wv_ref, gq_ref, gqr_ref, gk_ref, gkr_ref,
                  gate_ref, rwkv_ref, conv_ref, q_ref, k_ref, v_ref):
    x = x_ref[...]
    ms = jnp.mean(x * x, axis=-1, keepdims=True)
    h = (x * lax.rsqrt(ms + NORM_EPS) * g_ref[...]).astype(BF16)
    o_rwkv = GATE_COLS + MLA_OUT_COLS
    o_conv = o_rwkv + rwkv_ref.shape[-1]
    proj = lambda lo, hi: jnp.dot(h, w_ref[:, lo:hi], preferred_element_type=F32)
    _mla_heads(proj(GATE_COLS, o_rwkv), cos_ref, sin_ref, qan_ref, kvan_ref, wq_ref, wqr_ref, wk_ref, wv_ref,
               gq_ref, gqr_ref, gk_ref, gkr_ref, q_ref, k_ref, v_ref)
    gate_ref[...] = jax.nn.sigmoid(proj(0, GATE_COLS)).astype(gate_ref.dtype)
    rwkv_ref[...] = proj(o_rwkv, o_conv)
    conv_ref[...] = proj(o_conv, w_ref.shape[-1])


def _in_proj(x, cosf, sinf, p, layer):
    b, s, d = x.shape
    tm = ATTN_BLOCK
    per_seq = s // tm
    t = b * s
    row = lambda n: pl.BlockSpec((tm, n), lambda i: (i, 0))
    tok = pl.BlockSpec((1, tm, LANES), lambda i: (i // per_seq, i % per_seq, 0))
    head = lambda w: pl.BlockSpec((1, MLA_HEADS, tm, w), lambda i: (i // per_seq, 0, i % per_seq, 0))
    gain, w_all = p["attn_norm"], p["w_in"]
    mla_consts = (p["qan"], p["kvan"], p["wq"], p["wqr"], p["wk"], p["wv"], p["gq"], p["gqr"], p["gk"], p["gkr"])
    widths = (GATE_COLS, RWKV_COLS + LANES, 3 * CONV_WIDTH)
    assert w_all.shape[-1] == GATE_COLS + MLA_OUT_COLS + widths[1] + widths[2]
    return pl.pallas_call(
        _in_proj_body,
        grid=(t // tm,),
        in_specs=[row(d)] + [_layer_spec(a, layer) for a in (gain, w_all)]
        + [tok, tok] + [_layer_spec(a, layer) for a in mla_consts],
        out_specs=[row(n) for n in widths]
        + [head(HEAD_PAD), head(HEAD_PAD),
           pl.BlockSpec((1, MLA_HEADS, 1, VT_ROWS, tm), lambda i: (i // per_seq, 0, i % per_seq, 0, 0))],
        out_shape=[jax.ShapeDtypeStruct((t, widths[0]), BF16)]
        + [jax.ShapeDtypeStruct((t, n), F32) for n in widths[1:]]
        + [jax.ShapeDtypeStruct((b, MLA_HEADS, s, HEAD_PAD), BF16),
           jax.ShapeDtypeStruct((b, MLA_HEADS, s, HEAD_PAD), BF16),
           jax.ShapeDtypeStruct((b, MLA_HEADS, per_seq, VT_ROWS, tm), BF16)],
        compiler_params=_cparams(("parallel",)),
        name="in_proj",
    )(x.reshape(t, d), gain, w_all, cosf, sinf, *mla_consts)


def _mla_heads(c, cos_ref, sin_ref, qan_ref, kvan_ref, wq_ref, wqr_ref, wk_ref, wv_ref,
               gq_ref, gqr_ref, gk_ref, gkr_ref, q_ref, k_ref, v_ref):
    cq = c[:, :Q_LORA_RANK]
    ckv = c[:, Q_LORA_RANK:Q_LORA_RANK + KV_LORA_RANK]
    kpe = c[:, Q_LORA_RANK + KV_LORA_RANK:Q_LORA_RANK + KV_LORA_RANK + LANES]
    kper = c[:, Q_LORA_RANK + KV_LORA_RANK + LANES:]

    def rms(z, g):
        return (z * lax.rsqrt(jnp.mean(z * z, axis=-1, keepdims=True) + NORM_EPS) * g).astype(BF16)

    cqn = rms(cq, qan_ref[...])
    ckvn = rms(ckv, kvan_ref[...])
    q = jnp.dot(cqn, wq_ref[...], preferred_element_type=F32)
    qr = jnp.dot(cqn, wqr_ref[...], preferred_element_type=F32)
    kn = jnp.dot(ckvn, wk_ref[...], preferred_element_type=F32)
    v = jnp.dot(ckvn, wv_ref[...], preferred_element_type=F32)
    cosf = cos_ref[0]
    sinf = sin_ref[0]
    gqc = gq_ref[...] * cosf
    gqs = gqr_ref[...] * sinf
    gkc = gk_ref[...] * cosf
    kper_s = kper * gkr_ref[...] * sinf
    inv_dim = 1.0 / QK_HEAD_DIM
    vt = v.T
    pad_row = lax.broadcasted_iota(jnp.int32, (VT_ROWS - V_HEAD_DIM, vt.shape[1]), 0)
    ones_rows = jnp.where(pad_row == 0, 1.0, 0.0)
    for h in range(MLA_HEADS):
        sl = slice(h * HEAD_PAD, (h + 1) * HEAD_PAD)
        qh = q[:, sl]
        rq = lax.rsqrt(jnp.sum(qh * qh, axis=-1, keepdims=True) * inv_dim + NORM_EPS)
        q_ref[0, h] = (rq * (qh * gqc + qr[:, sl] * gqs)).astype(q_ref.dtype)
        kh = kn[:, sl] + kpe
        rk = lax.rsqrt(jnp.sum(kh * kh, axis=-1, keepdims=True) * inv_dim + NORM_EPS)
        k_ref[0, h] = (rk * (kh * gkc + kper_s)).astype(k_ref.dtype)
        v_ref[0, h, 0] = jnp.concatenate([vt[h * V_HEAD_DIM:(h + 1) * V_HEAD_DIM, :], ones_rows],
                                         axis=0).astype(v_ref.dtype)


MASK_VALUE = -1e30


ATTN_BLOCK = 256
ATTN_LOOKAHEAD = 4


def _attn_body(q_ref, k_ref, vt_ref, o_ref, m_ref, acc_ref, pend_ref, *, tq):
    i = pl.program_id(1)
    m_ref[...] = jnp.full(m_ref.shape, MASK_VALUE, F32)
    acc_ref[...] = jnp.zeros(acc_ref.shape, F32)
    key_idx = lax.broadcasted_iota(jnp.int32, (tq, tq), 0)
    qry_idx = lax.broadcasted_iota(jnp.int32, (tq, tq), 1)
    causal = key_idx <= qry_idx

    def scores_t(j, h):
        kj = k_ref[0, h, pl.ds(pl.multiple_of(j * tq, tq), tq), :]
        return lax.dot_general(kj, q_ref[0, h], (((1,), (1,)), ((), ())), preferred_element_type=F32)

    for h in range(ATTN_LOOKAHEAD):
        pend_ref[h] = scores_t(0, h)

    def block(j, carry, diagonal):
        pending = [pend_ref[h] for h in range(ATTN_LOOKAHEAD)]
        for h in range(MLA_HEADS):
            st = pending.pop(0)
            ahead = h + ATTN_LOOKAHEAD
            if ahead < MLA_HEADS:
                pending.append(scores_t(j, ahead))
            elif not diagonal:
                pend_ref[ahead - MLA_HEADS] = scores_t(j + 1, ahead - MLA_HEADS)
            if diagonal:
                st = jnp.where(causal, st, MASK_VALUE)
            m_prev = m_ref[h]
            m_new = jnp.maximum(m_prev, jnp.max(st, axis=0, keepdims=True))
            alpha = jnp.exp2(m_prev - m_new)
            pt = jnp.exp2(st - m_new)
            acc_ref[h] = alpha * acc_ref[h] + jnp.dot(vt_ref[0, h, j], pt.astype(BF16), preferred_element_type=F32)
            m_ref[h] = m_new
        return carry

    lax.fori_loop(0, i, functools.partial(block, diagonal=False), 0)
    block(i, 0, diagonal=True)
    for h in range(MLA_HEADS):
        out_t = acc_ref[h, 0:V_HEAD_DIM, :] / acc_ref[h, V_HEAD_DIM:V_HEAD_DIM + 1, :]
        o_ref[0, :, h * V_HEAD_DIM:(h + 1) * V_HEAD_DIM] = out_t.T.astype(o_ref.dtype)


def _attention(q, k, vt):
    b, nh, s, dp = q.shape
    tq = ATTN_BLOCK
    return pl.pallas_call(
        functools.partial(_attn_body, tq=tq),
        grid=(b, s // tq),
        in_specs=[pl.BlockSpec((1, nh, tq, dp), lambda bi, i: (bi, 0, i, 0)),
                  pl.BlockSpec((1, nh, s, dp), lambda bi, i: (bi, 0, 0, 0)),
                  pl.BlockSpec((1, nh, s // tq, VT_ROWS, tq), lambda bi, i: (bi, 0, 0, 0, 0))],
        out_specs=pl.BlockSpec((1, tq, nh * V_HEAD_DIM), lambda bi, i: (bi, i, 0)),
        out_shape=jax.ShapeDtypeStruct((b, s, nh * V_HEAD_DIM), BF16),
        scratch_shapes=[pltpu.VMEM((nh, 1, tq), F32), pltpu.VMEM((nh, VT_ROWS, tq), F32),
                        pltpu.VMEM((ATTN_LOOKAHEAD, tq, tq), F32)],
        compiler_params=_cparams(("parallel", "arbitrary")),
        name="mla_attention",
    )(q, k, vt)


SCAN_VR = 16
SCAN_VQ = RWKV_HEAD_DIM // SCAN_VR
SCAN_TENSORS = 4
SCAN_TC = 64
PREP_TS = LANES


def _rwkv_prep_body(*refs, has_vres, ts, nb):
    if has_vres:
        (x_ref, xp_ref, vf_ref, mu_ref, w0_ref, w2_ref, a0_ref, a2_ref, g2_ref, kk_ref, ka_ref, rk_ref, e_ref, tri_ref,
         v0_ref, v2_ref, xt_ref, val_o, dend_o, g_o, bonus_o, vt_ref) = refs
    else:
        (x_ref, xp_ref, mu_ref, w0_ref, w2_ref, a0_ref, a2_ref, g2_ref, kk_ref, ka_ref, rk_ref, e_ref, tri_ref,
         xt_ref, val_o, dend_o, g_o, bonus_o, vfirst_o, vt_ref) = refs
    i = pl.program_id(0)
    bh = nb * RWKV_HEADS

    def per_batch(b, carry):
        x = x_ref[b]
        prev = jnp.where(i > 0, xp_ref[b][7:8, :], 0.0)
        row = lax.broadcasted_iota(jnp.int32, (ts, 1), 0)
        shifted = jnp.where(row == 0, prev, pltpu.roll(x, 1, axis=0))
        xs = x + (shifted - x) * mu_ref[...]
        wd = RWKV_WIDTH
        r = xs[:, 0:wd]
        k = xs[:, wd:2 * wd]
        v = xs[:, 2 * wd:3 * wd]
        lora_in = xs[:, 3 * wd:3 * wd + LANES]
        xg = xs[:, 3 * wd + LANES:3 * wd + 2 * LANES]
        e = e_ref[...]
        zw = w0_ref[...] + _bdot(jnp.tanh(lora_in), w2_ref[...])
        nz = -zw
        softplus = jnp.maximum(nz, 0.0) + jnp.log(1.0 + jnp.exp(-jnp.abs(nz)))
        log_decay = -jnp.exp(-softplus - 0.5)
        tri = tri_ref[...]
        log_d = sum(jnp.dot(tri, part, preferred_element_type=F32) for part in _split_bf16(log_decay, 3))
        d_incl = jnp.exp(log_d)
        d_prev = jnp.exp(log_d - log_decay)
        d_inv = jnp.exp(-log_d)
        ends = [d_incl[(c + 1) * SCAN_TC - 1:(c + 1) * SCAN_TC, :] for c in range(ts // SCAN_TC)]
        dend_o[0, pl.ds(b, 1), :] = jnp.concatenate(ends, axis=1)
        a_lr = jax.nn.sigmoid(a0_ref[...] + _bdot(lora_in, a2_ref[...]))
        g_o[b] = _bdot(jax.nn.sigmoid(xg), g2_ref[...])
        if has_vres:
            xvs = xs[:, RWKV_COLS:RWKV_COLS + LANES]
            v = v + (vf_ref[b] - v) * jax.nn.sigmoid(v0_ref[...] + _bdot(xvs, v2_ref[...]))
        else:
            vfirst_o[b] = v
        kk = k * kk_ref[...]
        norm = jnp.sqrt(_seg_sum(kk * kk, e))
        kk = kk / jnp.maximum(norm, 1e-12)
        k = k * (1.0 + (a_lr - 1.0) * ka_ref[...])
        bonus_o[b] = _seg_sum(r * k * rk_ref[...], e) * v
        rows = pl.ds(pl.multiple_of(b * wd, wd), wd)
        for idx, val in enumerate((-kk * d_prev, kk * a_lr * d_inv, k * d_inv, r * d_incl)):
            xt_ref[idx, rows, :] = val.T
        vt_ref[rows, :] = v.T
        return carry

    lax.fori_loop(0, nb, per_batch, 0)
    for vr in range(SCAN_VR):
        slab = jnp.concatenate(
            [vt_ref[pl.ds(vq * SCAN_VR + vr, bh, stride=RWKV_HEAD_DIM), :] for vq in range(SCAN_VQ)], axis=0)
        val_o[pl.ds(vr, ts, stride=SCAN_VR), :] = slab.T


def _rwkv_prep(rwkv_cols, v_first, p, seg, layer):
    b, s, n = rwkv_cols.shape
    ts = PREP_TS
    has_vres = v_first is not None
    tok = lambda w: pl.BlockSpec((b, ts, w), lambda i: (0, i, 0))
    halo = pl.BlockSpec((b, 8, n), lambda i: (0, jnp.maximum(i * (ts // 8) - 1, 0), 0))
    step = np.arange(ts)
    same_chunk = (step[:, None] // SCAN_TC) == (step[None, :] // SCAN_TC)
    chunk_tri = jnp.asarray(same_chunk & (step[None, :] <= step[:, None]), BF16)
    layer_consts = [p["mu"], p["w0"], p["w2"], p["a0"], p["a2"], p["g2"], p["k_k"], p["k_a"], p["r_k"]]
    args = [rwkv_cols, rwkv_cols]
    in_specs = [tok(n), halo]
    if has_vres:
        args.append(v_first)
        in_specs.append(tok(RWKV_WIDTH))
    in_specs += [_layer_spec(a, layer) for a in layer_consts] + [_const_spec(seg.shape), _const_spec(chunk_tri.shape)]
    consts = layer_consts + [seg, chunk_tri]
    if has_vres:
        in_specs += [_layer_spec(p["v0"], layer), _layer_spec(p["v2"], layer)]
        consts += [p["v0"], p["v2"]]
    n_tok_out = 2 if has_vres else 3
    return pl.pallas_call(
        functools.partial(_rwkv_prep_body, has_vres=has_vres, ts=ts, nb=b),
        grid=(s // ts,),
        in_specs=in_specs,
        out_specs=[pl.BlockSpec((SCAN_TENSORS, b * RWKV_WIDTH, ts), lambda i: (0, 0, i)),
                   pl.BlockSpec((ts * SCAN_VR, LANES), lambda i: (i, 0)),
                   pl.BlockSpec((1, b, SCAN_SUB * RWKV_WIDTH), lambda i: (i, 0, 0))] + [tok(RWKV_WIDTH)] * n_tok_out,
        out_shape=[jax.ShapeDtypeStruct((SCAN_TENSORS, b * RWKV_WIDTH, s), F32),
                   jax.ShapeDtypeStruct((s * SCAN_VR, LANES), F32),
                   jax.ShapeDtypeStruct((s // ts, b, SCAN_SUB * RWKV_WIDTH), F32)]
        + [jax.ShapeDtypeStruct((b, s, RWKV_WIDTH), F32)] * n_tok_out,
        scratch_shapes=[pltpu.VMEM((b * RWKV_WIDTH, ts), F32)],
        compiler_params=_cparams(("parallel",)),
        name="rwkv_prep",
    )(*args, *consts)


N_ACC = 4
SCAN_SUB = PREP_TS // SCAN_TC


SCAN_UNROLL = 16
SLABS_PER_TILE = SCAN_TENSORS * RWKV_HEAD_DIM
SLABS_PER_BODY = SLABS_PER_TILE * SCAN_UNROLL // PREP_TS
BODIES_PER_TENSOR = RWKV_HEAD_DIM // SLABS_PER_BODY


def _scan_body(xt_ref, v_ref, dprev_ref, dcur_ref, y_ref, s_ref, stage_ref, ych_ref, yt_ref, *, nb):
    g = pl.program_id(0)
    bh = nb * RWKV_HEADS
    wslot = g % 2
    rslot = 1 - wslot

    def stage_slabs(body_idx):
        tensor = body_idx // BODIES_PER_TENSOR
        key0 = (body_idx % BODIES_PER_TENSOR) * SLABS_PER_BODY
        for j in range(SLABS_PER_BODY):
            rows = xt_ref[tensor, pl.ds(key0 + j, bh, stride=RWKV_HEAD_DIM), :]
            stage_ref[wslot, tensor, key0 + j] = jnp.concatenate([rows] * SCAN_VQ, axis=0).T

    @pl.when(g == 0)
    def _():
        s_ref[...] = jnp.zeros_like(s_ref)

        def warm(body_idx, carry):
            stage_slabs(body_idx)
            return carry

        lax.fori_loop(0, SLABS_PER_TILE // SLABS_PER_BODY, warm, 0)

    def tree(acc):
        return (acc[0] + acc[1]) + (acc[2] + acc[3])

    def accumulate(acc, kk, term):
        acc[kk % N_ACC] = term if acc[kk % N_ACC] is None else acc[kk % N_ACC] + term

    def scan_chunk(chunk, dend_ref):
        t0 = chunk * SCAN_TC

        def operand(idx, kk, t):
            return stage_ref[rslot, idx, kk, pl.ds(t, 1), :]

        acc = [None] * N_ACC
        for kk in range(RWKV_HEAD_DIM):
            s_start = s_ref[kk] * dend_ref[0, pl.ds(kk, 1), :]
            s_ref[kk] = s_start
            accumulate(acc, kk, s_start * operand(0, kk, t0))

        def step(t, u):
            op = lambda idx, kk: operand(idx, kk, t)
            t_next = jnp.minimum(t + 1, t0 + SCAN_TC - 1)
            v = v_ref[pl.ds(pl.multiple_of(t * SCAN_VR, SCAN_VR), SCAN_VR), :]
            yacc = [None] * N_ACC
            uacc = [None] * N_ACC
            for kk in range(RWKV_HEAD_DIM):
                s_new = s_ref[kk] + (u * op(1, kk) + v * op(2, kk))
                s_ref[kk] = s_new
                accumulate(yacc, kk, s_new * op(3, kk))
                accumulate(uacc, kk, s_new * operand(0, kk, t_next))
            ych_ref[pl.ds(pl.multiple_of(t * SCAN_VR, SCAN_VR), SCAN_VR), :] = tree(yacc)
            return tree(uacc)

        def body(body_idx, u):
            stage_slabs(chunk * (SCAN_TC // SCAN_UNROLL) + body_idx)
            for j in range(SCAN_UNROLL):
                u = step(t0 + body_idx * SCAN_UNROLL + j, u)
            return u

        lax.fori_loop(0, SCAN_TC // SCAN_UNROLL, body, tree(acc))

    @pl.when(g > 0)
    def _():
        scan_chunk(0, dprev_ref)
        scan_chunk(1, dcur_ref)
        for vr in range(SCAN_VR):
            lanes_by_t = ych_ref[pl.ds(vr, PREP_TS, stride=SCAN_VR), :].T
            for vq in range(SCAN_VQ):
                yt_ref[pl.ds(vq * SCAN_VR + vr, bh, stride=RWKV_HEAD_DIM), :] = lanes_by_t[vq * bh:(vq + 1) * bh, :]
        for b in range(nb):
            y_ref[b] = yt_ref[b * RWKV_WIDTH:(b + 1) * RWKV_WIDTH, :].T


def _wkv_scan(xt, val, dend, nb):
    assert SCAN_SUB == 2
    s = xt.shape[-1]
    n_tiles = s // PREP_TS
    prev_tile = lambda g: jnp.maximum(g - 1, 0)
    return pl.pallas_call(
        functools.partial(_scan_body, nb=nb),
        grid=(n_tiles + 1,),
        in_specs=[
            pl.BlockSpec((SCAN_TENSORS, nb * RWKV_WIDTH, PREP_TS), lambda g: (0, 0, jnp.minimum(g, n_tiles - 1))),
            pl.BlockSpec((PREP_TS * SCAN_VR, LANES), lambda g: (prev_tile(g), 0)),
            pl.BlockSpec((1, RWKV_HEAD_DIM, LANES), lambda g: (jnp.maximum(prev_tile(g) * SCAN_SUB - 1, 0), 0, 0)),
            pl.BlockSpec((1, RWKV_HEAD_DIM, LANES), lambda g: (prev_tile(g) * SCAN_SUB, 0, 0))],
        out_specs=pl.BlockSpec((nb, PREP_TS, RWKV_WIDTH), lambda g: (0, prev_tile(g), 0)),
        out_shape=jax.ShapeDtypeStruct((nb, s, RWKV_WIDTH), F32),
        scratch_shapes=[pltpu.VMEM((RWKV_HEAD_DIM, SCAN_VR, LANES), F32),
                        pltpu.VMEM((2, SCAN_TENSORS, RWKV_HEAD_DIM, PREP_TS, LANES), F32),
                        pltpu.VMEM((PREP_TS * SCAN_VR, LANES), F32),
                        pltpu.VMEM((nb * RWKV_WIDTH, PREP_TS), F32)],
        compiler_params=_cparams(("arbitrary",)),
        name="wkv_scan",
    )(xt, val, dend, dend)


def _merge_body(x_ref, gate_ref, att_ref, y_ref, g_ref, bonus_ref, conv_ref, convp_ref,
                lnw_ref, lnb_ref, e_ref, cw_ref, wa_ref, wb_ref, wc_ref, wo_ref, out_ref, *, ts):
    i = pl.program_id(1)
    e = e_ref[...]
    y = y_ref[0]
    inv_n = 1.0 / RWKV_HEAD_DIM
    mean = _seg_sum(y, e) * inv_n
    d = y - mean
    var = _seg_sum(d * d, e) * inv_n
    yn = d * lax.rsqrt(var + GN_EPS) * lnw_ref[...] + lnb_ref[...] + bonus_ref[0]
    ob = _bdot(yn * g_ref[0], wb_ref[...])

    cw = CONV_WIDTH
    c = conv_ref[0]
    u = c[:, cw:2 * cw] * c[:, 2 * cw:3 * cw]
    cp = convp_ref[0]
    up = jnp.where(i > 0, cp[:, cw:2 * cw] * cp[:, 2 * cw:3 * cw], 0.0)
    p6 = up[6:7, :]
    p7 = up[7:8, :]
    row = lax.broadcasted_iota(jnp.int32, (ts, 1), 0)
    u1 = jnp.where(row == 0, p7, pltpu.roll(u, 1, axis=0))
    u2 = jnp.where(row == 0, p6, jnp.where(row == 1, p7, pltpu.roll(u, 2, axis=0)))
    taps = cw_ref[...]
    yc = taps[0:1, :] * u2 + taps[1:2, :] * u1 + taps[2:3, :] * u
    oc = _bdot(c[:, 0:cw] * yc, wc_ref[...])

    oa = jnp.dot(att_ref[0], wa_ref[...], preferred_element_type=F32)
    gates = gate_ref[0].astype(F32)
    dm = D_MODEL
    merged = gates[:, 0:dm] * oa + gates[:, dm:2 * dm] * ob + gates[:, 2 * dm:3 * dm] * oc
    out_ref[0] = x_ref[0] + _bdot(merged, wo_ref[...])


def _merge(x, gates, att, y, g, bonus, conv_cols, p, seg, layer, ts=512):
    b, s, d = x.shape
    tok = lambda w: pl.BlockSpec((1, ts, w), lambda bi, i: (bi, i, 0))
    nconv = conv_cols.shape[-1]
    halo = pl.BlockSpec((1, 8, nconv), lambda bi, i: (bi, jnp.maximum(i * (ts // 8) - 1, 0), 0))
    consts = (p["ln_w"], p["ln_b"], seg, p["conv_w"], p["mla_w_o"], p["rwkv_w_o"], p["conv_w_o"], p["w_out"])
    const_specs = [_const_spec(a.shape) if a is seg else _layer_spec(a, layer) for a in consts]
    return pl.pallas_call(
        functools.partial(_merge_body, ts=ts),
        grid=(b, s // ts),
        in_specs=[tok(d), tok(GATE_COLS), tok(att.shape[-1]), tok(RWKV_WIDTH), tok(RWKV_WIDTH), tok(RWKV_WIDTH),
                  tok(nconv), halo] + const_specs,
        out_specs=tok(d),
        out_shape=jax.ShapeDtypeStruct((b, s, d), F32),
        compiler_params=_cparams(("parallel", "parallel")),
        name="branch_merge",
    )(x, gates, att, y, g, bonus, conv_cols, conv_cols, *consts)


def _mlp_body(x_ref, g_ref, wu_ref, wd_ref, o_ref):
    x = x_ref[...]
    ms = jnp.mean(x * x, axis=-1, keepdims=True)
    h = (x * lax.rsqrt(ms + NORM_EPS) * g_ref[...]).astype(BF16)
    up = jnp.dot(h, wu_ref[...], preferred_element_type=F32)
    act = jnp.square(jnp.maximum(up, 0.0)).astype(BF16)
    o_ref[...] = x + jnp.dot(act, wd_ref[...], preferred_element_type=F32)


def _mlp(x2d, p, layer, tm=256):
    t, d = x2d.shape
    row = pl.BlockSpec((tm, d), lambda i: (i, 0))
    consts = (p["mlp_norm"], p["w_up"], p["w_down"])
    return pl.pallas_call(
        _mlp_body,
        grid=(t // tm,),
        in_specs=[row] + [_layer_spec(a, layer) for a in consts],
        out_specs=row,
        out_shape=jax.ShapeDtypeStruct((t, d), F32),
        compiler_params=_cparams(("parallel",)),
        name="mlp",
    )(x2d, *consts)


def _rope_partner_cols(w):
    half = QK_ROPE_DIM // 2
    return jnp.concatenate([-w[..., half:], w[..., :half]], axis=-1)


def _pad_lanes(w, lo, total=HEAD_PAD):
    n = w.shape[-1]
    pad = [(0, 0)] * (w.ndim - 1) + [(lo, total - lo - n)]
    return jnp.pad(w, pad)


def _stacked_params(attn_norm, w_in, mla_q_a_norm, mla_wq_b, mla_kv_a_norm, mla_wkv_b, mla_q_norm, mla_k_norm,
                    mla_w_o, rwkv_mu, rwkv_w0, rwkv_w2, rwkv_a0, rwkv_a2, rwkv_g2, rwkv_k_k, rwkv_k_a, rwkv_r_k,
                    rwkv_ln_w, rwkv_ln_b, rwkv_w_o, rwkv_v1, rwkv_v_mu, rwkv_v0, rwkv_v2, conv_w, conv_w_o, w_out,
                    mlp_norm, w_up, w_down):
    p = {}
    n_layers = w_in.shape[0]
    row = lambda a: a.reshape(n_layers, 1, -1).astype(F32)
    first_layer_zeros = lambda a: jnp.pad(a, [(1, 0)] + [(0, 0)] * (a.ndim - 1))
    w = w_in
    o_mla = GATE_COLS
    o_rwkv = o_mla + MLA_COLS
    o_conv = o_rwkv + RWKV_COLS
    p["attn_norm"] = row(attn_norm)
    w_kpe = w[..., o_mla + Q_LORA_RANK + KV_LORA_RANK:o_rwkv]
    p["w_in"] = jnp.concatenate(
        [w[..., :o_mla + Q_LORA_RANK + KV_LORA_RANK], _pad_lanes(w_kpe, QK_NOPE_DIM),
         _pad_lanes(_rope_partner_cols(w_kpe), QK_NOPE_DIM),
         w[..., o_rwkv:o_conv], _pad_lanes(first_layer_zeros(rwkv_v1), 0), w[..., o_conv:]], axis=-1).astype(BF16)
    p["mu"] = row(jnp.concatenate([rwkv_mu, _pad_lanes(first_layer_zeros(rwkv_v_mu), 0)], axis=-1))
    p["v0"] = row(first_layer_zeros(rwkv_v0))
    p["v2"] = jnp.pad(first_layer_zeros(rwkv_v2), ((0, 0), (0, LANES - MV_LORA), (0, 0)))

    scale = QK_HEAD_DIM ** -0.5 * float(np.log2(np.e))
    wq = mla_wq_b.reshape(n_layers, Q_LORA_RANK, MLA_HEADS, QK_HEAD_DIM)
    p["wq"] = _pad_lanes(wq, 0).reshape(n_layers, Q_LORA_RANK, -1).astype(BF16)
    p["wqr"] = _pad_lanes(_rope_partner_cols(wq[..., QK_NOPE_DIM:]), QK_NOPE_DIM).reshape(
        n_layers, Q_LORA_RANK, -1).astype(BF16)
    wkv = mla_wkv_b.reshape(n_layers, KV_LORA_RANK, MLA_HEADS, QK_NOPE_DIM + V_HEAD_DIM)
    p["wk"] = _pad_lanes(wkv[..., :QK_NOPE_DIM], 0).reshape(n_layers, KV_LORA_RANK, -1).astype(BF16)
    p["wv"] = wkv[..., QK_NOPE_DIM:].reshape(n_layers, KV_LORA_RANK, -1).astype(BF16)
    p["qan"] = row(mla_q_a_norm)
    p["kvan"] = row(mla_kv_a_norm)
    swap = lambda g: jnp.concatenate([g[..., QK_ROPE_DIM // 2:], g[..., :QK_ROPE_DIM // 2]], axis=-1)
    gq, gk = mla_q_norm * scale, mla_k_norm
    p["gq"] = row(_pad_lanes(gq, 0))
    p["gqr"] = row(_pad_lanes(swap(gq[..., QK_NOPE_DIM:]), QK_NOPE_DIM))
    p["gk"] = row(_pad_lanes(gk, 0))
    p["gkr"] = row(_pad_lanes(swap(gk[..., QK_NOPE_DIM:]), QK_NOPE_DIM))
    p["mla_w_o"] = mla_w_o.astype(BF16)

    p["w0"] = row(rwkv_w0)
    p["w2"] = jnp.pad(rwkv_w2, ((0, 0), (0, AAA_LORA), (0, 0)))
    p["a0"] = row(rwkv_a0)
    p["a2"] = jnp.pad(rwkv_a2, ((0, 0), (DECAY_LORA, 0), (0, 0)))
    p["g2"] = rwkv_g2
    p["k_k"] = row(rwkv_k_k)
    p["k_a"] = row(rwkv_k_a)
    p["r_k"] = row(rwkv_r_k)
    p["ln_w"] = row(rwkv_ln_w)
    p["ln_b"] = row(rwkv_ln_b)
    p["rwkv_w_o"] = rwkv_w_o.astype(BF16)
    p["conv_w"] = conv_w.astype(F32)
    p["conv_w_o"] = conv_w_o.astype(BF16)
    p["w_out"] = w_out.astype(BF16)
    p["mlp_norm"] = row(mlp_norm)
    p["w_up"] = w_up.astype(BF16)
    p["w_down"] = w_down.astype(BF16)
    return p


def _segment_matrix():
    head_of = np.arange(RWKV_WIDTH) // RWKV_HEAD_DIM
    return jnp.asarray(head_of[:, None] == head_of[None, :], BF16)


def _cos_sin_body(ang_ref, cos_ref, sin_ref):
    ang = ang_ref[...]
    cos_ref[...] = jnp.cos(ang)
    sin_ref[...] = jnp.sin(ang)


def _rope_tables(positions):
    half = QK_ROPE_DIM // 2
    freqs = ROPE_THETA ** (-(jnp.arange(half, dtype=F32) * 2.0 / QK_ROPE_DIM))
    ang = (positions.astype(F32)[..., None] * freqs).reshape(-1, LANES)
    cos, sin = pl.pallas_call(
        _cos_sin_body,
        out_shape=[jax.ShapeDtypeStruct(ang.shape, F32)] * 2,
        name="rope_cos_sin",
    )(ang)
    cos = cos.reshape(positions.shape + (half,))
    sin = sin.reshape(positions.shape + (half,))
    ones = jnp.ones(positions.shape + (QK_NOPE_DIM,), F32)
    tail = jnp.ones(positions.shape + (HEAD_PAD - QK_HEAD_DIM,), F32)
    cosf = jnp.concatenate([ones, cos, cos, tail], axis=-1)
    sinf = jnp.concatenate([0 * ones, sin, sin, 0 * tail], axis=-1)
    return cosf, sinf


def kernel(x, positions, attn_norm, w_in, mla_q_a_norm, mla_wq_b, mla_kv_a_norm, mla_wkv_b, mla_q_norm, mla_k_norm, mla_w_o, rwkv_mu, rwkv_w0, rwkv_w2, rwkv_a0, rwkv_a2, rwkv_g2, rwkv_k_k, rwkv_k_a, rwkv_r_k, rwkv_ln_w, rwkv_ln_b, rwkv_w_o, rwkv_v1, rwkv_v_mu, rwkv_v0, rwkv_v2, conv_w, conv_w_o, w_out, mlp_norm, w_up, w_down):
    weights = (attn_norm, w_in, mla_q_a_norm, mla_wq_b, mla_kv_a_norm, mla_wkv_b, mla_q_norm, mla_k_norm, mla_w_o,
               rwkv_mu, rwkv_w0, rwkv_w2, rwkv_a0, rwkv_a2, rwkv_g2, rwkv_k_k, rwkv_k_a, rwkv_r_k, rwkv_ln_w,
               rwkv_ln_b, rwkv_w_o, rwkv_v1, rwkv_v_mu, rwkv_v0, rwkv_v2, conv_w, conv_w_o, w_out, mlp_norm,
               w_up, w_down)
    b, s, d = x.shape
    cosf, sinf = _rope_tables(positions)
    v_first = None
    p = _stacked_params(*weights)
    seg = _segment_matrix()
    for l in range(DEPTH):
        gates, rwkv_cols, conv_cols, q, k, v = _in_proj(x, cosf, sinf, p, l)
        att = _attention(q, k, v)
        prep = _rwkv_prep(rwkv_cols.reshape(b, s, -1), v_first, p, seg, l)
        scan_xt, scan_val, chunk_decay, g_, bonus = prep[:5]
        if l == 0:
            v_first = prep[5]
        dend = chunk_decay.reshape(-1, b, SCAN_SUB, RWKV_HEADS, RWKV_HEAD_DIM).transpose(0, 2, 4, 1, 3)
        dend = jnp.tile(dend.reshape(-1, RWKV_HEAD_DIM, b * RWKV_HEADS), (1, 1, SCAN_VQ))
        y = _wkv_scan(scan_xt, scan_val, dend, b)
        x = _merge(x, gates.reshape(b, s, -1), att, y, g_, bonus, conv_cols.reshape(b, s, -1), p, seg, l)
        x = _mlp(x.reshape(b * s, d), p, l).reshape(b, s, d)
    return x
```

```python
import functools

import jax
import jax.numpy as jnp
import numpy as np
from jax import lax
from jax.experimental import pallas as pl
from jax.experimental.pallas import tpu as pltpu

D_MODEL = 1024
DEPTH = 2
MLA_HEADS = 8
QK_NOPE_DIM = 64
QK_ROPE_DIM = 32
QK_HEAD_DIM = QK_NOPE_DIM + QK_ROPE_DIM
V_HEAD_DIM = 64
Q_LORA_RANK = 384
KV_LORA_RANK = 256
ROPE_THETA = 10000.0
RWKV_HEAD_DIM = 64
RWKV_HEADS = 4
RWKV_WIDTH = RWKV_HEADS * RWKV_HEAD_DIM
DECAY_LORA = 64
AAA_LORA = 64
GATE_LORA = 128
MV_LORA = 32
GN_EPS = 64e-5
CONV_WIDTH = 256
CONV_K = 3
D_FF = 4 * D_MODEL
N_BRANCH = 3
NORM_EPS = 1e-6
GATE_COLS = N_BRANCH * D_MODEL
MLA_COLS = Q_LORA_RANK + KV_LORA_RANK + QK_ROPE_DIM
RWKV_COLS = 3 * RWKV_WIDTH + DECAY_LORA + AAA_LORA + GATE_LORA

LANES = 128
HEAD_PAD = LANES
BF16_SUBLANES = 16
VT_ROWS = V_HEAD_DIM + BF16_SUBLANES
MLA_OUT_COLS = Q_LORA_RANK + KV_LORA_RANK + 2 * LANES
VMEM_LIMIT = 56 * 1024 * 1024

F32 = jnp.float32
BF16 = jnp.bfloat16


def _cparams(sem):
    return pltpu.CompilerParams(dimension_semantics=sem, vmem_limit_bytes=VMEM_LIMIT)


def _const_spec(shape):
    nd = len(shape)
    return pl.BlockSpec(shape, lambda *_: (0,) * nd, pipeline_mode=pl.Buffered(1))


def _layer_spec(arr, layer):
    nd = arr.ndim
    return pl.BlockSpec((None,) + arr.shape[1:], lambda *_: (layer,) + (0,) * (nd - 1),
                        pipeline_mode=pl.Buffered(1))


def _bdot(a, b):
    return jnp.dot(a.astype(BF16), b.astype(BF16), preferred_element_type=F32)


def _seg_sum(x, e):
    return sum(jnp.dot(part, e, preferred_element_type=F32) for part in _split_bf16(x, 2))


def _split_bf16(x, parts):
    terms = []
    for _ in range(parts - 1):
        term = x.astype(BF16)
        terms.append(term)
        x = x - term.astype(F32)
    return terms + [x.astype(BF16)]


def _in_proj_body(x_ref, g_ref, w_ref, cos_ref, sin_ref, qan_ref, kvan_ref,
                  wq_ref, wqr_ref, wk_ref, wv_ref, gq_ref, gqr_ref, gk_ref, gkr_ref,
                  gate_ref, rwkv_ref, conv_ref, q_ref, k_ref, v_ref):
    x = x_ref[...]
    ms = jnp.mean(x * x, axis=-1, keepdims=True)
    h = (x * lax.rsqrt(ms + NORM_EPS) * g_ref[...]).astype(BF16)
    o_rwkv = GATE_COLS + MLA_OUT_COLS
    o_conv = o_rwkv + rwkv_ref.shape[-1]
    proj = lambda lo, hi: jnp.dot(h, w_ref[:, lo:hi], preferred_element_type=F32)
    _mla_heads(proj(GATE_COLS, o_rwkv), cos_ref, sin_ref, qan_ref, kvan_ref, wq_ref, wqr_ref, wk_ref, wv_ref,
               gq_ref, gqr_ref, gk_ref, gkr_ref, q_ref, k_ref, v_ref)
    gate_ref[...] = jax.nn.sigmoid(proj(0, GATE_COLS)).astype(gate_ref.dtype)
    rwkv_ref[...] = proj(o_rwkv, o_conv)
    conv_ref[...] = proj(o_conv, w_ref.shape[-1])


def _in_proj(x, cosf, sinf, p, layer):
    b, s, d = x.shape
    tm = ATTN_BLOCK
    per_seq = s // tm
    t = b * s
    row = lambda n: pl.BlockSpec((tm, n), lambda i: (i, 0))
    tok = pl.BlockSpec((1, tm, LANES), lambda i: (i // per_seq, i % per_seq, 0))
    head = lambda w: pl.BlockSpec((1, MLA_HEADS, tm, w), lambda i: (i // per_seq, 0, i % per_seq, 0))
    gain, w_all = p["attn_norm"], p["w_in"]
    mla_consts = (p["qan"], p["kvan"], p["wq"], p["wqr"], p["wk"], p["wv"], p["gq"], p["gqr"], p["gk"], p["gkr"])
    widths = (GATE_COLS, RWKV_COLS + LANES, 3 * CONV_WIDTH)
    assert w_all.shape[-1] == GATE_COLS + MLA_OUT_COLS + widths[1] + widths[2]
    return pl.pallas_call(
        _in_proj_body,
        grid=(t // tm,),
        in_specs=[row(d)] + [_layer_spec(a, layer) for a in (gain, w_all)]
        + [tok, tok] + [_layer_spec(a, layer) for a in mla_consts],
        out_specs=[row(n) for n in widths]
        + [head(HEAD_PAD), head(HEAD_PAD),
           pl.BlockSpec((1, MLA_HEADS, 1, VT_ROWS, tm), lambda i: (i // per_seq, 0, i % per_seq, 0, 0))],
        out_shape=[jax.ShapeDtypeStruct((t, widths[0]), BF16)]
        + [jax.ShapeDtypeStruct((t, n), F32) for n in widths[1:]]
        + [jax.ShapeDtypeStruct((b, MLA_HEADS, s, HEAD_PAD), BF16),
           jax.ShapeDtypeStruct((b, MLA_HEADS, s, HEAD_PAD), BF16),
           jax.ShapeDtypeStruct((b, MLA_HEADS, per_seq, VT_ROWS, tm), BF16)],
        compiler_params=_cparams(("parallel",)),
        name="in_proj",
    )(x.reshape(t, d), gain, w_all, cosf, sinf, *mla_consts)


def _mla_heads(c, cos_ref, sin_ref, qan_ref, kvan_ref, wq_ref, wqr_ref, wk_ref, wv_ref,
               gq_ref, gqr_ref, gk_ref, gkr_ref, q_ref, k_ref, v_ref):
    cq = c[:, :Q_LORA_RANK]
    ckv = c[:, Q_LORA_RANK:Q_LORA_RANK + KV_LORA_RANK]
    kpe = c[:, Q_LORA_RANK + KV_LORA_RANK:Q_LORA_RANK + KV_LORA_RANK + LANES]
    kper = c[:, Q_LORA_RANK + KV_LORA_RANK + LANES:]

    def rms(z, g):
        return (z * lax.rsqrt(jnp.mean(z * z, axis=-1, keepdims=True) + NORM_EPS) * g).astype(BF16)

    cqn = rms(cq, qan_ref[...])
    ckvn = rms(ckv, kvan_ref[...])
    q = jnp.dot(cqn, wq_ref[...], preferred_element_type=F32)
    qr = jnp.dot(cqn, wqr_ref[...], preferred_element_type=F32)
    kn = jnp.dot(ckvn, wk_ref[...], preferred_element_type=F32)
    v = jnp.dot(ckvn, wv_ref[...], preferred_element_type=F32)
    cosf = cos_ref[0]
    sinf = sin_ref[0]
    gqc = gq_ref[...] * cosf
    gqs = gqr_ref[...] * sinf
    gkc = gk_ref[...] * cosf
    kper_s = kper * gkr_ref[...] * sinf
    inv_dim = 1.0 / QK_HEAD_DIM
    vt = v.T
    pad_row = lax.broadcasted_iota(jnp.int32, (VT_ROWS - V_HEAD_DIM, vt.shape[1]), 0)
    ones_rows = jnp.where(pad_row == 0, 1.0, 0.0)
    for h in range(MLA_HEADS):
        sl = slice(h * HEAD_PAD, (h + 1) * HEAD_PAD)
        qh = q[:, sl]
        rq = lax.rsqrt(jnp.sum(qh * qh, axis=-1, keepdims=True) * inv_dim + NORM_EPS)
        q_ref[0, h] = (rq * (qh * gqc + qr[:, sl] * gqs)).astype(q_ref.dtype)
        kh = kn[:, sl] + kpe
        rk = lax.rsqrt(jnp.sum(kh * kh, axis=-1, keepdims=True) * inv_dim + NORM_EPS)
        k_ref[0, h] = (rk * (kh * gkc + kper_s)).astype(k_ref.dtype)
        v_ref[0, h, 0] = jnp.concatenate([vt[h * V_HEAD_DIM:(h + 1) * V_HEAD_DIM, :], ones_rows],
                                         axis=0).astype(v_ref.dtype)


MASK_VALUE = -1e30


ATTN_BLOCK = 256
ATTN_LOOKAHEAD = 4


def _attn_body(q_ref, k_ref, vt_ref, o_ref, m_ref, acc_ref, pend_ref, *, tq):
    i = pl.program_id(1)
    m_ref[...] = jnp.full(m_ref.shape, MASK_VALUE, F32)
    acc_ref[...] = jnp.zeros(acc_ref.shape, F32)
    key_idx = lax.broadcasted_iota(jnp.int32, (tq, tq), 0)
    qry_idx = lax.broadcasted_iota(jnp.int32, (tq, tq), 1)
    causal = key_idx <= qry_idx

    def scores_t(j, h):
        kj = k_ref[0, h, pl.ds(pl.multiple_of(j * tq, tq), tq), :]
        return lax.dot_general(kj, q_ref[0, h], (((1,), (1,)), ((), ())), preferred_element_type=F32)

    for h in range(ATTN_LOOKAHEAD):
        pend_ref[h] = scores_t(0, h)

    def block(j, carry, diagonal):
        pending = [pend_ref[h] for h in range(ATTN_LOOKAHEAD)]
        for h in range(MLA_HEADS):
            st = pending.pop(0)
            ahead = h + ATTN_LOOKAHEAD
            if ahead < MLA_HEADS:
                pending.append(scores_t(j, ahead))
            elif not diagonal:
                pend_ref[ahead - MLA_HEADS] = scores_t(j + 1, ahead - MLA_HEADS)
            if diagonal:
                st = jnp.where(causal, st, MASK_VALUE)
            m_prev = m_ref[h]
            m_new = jnp.maximum(m_prev, jnp.max(st, axis=0, keepdims=True))
            alpha = jnp.exp2(m_prev - m_new)
            pt = jnp.exp2(st - m_new)
            acc_ref[h] = alpha * acc_ref[h] + jnp.dot(vt_ref[0, h, j], pt.astype(BF16), preferred_element_type=F32)
            m_ref[h] = m_new
        return carry

    lax.fori_loop(0, i, functools.partial(block, diagonal=False), 0)
    block(i, 0, diagonal=True)
    for h in range(MLA_HEADS):
        out_t = acc_ref[h, 0:V_HEAD_DIM, :] / acc_ref[h, V_HEAD_DIM:V_HEAD_DIM + 1, :]
        o_ref[0, :, h * V_HEAD_DIM:(h + 1) * V_HEAD_DIM] = out_t.T.astype(o_ref.dtype)


def _attention(q, k, vt):
    b, nh, s, dp = q.shape
    tq = ATTN_BLOCK
    return pl.pallas_call(
        functools.partial(_attn_body, tq=tq),
        grid=(b, s // tq),
        in_specs=[pl.BlockSpec((1, nh, tq, dp), lambda bi, i: (bi, 0, i, 0)),
                  pl.BlockSpec((1, nh, s, dp), lambda bi, i: (bi, 0, 0, 0)),
                  pl.BlockSpec((1, nh, s // tq, VT_ROWS, tq), lambda bi, i: (bi, 0, 0, 0, 0))],
        out_specs=pl.BlockSpec((1, tq, nh * V_HEAD_DIM), lambda bi, i: (bi, i, 0)),
        out_shape=jax.ShapeDtypeStruct((b, s, nh * V_HEAD_DIM), BF16),
        scratch_shapes=[pltpu.VMEM((nh, 1, tq), F32), pltpu.VMEM((nh, VT_ROWS, tq), F32),
                        pltpu.VMEM((ATTN_LOOKAHEAD, tq, tq), F32)],
        compiler_params=_cparams(("parallel", "arbitrary")),
        name="mla_attention",
    )(q, k, vt)


SCAN_VR = 16
SCAN_VQ = RWKV_HEAD_DIM // SCAN_VR
SCAN_TENSORS = 4
SCAN_TC = 64
PREP_TS = LANES


def _rwkv_prep_body(*refs, has_vres, ts, nb):
    if has_vres:
        (x_ref, xp_ref, vf_ref, mu_ref, w0_ref, w2_ref, a0_ref, a2_ref, g2_ref, kk_ref, ka_ref, rk_ref, e_ref, tri_ref,
         v0_ref, v2_ref, xt_ref, val_o, dend_o, g_o, bonus_o, vt_ref) = refs
    else:
        (x_ref, xp_ref, mu_ref, w0_ref, w2_ref, a0_ref, a2_ref, g2_ref, kk_ref, ka_ref, rk_ref, e_ref, tri_ref,
         xt_ref, val_o, dend_o, g_o, bonus_o, vfirst_o, vt_ref) = refs
    i = pl.program_id(0)
    bh = nb * RWKV_HEADS

    def per_batch(b, carry):
        x = x_ref[b]
        prev = jnp.where(i > 0, xp_ref[b][7:8, :], 0.0)
        row = lax.broadcasted_iota(jnp.int32, (ts, 1), 0)
        shifted = jnp.where(row == 0, prev, pltpu.roll(x, 1, axis=0))
        xs = x + (shifted - x) * mu_ref[...]
        wd = RWKV_WIDTH
        r = xs[:, 0:wd]
        k = xs[:, wd:2 * wd]
        v = xs[:, 2 * wd:3 * wd]
        lora_in = xs[:, 3 * wd:3 * wd + LANES]
        xg = xs[:, 3 * wd + LANES:3 * wd + 2 * LANES]
        e = e_ref[...]
        zw = w0_ref[...] + _bdot(jnp.tanh(lora_in), w2_ref[...])
        nz = -zw
        softplus = jnp.maximum(nz, 0.0) + jnp.log(1.0 + jnp.exp(-jnp.abs(nz)))
        log_decay = -jnp.exp(-softplus - 0.5)
        tri = tri_ref[...]
        log_d = sum(jnp.dot(tri, part, preferred_element_type=F32) for part in _split_bf16(log_decay, 3))
        d_incl = jnp.exp(log_d)
        d_prev = jnp.exp(log_d - log_decay)
        d_inv = jnp.exp(-log_d)
        ends = [d_incl[(c + 1) * SCAN_TC - 1:(c + 1) * SCAN_TC, :] for c in range(ts // SCAN_TC)]
        dend_o[0, pl.ds(b, 1), :] = jnp.concatenate(ends, axis=1)
        a_lr = jax.nn.sigmoid(a0_ref[...] + _bdot(lora_in, a2_ref[...]))
        g_o[b] = _bdot(jax.nn.sigmoid(xg), g2_ref[...])
        if has_vres:
            xvs = xs[:, RWKV_COLS:RWKV_COLS + LANES]
            v = v + (vf_ref[b] - v) * jax.nn.sigmoid(v0_ref[...] + _bdot(xvs, v2_ref[...]))
        else:
            vfirst_o[b] = v
        kk = k * kk_ref[...]
        norm = jnp.sqrt(_seg_sum(kk * kk, e))
        kk = kk / jnp.maximum(norm, 1e-12)
        k = k * (1.0 + (a_lr - 1.0) * ka_ref[...])
        bonus_o[b] = _seg_sum(r * k * rk_ref[...], e) * v
        rows = pl.ds(pl.multiple_of(b * wd, wd), wd)
        for idx, val in enumerate((-kk * d_prev, kk * a_lr * d_inv, k * d_inv, r * d_incl)):
            xt_ref[idx, rows, :] = val.T
        vt_ref[rows, :] = v.T
        return carry

    lax.fori_loop(0, nb, per_batch, 0)
    for vr in range(SCAN_VR):
        slab = jnp.concatenate(
            [vt_ref[pl.ds(vq * SCAN_VR + vr, bh, stride=RWKV_HEAD_DIM), :] for vq in range(SCAN_VQ)], axis=0)
        val_o[pl.ds(vr, ts, stride=SCAN_VR), :] = slab.T


def _rwkv_prep(rwkv_cols, v_first, p, seg, layer):
    b, s, n = rwkv_cols.shape
    ts = PREP_TS
    has_vres = v_first is not None
    tok = lambda w: pl.BlockSpec((b, ts, w), lambda i: (0, i, 0))
    halo = pl.BlockSpec((b, 8, n), lambda i: (0, jnp.maximum(i * (ts // 8) - 1, 0), 0))
    step = np.arange(ts)
    same_chunk = (step[:, None] // SCAN_TC) == (step[None, :] // SCAN_TC)
    chunk_tri = jnp.asarray(same_chunk & (step[None, :] <= step[:, None]), BF16)
    layer_consts = [p["mu"], p["w0"], p["w2"], p["a0"], p["a2"], p["g2"], p["k_k"], p["k_a"], p["r_k"]]
    args = [rwkv_cols, rwkv_cols]
    in_specs = [tok(n), halo]
    if has_vres:
        args.append(v_first)
        in_specs.append(tok(RWKV_WIDTH))
    in_specs += [_layer_spec(a, layer) for a in layer_consts] + [_const_spec(seg.shape), _const_spec(chunk_tri.shape)]
    consts = layer_consts + [seg, chunk_tri]
    if has_vres:
        in_specs += [_layer_spec(p["v0"], layer), _layer_spec(p["v2"], layer)]
        consts += [p["v0"], p["v2"]]
    n_tok_out = 2 if has_vres else 3
    return pl.pallas_call(
        functools.partial(_rwkv_prep_body, has_vres=has_vres, ts=ts, nb=b),
        grid=(s // ts,),
        in_specs=in_specs,
        out_specs=[pl.BlockSpec((SCAN_TENSORS, b * RWKV_WIDTH, ts), lambda i: (0, 0, i)),
                   pl.BlockSpec((ts * SCAN_VR, LANES), lambda i: (i, 0)),
                   pl.BlockSpec((1, b, SCAN_SUB * RWKV_WIDTH), lambda i: (i, 0, 0))] + [tok(RWKV_WIDTH)] * n_tok_out,
        out_shape=[jax.ShapeDtypeStruct((SCAN_TENSORS, b * RWKV_WIDTH, s), F32),
                   jax.ShapeDtypeStruct((s * SCAN_VR, LANES), F32),
                   jax.ShapeDtypeStruct((s // ts, b, SCAN_SUB * RWKV_WIDTH), F32)]
        + [jax.ShapeDtypeStruct((b, s, RWKV_WIDTH), F32)] * n_tok_out,
        scratch_shapes=[pltpu.VMEM((b * RWKV_WIDTH, ts), F32)],
        compiler_params=_cparams(("parallel",)),
        name="rwkv_prep",
    )(*args, *consts)


N_ACC = 4
SCAN_SUB = PREP_TS // SCAN_TC


SCAN_UNROLL = 16
SLABS_PER_TILE = SCAN_TENSORS * RWKV_HEAD_DIM
SLABS_PER_BODY = SLABS_PER_TILE * SCAN_UNROLL // PREP_TS
BODIES_PER_TENSOR = RWKV_HEAD_DIM // SLABS_PER_BODY


def _scan_body(xt_ref, v_ref, dprev_ref, dcur_ref, y_ref, s_ref, stage_ref, ych_ref, yt_ref, *, nb):
    g = pl.program_id(0)
    bh = nb * RWKV_HEADS
    wslot = g % 2
    rslot = 1 - wslot

    def stage_slabs(body_idx):
        tensor = body_idx // BODIES_PER_TENSOR
        key0 = (body_idx % BODIES_PER_TENSOR) * SLABS_PER_BODY
        for j in range(SLABS_PER_BODY):
            rows = xt_ref[tensor, pl.ds(key0 + j, bh, stride=RWKV_HEAD_DIM), :]
            stage_ref[wslot, tensor, key0 + j] = jnp.concatenate([rows] * SCAN_VQ, axis=0).T

    @pl.when(g == 0)
    def _():
        s_ref[...] = jnp.zeros_like(s_ref)

        def warm(body_idx, carry):
            stage_slabs(body_idx)
            return carry

        lax.fori_loop(0, SLABS_PER_TILE // SLABS_PER_BODY, warm, 0)

    def tree(acc):
        return (acc[0] + acc[1]) + (acc[2] + acc[3])

    def accumulate(acc, kk, term):
        acc[kk % N_ACC] = term if acc[kk % N_ACC] is None else acc[kk % N_ACC] + term

    def scan_chunk(chunk, dend_ref):
        t0 = chunk * SCAN_TC

        def operand(idx, kk, t):
            return stage_ref[rslot, idx, kk, pl.ds(t, 1), :]

        acc = [None] * N_ACC
        for kk in range(RWKV_HEAD_DIM):
            s_start = s_ref[kk] * dend_ref[0, pl.ds(kk, 1), :]
            s_ref[kk] = s_start
            accumulate(acc, kk, s_start * operand(0, kk, t0))

        def step(t, u):
            op = lambda idx, kk: operand(idx, kk, t)
            t_next = jnp.minimum(t + 1, t0 + SCAN_TC - 1)
            v = v_ref[pl.ds(pl.multiple_of(t * SCAN_VR, SCAN_VR), SCAN_VR), :]
            yacc = [None] * N_ACC
            uacc = [None] * N_ACC
            for kk in range(RWKV_HEAD_DIM):
                s_new = s_ref[kk] + (u * op(1, kk) + v * op(2, kk))
                s_ref[kk] = s_new
                accumulate(yacc, kk, s_new * op(3, kk))
                accumulate(uacc, kk, s_new * operand(0, kk, t_next))
            ych_ref[pl.ds(pl.multiple_of(t * SCAN_VR, SCAN_VR), SCAN_VR), :] = tree(yacc)
            return tree(uacc)

        def body(body_idx, u):
            stage_slabs(chunk * (SCAN_TC // SCAN_UNROLL) + body_idx)
            for j in range(SCAN_UNROLL):
                u = step(t0 + body_idx * SCAN_UNROLL + j, u)
            return u

        lax.fori_loop(0, SCAN_TC // SCAN_UNROLL, body, tree(acc))

    @pl.when(g > 0)
    def _():
        scan_chunk(0, dprev_ref)
        scan_chunk(1, dcur_ref)
        for vr in range(SCAN_VR):
            lanes_by_t = ych_ref[pl.ds(vr, PREP_TS, stride=SCAN_VR), :].T
            for vq in range(SCAN_VQ):
                yt_ref[pl.ds(vq * SCAN_VR + vr, bh, stride=RWKV_HEAD_DIM), :] = lanes_by_t[vq * bh:(vq + 1) * bh, :]
        for b in range(nb):
            y_ref[b] = yt_ref[b * RWKV_WIDTH:(b + 1) * RWKV_WIDTH, :].T


def _wkv_scan(xt, val, dend, nb):
    assert SCAN_SUB == 2
    s = xt.shape[-1]
    n_tiles = s // PREP_TS
    prev_tile = lambda g: jnp.maximum(g - 1, 0)
    return pl.pallas_call(
        functools.partial(_scan_body, nb=nb),
        grid=(n_tiles + 1,),
        in_specs=[
            pl.BlockSpec((SCAN_TENSORS, nb * RWKV_WIDTH, PREP_TS), lambda g: (0, 0, jnp.minimum(g, n_tiles - 1))),
            pl.BlockSpec((PREP_TS * SCAN_VR, LANES), lambda g: (prev_tile(g), 0)),
            pl.BlockSpec((1, RWKV_HEAD_DIM, LANES), lambda g: (jnp.maximum(prev_tile(g) * SCAN_SUB - 1, 0), 0, 0)),
            pl.BlockSpec((1, RWKV_HEAD_DIM, LANES), lambda g: (prev_tile(g) * SCAN_SUB, 0, 0))],
        out_specs=pl.BlockSpec((nb, PREP_TS, RWKV_WIDTH), lambda g: (0, prev_tile(g), 0)),
        out_shape=jax.ShapeDtypeStruct((nb, s, RWKV_WIDTH), F32),
        scratch_shapes=[pltpu.VMEM((RWKV_HEAD_DIM, SCAN_VR, LANES), F32),
                        pltpu.VMEM((2, SCAN_TENSORS, RWKV_HEAD_DIM, PREP_TS, LANES), F32),
                        pltpu.VMEM((PREP_TS * SCAN_VR, LANES), F32),
                        pltpu.VMEM((nb * RWKV_WIDTH, PREP_TS), F32)],
        compiler_params=_cparams(("arbitrary",)),
        name="wkv_scan",
    )(xt, val, dend, dend)


def _merge_body(x_ref, gate_ref, att_ref, y_ref, g_ref, bonus_ref, conv_ref, convp_ref,
                lnw_ref, lnb_ref, e_ref, cw_ref, wa_ref, wb_ref, wc_ref, wo_ref, out_ref, *, ts):
    i = pl.program_id(1)
    e = e_ref[...]
    y = y_ref[0]
    inv_n = 1.0 / RWKV_HEAD_DIM
    mean = _seg_sum(y, e) * inv_n
    d = y - mean
    var = _seg_sum(d * d, e) * inv_n
    yn = d * lax.rsqrt(var + GN_EPS) * lnw_ref[...] + lnb_ref[...] + bonus_ref[0]
    ob = _bdot(yn * g_ref[0], wb_ref[...])

    cw = CONV_WIDTH
    c = conv_ref[0]
    u = c[:, cw:2 * cw] * c[:, 2 * cw:3 * cw]
    cp = convp_ref[0]
    up = jnp.where(i > 0, cp[:, cw:2 * cw] * cp[:, 2 * cw:3 * cw], 0.0)
    p6 = up[6:7, :]
    p7 = up[7:8, :]
    row = lax.broadcasted_iota(jnp.int32, (ts, 1), 0)
    u1 = jnp.where(row == 0, p7, pltpu.roll(u, 1, axis=0))
    u2 = jnp.where(row == 0, p6, jnp.where(row == 1, p7, pltpu.roll(u, 2, axis=0)))
    taps = cw_ref[...]
    yc = taps[0:1, :] * u2 + taps[1:2, :] * u1 + taps[2:3, :] * u
    oc = _bdot(c[:, 0:cw] * yc, wc_ref[...])

    oa = jnp.dot(att_ref[0], wa_ref[...], preferred_element_type=F32)
    gates = gate_ref[0].astype(F32)
    dm = D_MODEL
    merged = gates[:, 0:dm] * oa + gates[:, dm:2 * dm] * ob + gates[:, 2 * dm:3 * dm] * oc
    out_ref[0] = x_ref[0] + _bdot(merged, wo_ref[...])


def _merge(x, gates, att, y, g, bonus, conv_cols, p, seg, layer, ts=512):
    b, s, d = x.shape
    tok = lambda w: pl.BlockSpec((1, ts, w), lambda bi, i: (bi, i, 0))
    nconv = conv_cols.shape[-1]
    halo = pl.BlockSpec((1, 8, nconv), lambda bi, i: (bi, jnp.maximum(i * (ts // 8) - 1, 0), 0))
    consts = (p["ln_w"], p["ln_b"], seg, p["conv_w"], p["mla_w_o"], p["rwkv_w_o"], p["conv_w_o"], p["w_out"])
    const_specs = [_const_spec(a.shape) if a is seg else _layer_spec(a, layer) for a in consts]
    return pl.pallas_call(
        functools.partial(_merge_body, ts=ts),
        grid=(b, s // ts),
        in_specs=[tok(d), tok(GATE_COLS), tok(att.shape[-1]), tok(RWKV_WIDTH), tok(RWKV_WIDTH), tok(RWKV_WIDTH),
                  tok(nconv), halo] + const_specs,
        out_specs=tok(d),
        out_shape=jax.ShapeDtypeStruct((b, s, d), F32),
        compiler_params=_cparams(("parallel", "parallel")),
        name="branch_merge",
    )(x, gates, att, y, g, bonus, conv_cols, conv_cols, *consts)


def _mlp_body(x_ref, g_ref, wu_ref, wd_ref, o_ref):
    x = x_ref[...]
    ms = jnp.mean(x * x, axis=-1, keepdims=True)
    h = (x * lax.rsqrt(ms + NORM_EPS) * g_ref[...]).astype(BF16)
    up = jnp.dot(h, wu_ref[...], preferred_element_type=F32)
    act = jnp.square(jnp.maximum(up, 0.0)).astype(BF16)
    o_ref[...] = x + jnp.dot(act, wd_ref[...], preferred_element_type=F32)


def _mlp(x2d, p, layer, tm=512):
    t, d = x2d.shape
    row = pl.BlockSpec((tm, d), lambda i: (i, 0))
    consts = (p["mlp_norm"], p["w_up"], p["w_down"])
    return pl.pallas_call(
        _mlp_body,
        grid=(t // tm,),
        in_specs=[row] + [_layer_spec(a, layer) for a in consts],
        out_specs=row,
        out_shape=jax.ShapeDtypeStruct((t, d), F32),
        compiler_params=_cparams(("parallel",)),
        name="mlp",
    )(x2d, *consts)


def _rope_partner_cols(w):
    half = QK_ROPE_DIM // 2
    return jnp.concatenate([-w[..., half:], w[..., :half]], axis=-1)


def _pad_lanes(w, lo, total=HEAD_PAD):
    n = w.shape[-1]
    pad = [(0, 0)] * (w.ndim - 1) + [(lo, total - lo - n)]
    return jnp.pad(w, pad)


def _stacked_params(attn_norm, w_in, mla_q_a_norm, mla_wq_b, mla_kv_a_norm, mla_wkv_b, mla_q_norm, mla_k_norm,
                    mla_w_o, rwkv_mu, rwkv_w0, rwkv_w2, rwkv_a0, rwkv_a2, rwkv_g2, rwkv_k_k, rwkv_k_a, rwkv_r_k,
                    rwkv_ln_w, rwkv_ln_b, rwkv_w_o, rwkv_v1, rwkv_v_mu, rwkv_v0, rwkv_v2, conv_w, conv_w_o, w_out,
                    mlp_norm, w_up, w_down):
    p = {}
    n_layers = w_in.shape[0]
    row = lambda a: a.reshape(n_layers, 1, -1).astype(F32)
    first_layer_zeros = lambda a: jnp.pad(a, [(1, 0)] + [(0, 0)] * (a.ndim - 1))
    w = w_in
    o_mla = GATE_COLS
    o_rwkv = o_mla + MLA_COLS
    o_conv = o_rwkv + RWKV_COLS
    p["attn_norm"] = row(attn_norm)
    w_kpe = w[..., o_mla + Q_LORA_RANK + KV_LORA_RANK:o_rwkv]
    p["w_in"] = jnp.concatenate(
        [w[..., :o_mla + Q_LORA_RANK + KV_LORA_RANK], _pad_lanes(w_kpe, QK_NOPE_DIM),
         _pad_lanes(_rope_partner_cols(w_kpe), QK_NOPE_DIM),
         w[..., o_rwkv:o_conv], _pad_lanes(first_layer_zeros(rwkv_v1), 0), w[..., o_conv:]], axis=-1).astype(BF16)
    p["mu"] = row(jnp.concatenate([rwkv_mu, _pad_lanes(first_layer_zeros(rwkv_v_mu), 0)], axis=-1))
    p["v0"] = row(first_layer_zeros(rwkv_v0))
    p["v2"] = jnp.pad(first_layer_zeros(rwkv_v2), ((0, 0), (0, LANES - MV_LORA), (0, 0)))

    scale = QK_HEAD_DIM ** -0.5 * float(np.log2(np.e))
    wq = mla_wq_b.reshape(n_layers, Q_LORA_RANK, MLA_HEADS, QK_HEAD_DIM)
    p["wq"] = _pad_lanes(wq, 0).reshape(n_layers, Q_LORA_RANK, -1).astype(BF16)
    p["wqr"] = _pad_lanes(_rope_partner_cols(wq[..., QK_NOPE_DIM:]), QK_NOPE_DIM).reshape(
        n_layers, Q_LORA_RANK, -1).astype(BF16)
    wkv = mla_wkv_b.reshape(n_layers, KV_LORA_RANK, MLA_HEADS, QK_NOPE_DIM + V_HEAD_DIM)
    p["wk"] = _pad_lanes(wkv[..., :QK_NOPE_DIM], 0).reshape(n_layers, KV_LORA_RANK, -1).astype(BF16)
    p["wv"] = wkv[..., QK_NOPE_DIM:].reshape(n_layers, KV_LORA_RANK, -1).astype(BF16)
    p["qan"] = row(mla_q_a_norm)
    p["kvan"] = row(mla_kv_a_norm)
    swap = lambda g: jnp.concatenate([g[..., QK_ROPE_DIM // 2:], g[..., :QK_ROPE_DIM // 2]], axis=-1)
    gq, gk = mla_q_norm * scale, mla_k_norm
    p["gq"] = row(_pad_lanes(gq, 0))
    p["gqr"] = row(_pad_lanes(swap(gq[..., QK_NOPE_DIM:]), QK_NOPE_DIM))
    p["gk"] = row(_pad_lanes(gk, 0))
    p["gkr"] = row(_pad_lanes(swap(gk[..., QK_NOPE_DIM:]), QK_NOPE_DIM))
    p["mla_w_o"] = mla_w_o.astype(BF16)

    p["w0"] = row(rwkv_w0)
    p["w2"] = jnp.pad(rwkv_w2, ((0, 0), (0, AAA_LORA), (0, 0)))
    p["a0"] = row(rwkv_a0)
    p["a2"] = jnp.pad(rwkv_a2, ((0, 0), (DECAY_LORA, 0), (0, 0)))
    p["g2"] = rwkv_g2
    p["k_k"] = row(rwkv_k_k)
    p["k_a"] = row(rwkv_k_a)
    p["r_k"] = row(rwkv_r_k)
    p["ln_w"] = row(rwkv_ln_w)
    p["ln_b"] = row(rwkv_ln_b)
    p["rwkv_w_o"] = rwkv_w_o.astype(BF16)
    p["conv_w"] = conv_w.astype(F32)
    p["conv_w_o"] = conv_w_o.astype(BF16)
    p["w_out"] = w_out.astype(BF16)
    p["mlp_norm"] = row(mlp_norm)
    p["w_up"] = w_up.astype(BF16)
    p["w_down"] = w_down.astype(BF16)
    return p


def _segment_matrix():
    head_of = np.arange(RWKV_WIDTH) // RWKV_HEAD_DIM
    return jnp.asarray(head_of[:, None] == head_of[None, :], BF16)


def _cos_sin_body(ang_ref, cos_ref, sin_ref):
    ang = ang_ref[...]
    cos_ref[...] = jnp.cos(ang)
    sin_ref[...] = jnp.sin(ang)


def _rope_tables(positions):
    half = QK_ROPE_DIM // 2
    freqs = ROPE_THETA ** (-(jnp.arange(half, dtype=F32) * 2.0 / QK_ROPE_DIM))
    ang = (positions.astype(F32)[..., None] * freqs).reshape(-1, LANES)
    cos, sin = pl.pallas_call(
        _cos_sin_body,
        out_shape=[jax.ShapeDtypeStruct(ang.shape, F32)] * 2,
        name="rope_cos_sin",
    )(ang)
    cos = cos.reshape(positions.shape + (half,))
    sin = sin.reshape(positions.shape + (half,))
    ones = jnp.ones(positions.shape + (QK_NOPE_DIM,), F32)
    tail = jnp.ones(positions.shape + (HEAD_PAD - QK_HEAD_DIM,), F32)
    cosf = jnp.concatenate([ones, cos, cos, tail], axis=-1)
    sinf = jnp.concatenate([0 * ones, sin, sin, 0 * tail], axis=-1)
    return cosf, sinf


def kernel(x, positions, attn_norm, w_in, mla_q_a_norm, mla_wq_b, mla_kv_a_norm, mla_wkv_b, mla_q_norm, mla_k_norm, mla_w_o, rwkv_mu, rwkv_w0, rwkv_w2, rwkv_a0, rwkv_a2, rwkv_g2, rwkv_k_k, rwkv_k_a, rwkv_r_k, rwkv_ln_w, rwkv_ln_b, rwkv_w_o, rwkv_v1, rwkv_v_mu, rwkv_v0, rwkv_v2, conv_w, conv_w_o, w_out, mlp_norm, w_up, w_down):
    weights = (attn_norm, w_in, mla_q_a_norm, mla_wq_b, mla_kv_a_norm, mla_wkv_b, mla_q_norm, mla_k_norm, mla_w_o,
               rwkv_mu, rwkv_w0, rwkv_w2, rwkv_a0, rwkv_a2, rwkv_g2, rwkv_k_k, rwkv_k_a, rwkv_r_k, rwkv_ln_w,
               rwkv_ln_b, rwkv_w_o, rwkv_v1, rwkv_v_mu, rwkv_v0, rwkv_v2, conv_w, conv_w_o, w_out, mlp_norm,
               w_up, w_down)
    b, s, d = x.shape
    cosf, sinf = _rope_tables(positions)
    v_first = None
    p = _stacked_params(*weights)
    seg = _segment_matrix()
    for l in range(DEPTH):
        gates, rwkv_cols, conv_cols, q, k, v = _in_proj(x, cosf, sinf, p, l)
        att = _attention(q, k, v)
        prep = _rwkv_prep(rwkv_cols.reshape(b, s, -1), v_first, p, seg, l)
        scan_xt, scan_val, chunk_decay, g_, bonus = prep[:5]
        if l == 0:
            v_first = prep[5]
        dend = chunk_decay.reshape(-1, b, SCAN_SUB, RWKV_HEADS, RWKV_HEAD_DIM).transpose(0, 2, 4, 1, 3)
        dend = jnp.tile(dend.reshape(-1, RWKV_HEAD_DIM, b * RWKV_HEADS), (1, 1, SCAN_VQ))
        y = _wkv_scan(scan_xt, scan_val, dend, b)
        x = _merge(x, gates.reshape(b, s, -1), att, y, g_, bonus, conv_cols.reshape(b, s, -1), p, seg, l)
        x = _mlp(x.reshape(b * s, d), p, l).reshape(b, s, d)
    return x
```

```python
import functools

import jax
import jax.numpy as jnp
import numpy as np
from jax import lax
from jax.experimental import pallas as pl
from jax.experimental.pallas import tpu as pltpu

D_MODEL = 1024
DEPTH = 2
MLA_HEADS = 8
QK_NOPE_DIM = 64
QK_ROPE_DIM = 32
QK_HEAD_DIM = QK_NOPE_DIM + QK_ROPE_DIM
V_HEAD_DIM = 64
Q_LORA_RANK = 384
KV_LORA_RANK = 256
ROPE_THETA = 10000.0
RWKV_HEAD_DIM = 64
RWKV_HEADS = 4
RWKV_WIDTH = RWKV_HEADS * RWKV_HEAD_DIM
DECAY_LORA = 64
AAA_LORA = 64
GATE_LORA = 128
MV_LORA = 32
GN_EPS = 64e-5
CONV_WIDTH = 256
CONV_K = 3
D_FF = 4 * D_MODEL
N_BRANCH = 3
NORM_EPS = 1e-6
GATE_COLS = N_BRANCH * D_MODEL
MLA_COLS = Q_LORA_RANK + KV_LORA_RANK + QK_ROPE_DIM
RWKV_COLS = 3 * RWKV_WIDTH + DECAY_LORA + AAA_LORA + GATE_LORA

LANES = 128
HEAD_PAD = LANES
BF16_SUBLANES = 16
VT_ROWS = V_HEAD_DIM + BF16_SUBLANES
MLA_OUT_COLS = Q_LORA_RANK + KV_LORA_RANK + 2 * LANES
VMEM_LIMIT = 56 * 1024 * 1024

F32 = jnp.float32
BF16 = jnp.bfloat16


def _cparams(sem):
    return pltpu.CompilerParams(dimension_semantics=sem, vmem_limit_bytes=VMEM_LIMIT)


def _const_spec(shape):
    nd = len(shape)
    return pl.BlockSpec(shape, lambda *_: (0,) * nd, pipeline_mode=pl.Buffered(1))


def _layer_spec(arr, layer):
    nd = arr.ndim
    return pl.BlockSpec((None,) + arr.shape[1:], lambda *_: (layer,) + (0,) * (nd - 1),
                        pipeline_mode=pl.Buffered(1))


def _bdot(a, b):
    return jnp.dot(a.astype(BF16), b.astype(BF16), preferred_element_type=F32)


def _seg_sum(x, e):
    return sum(jnp.dot(part, e, preferred_element_type=F32) for part in _split_bf16(x, 2))


def _split_bf16(x, parts):
    terms = []
    for _ in range(parts - 1):
        term = x.astype(BF16)
        terms.append(term)
        x = x - term.astype(F32)
    return terms + [x.astype(BF16)]


def _in_proj_body(x_ref, g_ref, w_ref, cos_ref, sin_ref, qan_ref, kvan_ref,
                  wq_ref, wqr_ref, wk_ref, wv_ref, gq_ref, gqr_ref, gk_ref, gkr_ref,
                  gate_ref, rwkv_ref, conv_ref, q_ref, k_ref, v_ref):
    x = x_ref[...]
    ms = jnp.mean(x * x, axis=-1, keepdims=True)
    h = (x * lax.rsqrt(ms + NORM_EPS) * g_ref[...]).astype(BF16)
    o_rwkv = GATE_COLS + MLA_OUT_COLS
    o_conv = o_rwkv + rwkv_ref.shape[-1]
    proj = lambda lo, hi: jnp.dot(h, w_ref[:, lo:hi], preferred_element_type=F32)
    _mla_heads(proj(GATE_COLS, o_rwkv), cos_ref, sin_ref, qan_ref, kvan_ref, wq_ref, wqr_ref, wk_ref, wv_ref,
               gq_ref, gqr_ref, gk_ref, gkr_ref, q_ref, k_ref, v_ref)
    gate_ref[...] = jax.nn.sigmoid(proj(0, GATE_COLS)).astype(gate_ref.dtype)
    rwkv_ref[...] = proj(o_rwkv, o_conv)
    conv_ref[...] = proj(o_conv, w_ref.shape[-1])


def _in_proj(x, cosf, sinf, p, layer):
    b, s, d = x.shape
    tm = ATTN_BLOCK
    per_seq = s // tm
    t = b * s
    row = lambda n: pl.BlockSpec((tm, n), lambda i: (i, 0))
    tok = pl.BlockSpec((1, tm, LANES), lambda i: (i // per_seq, i % per_seq, 0))
    head = lambda w: pl.BlockSpec((1, MLA_HEADS, tm, w), lambda i: (i // per_seq, 0, i % per_seq, 0))
    gain, w_all = p["attn_norm"], p["w_in"]
    mla_consts = (p["qan"], p["kvan"], p["wq"], p["wqr"], p["wk"], p["wv"], p["gq"], p["gqr"], p["gk"], p["gkr"])
    widths = (GATE_COLS, RWKV_COLS + LANES, 3 * CONV_WIDTH)
    assert w_all.shape[-1] == GATE_COLS + MLA_OUT_COLS + widths[1] + widths[2]
    return pl.pallas_call(
        _in_proj_body,
        grid=(t // tm,),
        in_specs=[row(d)] + [_layer_spec(a, layer) for a in (gain, w_all)]
        + [tok, tok] + [_layer_spec(a, layer) for a in mla_consts],
        out_specs=[row(n) for n in widths]
        + [head(HEAD_PAD), head(HEAD_PAD),
           pl.BlockSpec((1, MLA_HEADS, 1, VT_ROWS, tm), lambda i: (i // per_seq, 0, i % per_seq, 0, 0))],
        out_shape=[jax.ShapeDtypeStruct((t, widths[0]), BF16)]
        + [jax.ShapeDtypeStruct((t, n), F32) for n in widths[1:]]
        + [jax.ShapeDtypeStruct((b, MLA_HEADS, s, HEAD_PAD), BF16),
           jax.ShapeDtypeStruct((b, MLA_HEADS, s, HEAD_PAD), BF16),
           jax.ShapeDtypeStruct((b, MLA_HEADS, per_seq, VT_ROWS, tm), BF16)],
        compiler_params=_cparams(("parallel",)),
        name="in_proj",
    )(x.reshape(t, d), gain, w_all, cosf, sinf, *mla_consts)


def _mla_heads(c, cos_ref, sin_ref, qan_ref, kvan_ref, wq_ref, wqr_ref, wk_ref, wv_ref,
               gq_ref, gqr_ref, gk_ref, gkr_ref, q_ref, k_ref, v_ref):
    cq = c[:, :Q_LORA_RANK]
    ckv = c[:, Q_LORA_RANK:Q_LORA_RANK + KV_LORA_RANK]
    kpe = c[:, Q_LORA_RANK + KV_LORA_RANK:Q_LORA_RANK + KV_LORA_RANK + LANES]
    kper = c[:, Q_LORA_RANK + KV_LORA_RANK + LANES:]

    def rms(z, g):
        return (z * lax.rsqrt(jnp.mean(z * z, axis=-1, keepdims=True) + NORM_EPS) * g).astype(BF16)

    cqn = rms(cq, qan_ref[...])
    ckvn = rms(ckv, kvan_ref[...])
    q = jnp.dot(cqn, wq_ref[...], preferred_element_type=F32)
    qr = jnp.dot(cqn, wqr_ref[...], preferred_element_type=F32)
    kn = jnp.dot(ckvn, wk_ref[...], preferred_element_type=F32)
    v = jnp.dot(ckvn, wv_ref[...], preferred_element_type=F32)
    cosf = cos_ref[0]
    sinf = sin_ref[0]
    gqc = gq_ref[...] * cosf
    gqs = gqr_ref[...] * sinf
    gkc = gk_ref[...] * cosf
    kper_s = kper * gkr_ref[...] * sinf
    inv_dim = 1.0 / QK_HEAD_DIM
    vt = v.T
    pad_row = lax.broadcasted_iota(jnp.int32, (VT_ROWS - V_HEAD_DIM, vt.shape[1]), 0)
    ones_rows = jnp.where(pad_row == 0, 1.0, 0.0)
    for h in range(MLA_HEADS):
        sl = slice(h * HEAD_PAD, (h + 1) * HEAD_PAD)
        qh = q[:, sl]
        rq = lax.rsqrt(jnp.sum(qh * qh, axis=-1, keepdims=True) * inv_dim + NORM_EPS)
        q_ref[0, h] = (rq * (qh * gqc + qr[:, sl] * gqs)).astype(q_ref.dtype)
        kh = kn[:, sl] + kpe
        rk = lax.rsqrt(jnp.sum(kh * kh, axis=-1, keepdims=True) * inv_dim + NORM_EPS)
        k_ref[0, h] = (rk * (kh * gkc + kper_s)).astype(k_ref.dtype)
        v_ref[0, h, 0] = jnp.concatenate([vt[h * V_HEAD_DIM:(h + 1) * V_HEAD_DIM, :], ones_rows],
                                         axis=0).astype(v_ref.dtype)


MASK_VALUE = -1e30


ATTN_BLOCK = 256
ATTN_LOOKAHEAD = 4


def _attn_body(q_ref, k_ref, vt_ref, o_ref, m_ref, acc_ref, pend_ref, *, tq):
    i = pl.program_id(1)
    m_ref[...] = jnp.full(m_ref.shape, MASK_VALUE, F32)
    acc_ref[...] = jnp.zeros(acc_ref.shape, F32)
    key_idx = lax.broadcasted_iota(jnp.int32, (tq, tq), 0)
    qry_idx = lax.broadcasted_iota(jnp.int32, (tq, tq), 1)
    causal = key_idx <= qry_idx

    def scores_t(j, h):
        kj = k_ref[0, h, pl.ds(pl.multiple_of(j * tq, tq), tq), :]
        return lax.dot_general(kj, q_ref[0, h], (((1,), (1,)), ((), ())), preferred_element_type=F32)

    for h in range(ATTN_LOOKAHEAD):
        pend_ref[h] = scores_t(0, h)

    def block(j, carry, diagonal):
        pending = [pend_ref[h] for h in range(ATTN_LOOKAHEAD)]
        for h in range(MLA_HEADS):
            st = pending.pop(0)
            ahead = h + ATTN_LOOKAHEAD
            if ahead < MLA_HEADS:
                pending.append(scores_t(j, ahead))
            elif not diagonal:
                pend_ref[ahead - MLA_HEADS] = scores_t(j + 1, ahead - MLA_HEADS)
            if diagonal:
                st = jnp.where(causal, st, MASK_VALUE)
            m_prev = m_ref[h]
            m_new = jnp.maximum(m_prev, jnp.max(st, axis=0, keepdims=True))
            alpha = jnp.exp2(m_prev - m_new)
            pt = jnp.exp2(st - m_new)
            acc_ref[h] = alpha * acc_ref[h] + jnp.dot(vt_ref[0, h, j], pt.astype(BF16), preferred_element_type=F32)
            m_ref[h] = m_new
        return carry

    lax.fori_loop(0, i, functools.partial(block, diagonal=False), 0)
    block(i, 0, diagonal=True)
    for h in range(MLA_HEADS):
        out_t = acc_ref[h, 0:V_HEAD_DIM, :] / acc_ref[h, V_HEAD_DIM:V_HEAD_DIM + 1, :]
        o_ref[0, :, h * V_HEAD_DIM:(h + 1) * V_HEAD_DIM] = out_t.T.astype(o_ref.dtype)


def _attention(q, k, vt):
    b, nh, s, dp = q.shape
    tq = ATTN_BLOCK
    return pl.pallas_call(
        functools.partial(_attn_body, tq=tq),
        grid=(b, s // tq),
        in_specs=[pl.BlockSpec((1, nh, tq, dp), lambda bi, i: (bi, 0, i, 0)),
                  pl.BlockSpec((1, nh, s, dp), lambda bi, i: (bi, 0, 0, 0)),
                  pl.BlockSpec((1, nh, s // tq, VT_ROWS, tq), lambda bi, i: (bi, 0, 0, 0, 0))],
        out_specs=pl.BlockSpec((1, tq, nh * V_HEAD_DIM), lambda bi, i: (bi, i, 0)),
        out_shape=jax.ShapeDtypeStruct((b, s, nh * V_HEAD_DIM), BF16),
        scratch_shapes=[pltpu.VMEM((nh, 1, tq), F32), pltpu.VMEM((nh, VT_ROWS, tq), F32),
                        pltpu.VMEM((ATTN_LOOKAHEAD, tq, tq), F32)],
        compiler_params=_cparams(("parallel", "arbitrary")),
        name="mla_attention",
    )(q, k, vt)


SCAN_VR = 16
SCAN_VQ = RWKV_HEAD_DIM // SCAN_VR
SCAN_TENSORS = 4
SCAN_TC = 64
PREP_TS = LANES


def _rwkv_prep_body(*refs, has_vres, ts, nb):
    if has_vres:
        (x_ref, xp_ref, vf_ref, mu_ref, w0_ref, w2_ref, a0_ref, a2_ref, g2_ref, kk_ref, ka_ref, rk_ref, e_ref, tri_ref,
         v0_ref, v2_ref, xt_ref, val_o, dend_o, g_o, bonus_o, vt_ref) = refs
    else:
        (x_ref, xp_ref, mu_ref, w0_ref, w2_ref, a0_ref, a2_ref, g2_ref, kk_ref, ka_ref, rk_ref, e_ref, tri_ref,
         xt_ref, val_o, dend_o, g_o, bonus_o, vfirst_o, vt_ref) = refs
    i = pl.program_id(0)
    bh = nb * RWKV_HEADS

    def per_batch(b, carry):
        x = x_ref[b]
        prev = jnp.where(i > 0, xp_ref[b][7:8, :], 0.0)
        row = lax.broadcasted_iota(jnp.int32, (ts, 1), 0)
        shifted = jnp.where(row == 0, prev, pltpu.roll(x, 1, axis=0))
        xs = x + (shifted - x) * mu_ref[...]
        wd = RWKV_WIDTH
        r = xs[:, 0:wd]
        k = xs[:, wd:2 * wd]
        v = xs[:, 2 * wd:3 * wd]
        lora_in = xs[:, 3 * wd:3 * wd + LANES]
        xg = xs[:, 3 * wd + LANES:3 * wd + 2 * LANES]
        e = e_ref[...]
        zw = w0_ref[...] + _bdot(jnp.tanh(lora_in), w2_ref[...])
        nz = -zw
        softplus = jnp.maximum(nz, 0.0) + jnp.log(1.0 + jnp.exp(-jnp.abs(nz)))
        log_decay = -jnp.exp(-softplus - 0.5)
        tri = tri_ref[...]
        log_d = sum(jnp.dot(tri, part, preferred_element_type=F32) for part in _split_bf16(log_decay, 3))
        d_incl = jnp.exp(log_d)
        d_prev = jnp.exp(log_d - log_decay)
        d_inv = jnp.exp(-log_d)
        ends = [d_incl[(c + 1) * SCAN_TC - 1:(c + 1) * SCAN_TC, :] for c in range(ts // SCAN_TC)]
        dend_o[0, pl.ds(b, 1), :] = jnp.concatenate(ends, axis=1)
        a_lr = jax.nn.sigmoid(a0_ref[...] + _bdot(lora_in, a2_ref[...]))
        g_o[b] = _bdot(jax.nn.sigmoid(xg), g2_ref[...])
        if has_vres:
            xvs = xs[:, RWKV_COLS:RWKV_COLS + LANES]
            v = v + (vf_ref[b] - v) * jax.nn.sigmoid(v0_ref[...] + _bdot(xvs, v2_ref[...]))
        else:
            vfirst_o[b] = v
        kk = k * kk_ref[...]
        norm = jnp.sqrt(_seg_sum(kk * kk, e))
        kk = kk / jnp.maximum(norm, 1e-12)
        k = k * (1.0 + (a_lr - 1.0) * ka_ref[...])
        bonus_o[b] = _seg_sum(r * k * rk_ref[...], e) * v
        rows = pl.ds(pl.multiple_of(b * wd, wd), wd)
        for idx, val in enumerate((-kk * d_prev, kk * a_lr * d_inv, k * d_inv, r * d_incl)):
            xt_ref[idx, rows, :] = val.T
        vt_ref[rows, :] = v.T
        return carry

    lax.fori_loop(0, nb, per_batch, 0)
    for vr in range(SCAN_VR):
        slab = jnp.concatenate(
            [vt_ref[pl.ds(vq * SCAN_VR + vr, bh, stride=RWKV_HEAD_DIM), :] for vq in range(SCAN_VQ)], axis=0)
        val_o[pl.ds(vr, ts, stride=SCAN_VR), :] = slab.T


def _rwkv_prep(rwkv_cols, v_first, p, seg, layer):
    b, s, n = rwkv_cols.shape
    ts = PREP_TS
    has_vres = v_first is not None
    tok = lambda w: pl.BlockSpec((b, ts, w), lambda i: (0, i, 0))
    halo = pl.BlockSpec((b, 8, n), lambda i: (0, jnp.maximum(i * (ts // 8) - 1, 0), 0))
    step = np.arange(ts)
    same_chunk = (step[:, None] // SCAN_TC) == (step[None, :] // SCAN_TC)
    chunk_tri = jnp.asarray(same_chunk & (step[None, :] <= step[:, None]), BF16)
    layer_consts = [p["mu"], p["w0"], p["w2"], p["a0"], p["a2"], p["g2"], p["k_k"], p["k_a"], p["r_k"]]
    args = [rwkv_cols, rwkv_cols]
    in_specs = [tok(n), halo]
    if has_vres:
        args.append(v_first)
        in_specs.append(tok(RWKV_WIDTH))
    in_specs += [_layer_spec(a, layer) for a in layer_consts] + [_const_spec(seg.shape), _const_spec(chunk_tri.shape)]
    consts = layer_consts + [seg, chunk_tri]
    if has_vres:
        in_specs += [_layer_spec(p["v0"], layer), _layer_spec(p["v2"], layer)]
        consts += [p["v0"], p["v2"]]
    n_tok_out = 2 if has_vres else 3
    return pl.pallas_call(
        functools.partial(_rwkv_prep_body, has_vres=has_vres, ts=ts, nb=b),
        grid=(s // ts,),
        in_specs=in_specs,
        out_specs=[pl.BlockSpec((SCAN_TENSORS, b * RWKV_WIDTH, ts), lambda i: (0, 0, i)),
                   pl.BlockSpec((ts * SCAN_VR, LANES), lambda i: (i, 0)),
                   pl.BlockSpec((1, b, SCAN_SUB * RWKV_WIDTH), lambda i: (i, 0, 0))] + [tok(RWKV_WIDTH)] * n_tok_out,
        out_shape=[jax.ShapeDtypeStruct((SCAN_TENSORS, b * RWKV_WIDTH, s), F32),
                   jax.ShapeDtypeStruct((s * SCAN_VR, LANES), F32),
                   jax.ShapeDtypeStruct((s // ts, b, SCAN_SUB * RWKV_WIDTH), F32)]
        + [jax.ShapeDtypeStruct((b, s, RWKV_WIDTH), F32)] * n_tok_out,
        scratch_shapes=[pltpu.VMEM((b * RWKV_WIDTH, ts), F32)],
        compiler_params=_cparams(("parallel",)),
        name="rwkv_prep",
    )(*args, *consts)


N_ACC = 4
SCAN_SUB = PREP_TS // SCAN_TC


SCAN_UNROLL = 32
SLABS_PER_TILE = SCAN_TENSORS * RWKV_HEAD_DIM
SLABS_PER_BODY = SLABS_PER_TILE * SCAN_UNROLL // PREP_TS
BODIES_PER_TENSOR = RWKV_HEAD_DIM // SLABS_PER_BODY


def _scan_body(xt_ref, v_ref, dprev_ref, dcur_ref, y_ref, s_ref, stage_ref, ych_ref, yt_ref, *, nb):
    g = pl.program_id(0)
    bh = nb * RWKV_HEADS
    wslot = g % 2
    rslot = 1 - wslot

    def stage_slabs(body_idx):
        tensor = body_idx // BODIES_PER_TENSOR
        key0 = (body_idx % BODIES_PER_TENSOR) * SLABS_PER_BODY
        for j in range(SLABS_PER_BODY):
            rows = xt_ref[tensor, pl.ds(key0 + j, bh, stride=RWKV_HEAD_DIM), :]
            stage_ref[wslot, tensor, key0 + j] = jnp.concatenate([rows] * SCAN_VQ, axis=0).T

    @pl.when(g == 0)
    def _():
        s_ref[...] = jnp.zeros_like(s_ref)

        def warm(body_idx, carry):
            stage_slabs(body_idx)
            return carry

        lax.fori_loop(0, SLABS_PER_TILE // SLABS_PER_BODY, warm, 0)

    def tree(acc):
        return (acc[0] + acc[1]) + (acc[2] + acc[3])

    def accumulate(acc, kk, term):
        acc[kk % N_ACC] = term if acc[kk % N_ACC] is None else acc[kk % N_ACC] + term

    def scan_chunk(chunk, dend_ref):
        t0 = chunk * SCAN_TC

        def operand(idx, kk, t):
            return stage_ref[rslot, idx, kk, pl.ds(t, 1), :]

        acc = [None] * N_ACC
        for kk in range(RWKV_HEAD_DIM):
            s_start = s_ref[kk] * dend_ref[0, pl.ds(kk, 1), :]
            s_ref[kk] = s_start
            accumulate(acc, kk, s_start * operand(0, kk, t0))

        def step(t, u):
            op = lambda idx, kk: operand(idx, kk, t)
            t_next = jnp.minimum(t + 1, t0 + SCAN_TC - 1)
            v = v_ref[pl.ds(pl.multiple_of(t * SCAN_VR, SCAN_VR), SCAN_VR), :]
            yacc = [None] * N_ACC
            uacc = [None] * N_ACC
            for kk in range(RWKV_HEAD_DIM):
                s_new = s_ref[kk] + (u * op(1, kk) + v * op(2, kk))
                s_ref[kk] = s_new
                accumulate(yacc, kk, s_new * op(3, kk))
                accumulate(uacc, kk, s_new * operand(0, kk, t_next))
            ych_ref[pl.ds(pl.multiple_of(t * SCAN_VR, SCAN_VR), SCAN_VR), :] = tree(yacc)
            return tree(uacc)

        def body(body_idx, u):
            stage_slabs(chunk * (SCAN_TC // SCAN_UNROLL) + body_idx)
            for j in range(SCAN_UNROLL):
                u = step(t0 + body_idx * SCAN_UNROLL + j, u)
            return u

        lax.fori_loop(0, SCAN_TC // SCAN_UNROLL, body, tree(acc))

    @pl.when(g > 0)
    def _():
        scan_chunk(0, dprev_ref)
        scan_chunk(1, dcur_ref)
        for vr in range(SCAN_VR):
            lanes_by_t = ych_ref[pl.ds(vr, PREP_TS, stride=SCAN_VR), :].T
            for vq in range(SCAN_VQ):
                yt_ref[pl.ds(vq * SCAN_VR + vr, bh, stride=RWKV_HEAD_DIM), :] = lanes_by_t[vq * bh:(vq + 1) * bh, :]
        for b in range(nb):
            y_ref[b] = yt_ref[b * RWKV_WIDTH:(b + 1) * RWKV_WIDTH, :].T


def _wkv_scan(xt, val, dend, nb):
    assert SCAN_SUB == 2
    s = xt.shape[-1]
    n_tiles = s // PREP_TS
    prev_tile = lambda g: jnp.maximum(g - 1, 0)
    return pl.pallas_call(
        functools.partial(_scan_body, nb=nb),
        grid=(n_tiles + 1,),
        in_specs=[
            pl.BlockSpec((SCAN_TENSORS, nb * RWKV_WIDTH, PREP_TS), lambda g: (0, 0, jnp.minimum(g, n_tiles - 1))),
            pl.BlockSpec((PREP_TS * SCAN_VR, LANES), lambda g: (prev_tile(g), 0)),
            pl.BlockSpec((1, RWKV_HEAD_DIM, LANES), lambda g: (jnp.maximum(prev_tile(g) * SCAN_SUB - 1, 0), 0, 0)),
            pl.BlockSpec((1, RWKV_HEAD_DIM, LANES), lambda g: (prev_tile(g) * SCAN_SUB, 0, 0))],
        out_specs=pl.BlockSpec((nb, PREP_TS, RWKV_WIDTH), lambda g: (0, prev_tile(g), 0)),
        out_shape=jax.ShapeDtypeStruct((nb, s, RWKV_WIDTH), F32),
        scratch_shapes=[pltpu.VMEM((RWKV_HEAD_DIM, SCAN_VR, LANES), F32),
                        pltpu.VMEM((2, SCAN_TENSORS, RWKV_HEAD_DIM, PREP_TS, LANES), F32),
                        pltpu.VMEM((PREP_TS * SCAN_VR, LANES), F32),
                        pltpu.VMEM((nb * RWKV_WIDTH, PREP_TS), F32)],
        compiler_params=_cparams(("arbitrary",)),
        name="wkv_scan",
    )(xt, val, dend, dend)


def _merge_body(x_ref, gate_ref, att_ref, y_ref, g_ref, bonus_ref, conv_ref, convp_ref,
                lnw_ref, lnb_ref, e_ref, cw_ref, wa_ref, wb_ref, wc_ref, wo_ref, out_ref, *, ts):
    i = pl.program_id(1)
    e = e_ref[...]
    y = y_ref[0]
    inv_n = 1.0 / RWKV_HEAD_DIM
    mean = _seg_sum(y, e) * inv_n
    d = y - mean
    var = _seg_sum(d * d, e) * inv_n
    yn = d * lax.rsqrt(var + GN_EPS) * lnw_ref[...] + lnb_ref[...] + bonus_ref[0]
    ob = _bdot(yn * g_ref[0], wb_ref[...])

    cw = CONV_WIDTH
    c = conv_ref[0]
    u = c[:, cw:2 * cw] * c[:, 2 * cw:3 * cw]
    cp = convp_ref[0]
    up = jnp.where(i > 0, cp[:, cw:2 * cw] * cp[:, 2 * cw:3 * cw], 0.0)
    p6 = up[6:7, :]
    p7 = up[7:8, :]
    row = lax.broadcasted_iota(jnp.int32, (ts, 1), 0)
    u1 = jnp.where(row == 0, p7, pltpu.roll(u, 1, axis=0))
    u2 = jnp.where(row == 0, p6, jnp.where(row == 1, p7, pltpu.roll(u, 2, axis=0)))
    taps = cw_ref[...]
    yc = taps[0:1, :] * u2 + taps[1:2, :] * u1 + taps[2:3, :] * u
    oc = _bdot(c[:, 0:cw] * yc, wc_ref[...])

    oa = jnp.dot(att_ref[0], wa_ref[...], preferred_element_type=F32)
    gates = gate_ref[0].astype(F32)
    dm = D_MODEL
    merged = gates[:, 0:dm] * oa + gates[:, dm:2 * dm] * ob + gates[:, 2 * dm:3 * dm] * oc
    out_ref[0] = x_ref[0] + _bdot(merged, wo_ref[...])


def _merge(x, gates, att, y, g, bonus, conv_cols, p, seg, layer, ts=512):
    b, s, d = x.shape
    tok = lambda w: pl.BlockSpec((1, ts, w), lambda bi, i: (bi, i, 0))
    nconv = conv_cols.shape[-1]
    halo = pl.BlockSpec((1, 8, nconv), lambda bi, i: (bi, jnp.maximum(i * (ts // 8) - 1, 0), 0))
    consts = (p["ln_w"], p["ln_b"], seg, p["conv_w"], p["mla_w_o"], p["rwkv_w_o"], p["conv_w_o"], p["w_out"])
    const_specs = [_const_spec(a.shape) if a is seg else _layer_spec(a, layer) for a in consts]
    return pl.pallas_call(
        functools.partial(_merge_body, ts=ts),
        grid=(b, s // ts),
        in_specs=[tok(d), tok(GATE_COLS), tok(att.shape[-1]), tok(RWKV_WIDTH), tok(RWKV_WIDTH), tok(RWKV_WIDTH),
                  tok(nconv), halo] + const_specs,
        out_specs=tok(d),
        out_shape=jax.ShapeDtypeStruct((b, s, d), F32),
        compiler_params=_cparams(("parallel", "parallel")),
        name="branch_merge",
    )(x, gates, att, y, g, bonus, conv_cols, conv_cols, *consts)


def _mlp_body(x_ref, g_ref, wu_ref, wd_ref, o_ref):
    x = x_ref[...]
    ms = jnp.mean(x * x, axis=-1, keepdims=True)
    h = (x * lax.rsqrt(ms + NORM_EPS) * g_ref[...]).astype(BF16)
    up = jnp.dot(h, wu_ref[...], preferred_element_type=F32)
    act = jnp.square(jnp.maximum(up, 0.0)).astype(BF16)
    o_ref[...] = x + jnp.dot(act, wd_ref[...], preferred_element_type=F32)


def _mlp(x2d, p, layer, tm=512):
    t, d = x2d.shape
    row = pl.BlockSpec((tm, d), lambda i: (i, 0))
    consts = (p["mlp_norm"], p["w_up"], p["w_down"])
    return pl.pallas_call(
        _mlp_body,
        grid=(t // tm,),
        in_specs=[row] + [_layer_spec(a, layer) for a in consts],
        out_specs=row,
        out_shape=jax.ShapeDtypeStruct((t, d), F32),
        compiler_params=_cparams(("parallel",)),
        name="mlp",
    )(x2d, *consts)


def _rope_partner_cols(w):
    half = QK_ROPE_DIM // 2
    return jnp.concatenate([-w[..., half:], w[..., :half]], axis=-1)


def _pad_lanes(w, lo, total=HEAD_PAD):
    n = w.shape[-1]
    pad = [(0, 0)] * (w.ndim - 1) + [(lo, total - lo - n)]
    return jnp.pad(w, pad)


def _stacked_params(attn_norm, w_in, mla_q_a_norm, mla_wq_b, mla_kv_a_norm, mla_wkv_b, mla_q_norm, mla_k_norm,
                    mla_w_o, rwkv_mu, rwkv_w0, rwkv_w2, rwkv_a0, rwkv_a2, rwkv_g2, rwkv_k_k, rwkv_k_a, rwkv_r_k,
                    rwkv_ln_w, rwkv_ln_b, rwkv_w_o, rwkv_v1, rwkv_v_mu, rwkv_v0, rwkv_v2, conv_w, conv_w_o, w_out,
                    mlp_norm, w_up, w_down):
    p = {}
    n_layers = w_in.shape[0]
    row = lambda a: a.reshape(n_layers, 1, -1).astype(F32)
    first_layer_zeros = lambda a: jnp.pad(a, [(1, 0)] + [(0, 0)] * (a.ndim - 1))
    w = w_in
    o_mla = GATE_COLS
    o_rwkv = o_mla + MLA_COLS
    o_conv = o_rwkv + RWKV_COLS
    p["attn_norm"] = row(attn_norm)
    w_kpe = w[..., o_mla + Q_LORA_RANK + KV_LORA_RANK:o_rwkv]
    p["w_in"] = jnp.concatenate(
        [w[..., :o_mla + Q_LORA_RANK + KV_LORA_RANK], _pad_lanes(w_kpe, QK_NOPE_DIM),
         _pad_lanes(_rope_partner_cols(w_kpe), QK_NOPE_DIM),
         w[..., o_rwkv:o_conv], _pad_lanes(first_layer_zeros(rwkv_v1), 0), w[..., o_conv:]], axis=-1).astype(BF16)
    p["mu"] = row(jnp.concatenate([rwkv_mu, _pad_lanes(first_layer_zeros(rwkv_v_mu), 0)], axis=-1))
    p["v0"] = row(first_layer_zeros(rwkv_v0))
    p["v2"] = jnp.pad(first_layer_zeros(rwkv_v2), ((0, 0), (0, LANES - MV_LORA), (0, 0)))

    scale = QK_HEAD_DIM ** -0.5 * float(np.log2(np.e))
    wq = mla_wq_b.reshape(n_layers, Q_LORA_RANK, MLA_HEADS, QK_HEAD_DIM)
    p["wq"] = _pad_lanes(wq, 0).reshape(n_layers, Q_LORA_RANK, -1).astype(BF16)
    p["wqr"] = _pad_lanes(_rope_partner_cols(wq[..., QK_NOPE_DIM:]), QK_NOPE_DIM).reshape(
        n_layers, Q_LORA_RANK, -1).astype(BF16)
    wkv = mla_wkv_b.reshape(n_layers, KV_LORA_RANK, MLA_HEADS, QK_NOPE_DIM + V_HEAD_DIM)
    p["wk"] = _pad_lanes(wkv[..., :QK_NOPE_DIM], 0).reshape(n_layers, KV_LORA_RANK, -1).astype(BF16)
    p["wv"] = wkv[..., QK_NOPE_DIM:].reshape(n_layers, KV_LORA_RANK, -1).astype(BF16)
    p["qan"] = row(mla_q_a_norm)
    p["kvan"] = row(mla_kv_a_norm)
    swap = lambda g: jnp.concatenate([g[..., QK_ROPE_DIM // 2:], g[..., :QK_ROPE_DIM // 2]], axis=-1)
    gq, gk = mla_q_norm * scale, mla_k_norm
    p["gq"] = row(_pad_lanes(gq, 0))
    p["gqr"] = row(_pad_lanes(swap(gq[..., QK_NOPE_DIM:]), QK_NOPE_DIM))
    p["gk"] = row(_pad_lanes(gk, 0))
    p["gkr"] = row(_pad_lanes(swap(gk[..., QK_NOPE_DIM:]), QK_NOPE_DIM))
    p["mla_w_o"] = mla_w_o.astype(BF16)

    p["w0"] = row(rwkv_w0)
    p["w2"] = jnp.pad(rwkv_w2, ((0, 0), (0, AAA_LORA), (0, 0)))
    p["a0"] = row(rwkv_a0)
    p["a2"] = jnp.pad(rwkv_a2, ((0, 0), (DECAY_LORA, 0), (0, 0)))
    p["g2"] = rwkv_g2
    p["k_k"] = row(rwkv_k_k)
    p["k_a"] = row(rwkv_k_a)
    p["r_k"] = row(rwkv_r_k)
    p["ln_w"] = row(rwkv_ln_w)
    p["ln_b"] = row(rwkv_ln_b)
    p["rwkv_w_o"] = rwkv_w_o.astype(BF16)
    p["conv_w"] = conv_w.astype(F32)
    p["conv_w_o"] = conv_w_o.astype(BF16)
    p["w_out"] = w_out.astype(BF16)
    p["mlp_norm"] = row(mlp_norm)
    p["w_up"] = w_up.astype(BF16)
    p["w_down"] = w_down.astype(BF16)
    return p


def _segment_matrix():
    head_of = np.arange(RWKV_WIDTH) // RWKV_HEAD_DIM
    return jnp.asarray(head_of[:, None] == head_of[None, :], BF16)


def _cos_sin_body(ang_ref, cos_ref, sin_ref):
    ang = ang_ref[...]
    cos_ref[...] = jnp.cos(ang)
    sin_ref[...] = jnp.sin(ang)


def _rope_tables(positions):
    half = QK_ROPE_DIM // 2
    freqs = ROPE_THETA ** (-(jnp.arange(half, dtype=F32) * 2.0 / QK_ROPE_DIM))
    ang = (positions.astype(F32)[..., None] * freqs).reshape(-1, LANES)
    cos, sin = pl.pallas_call(
        _cos_sin_body,
        out_shape=[jax.ShapeDtypeStruct(ang.shape, F32)] * 2,
        name="rope_cos_sin",
    )(ang)
    cos = cos.reshape(positions.shape + (half,))
    sin = sin.reshape(positions.shape + (half,))
    ones = jnp.ones(positions.shape + (QK_NOPE_DIM,), F32)
    tail = jnp.ones(positions.shape + (HEAD_PAD - QK_HEAD_DIM,), F32)
    cosf = jnp.concatenate([ones, cos, cos, tail], axis=-1)
    sinf = jnp.concatenate([0 * ones, sin, sin, 0 * tail], axis=-1)
    return cosf, sinf


def kernel(x, positions, attn_norm, w_in, mla_q_a_norm, mla_wq_b, mla_kv_a_norm, mla_wkv_b, mla_q_norm, mla_k_norm, mla_w_o, rwkv_mu, rwkv_w0, rwkv_w2, rwkv_a0, rwkv_a2, rwkv_g2, rwkv_k_k, rwkv_k_a, rwkv_r_k, rwkv_ln_w, rwkv_ln_b, rwkv_w_o, rwkv_v1, rwkv_v_mu, rwkv_v0, rwkv_v2, conv_w, conv_w_o, w_out, mlp_norm, w_up, w_down):
    weights = (attn_norm, w_in, mla_q_a_norm, mla_wq_b, mla_kv_a_norm, mla_wkv_b, mla_q_norm, mla_k_norm, mla_w_o,
               rwkv_mu, rwkv_w0, rwkv_w2, rwkv_a0, rwkv_a2, rwkv_g2, rwkv_k_k, rwkv_k_a, rwkv_r_k, rwkv_ln_w,
               rwkv_ln_b, rwkv_w_o, rwkv_v1, rwkv_v_mu, rwkv_v0, rwkv_v2, conv_w, conv_w_o, w_out, mlp_norm,
               w_up, w_down)
    b, s, d = x.shape
    cosf, sinf = _rope_tables(positions)
    v_first = None
    p = _stacked_params(*weights)
    seg = _segment_matrix()
    for l in range(DEPTH):
        gates, rwkv_cols, conv_cols, q, k, v = _in_proj(x, cosf, sinf, p, l)
        att = _attention(q, k, v)
        prep = _rwkv_prep(rwkv_cols.reshape(b, s, -1), v_first, p, seg, l)
        scan_xt, scan_val, chunk_decay, g_, bonus = prep[:5]
        if l == 0:
            v_first = prep[5]
        dend = chunk_decay.reshape(-1, b, SCAN_SUB, RWKV_HEADS, RWKV_HEAD_DIM).transpose(0, 2, 4, 1, 3)
        dend = jnp.tile(dend.reshape(-1, RWKV_HEAD_DIM, b * RWKV_HEADS), (1, 1, SCAN_VQ))
        y = _wkv_scan(scan_xt, scan_val, dend, b)
        x = _merge(x, gates.reshape(b, s, -1), att, y, g_, bonus, conv_cols.reshape(b, s, -1), p, seg, l)
        x = _mlp(x.reshape(b * s, d), p, l).reshape(b, s, d)
    return x
```
